```python
import math
import jax, jax.numpy as jnp
from jax import lax
import numpy as np

D_MODEL = 2048
BATCH = 1
SEQ = 16384
DEPTH = 4
DEC_BATCH = 2
DEC_SEQ = 4096
PAST_LEN = 128

HEAD_DIM = 128
N_HEADS_A = 8
N_HEADS_B = 8
D_A = N_HEADS_A * HEAD_DIM
D_B = N_HEADS_B * HEAD_DIM
D_MIX = D_A + D_B
QK_SCALE = HEAD_DIM ** -0.5
DIL_PATTERNS = ((128, 1), (512, 4), (2048, 16))
N_BUCKETS = 32
MAX_DISTANCE = 1024
GRID_W = 64
NA_ROWS = 8
NA_COLS = 16
D_FF = -(-8 * D_MODEL // (3 * 256)) * 256
EPS = 1e-6

kernel_name = "hymba_dilated_natten_encoder"


def rms_norm(x, g):
    xf = x.astype(jnp.float32)
    y = xf * lax.rsqrt(jnp.mean(xf * xf, axis=-1, keepdims=True) + EPS)
    return (y * g.astype(jnp.float32)).astype(x.dtype)


def t5_bucket(rel):
    nb = N_BUCKETS // 2
    exact = nb // 2
    n = np.abs(rel)
    sign = np.where(rel > 0, nb, 0)
    large = exact + (np.log(np.maximum(n, 1) / exact) / math.log(MAX_DISTANCE / exact) * (nb - exact)).astype(np.int64)
    large = np.minimum(large, nb - 1)
    return (sign + np.where(n < exact, n, large)).astype(np.int32)


def banded_attention_stats(q, k, v, bias, radius):
    blk = radius
    B, H, G, L, dh = q.shape
    nblk = -(-L // blk)
    Lp = nblk * blk
    pad = Lp - L
    qb = jnp.pad(q, ((0, 0),) * 3 + ((0, pad), (0, 0))).reshape(B, H, G, nblk, blk, dh)
    padk = ((0, 0),) * 3 + ((blk, pad + blk), (0, 0))
    kp = jnp.pad(k, padk)
    vp = jnp.pad(v, padk)

    def windows(a):
        return jnp.concatenate([a[..., j * blk:j * blk + Lp, :].reshape(B, H, G, nblk, blk, dh) for j in range(3)], axis=-2)

    kb, vb = windows(kp), windows(vp)
    i = np.arange(blk)[:, None]
    j = np.arange(3 * blk)[None, :]
    pos = np.arange(nblk)[:, None, None] * blk - blk + j[None]
    valid = (pos >= 0) & (pos < L) & (np.abs(j - blk - i) <= radius)[None]
    logits = jnp.einsum('bhgnqd,bhgnkd->bhgnqk', qb, kb, preferred_element_type=jnp.float32)
    logits = jnp.where(valid, logits + bias[:, None, None].astype(jnp.float32), -jnp.inf)
    m = jnp.max(logits, axis=-1)
    p = jnp.exp(logits - m[..., None])
    s = jnp.sum(p, axis=-1)
    num = jnp.einsum('bhgnqk,bhgnkd->bhgnqd', p, vb.astype(jnp.float32))
    return (num.reshape(B, H, G, Lp, dh)[..., :L, :],
            m.reshape(B, H, G, Lp)[..., :L],
            s.reshape(B, H, G, Lp)[..., :L])


def dilated_mixture_attention(q, k, v, t5_table):
    B, H, T, dh = q.shape
    nums, ms, ss = [], [], []
    for window, dil in DIL_PATTERNS:
        radius = window // (2 * dil)
        L = T // dil

        def to_sub(a):
            return a.reshape(B, H, L, dil, dh).swapaxes(2, 3)

        rel = (np.arange(3 * radius)[None, :] - radius - np.arange(radius)[:, None]) * dil
        bias = jnp.transpose(t5_table[t5_bucket(rel)], (2, 0, 1))
        num, m, s = banded_attention_stats(to_sub(q), to_sub(k), to_sub(v), bias, radius)
        nums.append(num.swapaxes(2, 3).reshape(B, H, T, dh))
        ms.append(m.swapaxes(2, 3).reshape(B, H, T))
        ss.append(s.swapaxes(2, 3).reshape(B, H, T))
    m_all = jnp.stack(ms)
    w = jnp.exp(m_all - jnp.max(m_all, axis=0))
    out = jnp.sum(w[..., None] * jnp.stack(nums), axis=0) / jnp.sum(w * jnp.stack(ss), axis=0)[..., None]
    return out.astype(v.dtype)


def neighborhood_attention(q, k, v, rpb):
    B, H, T, dh = q.shape
    rows = T // GRID_W
    kh = min(NA_ROWS, rows)
    r = np.arange(rows)
    rstart = np.clip(r - kh // 2, 0, rows - kh)
    ridx = rstart[:, None] + np.arange(kh)[None, :]
    c = np.arange(GRID_W)
    cstart = np.clip(c - NA_COLS // 2, 0, GRID_W - NA_COLS)
    cmask = (c[None, :] >= cstart[:, None]) & (c[None, :] < cstart[:, None] + NA_COLS)
    mask = np.broadcast_to(cmask[:, None, :], (GRID_W, kh, GRID_W)).reshape(GRID_W, kh * GRID_W)
    roff = ridx - r[:, None] + NA_ROWS - 1
    coff = np.clip(c[None, :] - c[:, None], -(NA_COLS - 1), NA_COLS - 1) + NA_COLS - 1
    bias = rpb[:, roff[:, None, :, None], coff[None, :, None, :]]
    bias = bias.reshape(H, rows, GRID_W, kh * GRID_W)
    qg = q.reshape(B, H, rows, GRID_W, dh)
    kg = k.reshape(B, H, rows, GRID_W, dh)[:, :, ridx].reshape(B, H, rows, kh * GRID_W, dh)
    vg = v.reshape(B, H, rows, GRID_W, dh)[:, :, ridx].reshape(B, H, rows, kh * GRID_W, dh)
    logits = jnp.einsum('bhrqd,bhrkd->bhrqk', qg, kg, preferred_element_type=jnp.float32)
    logits = jnp.where(mask, logits + bias[None].astype(jnp.float32), -jnp.inf)
    p = jax.nn.softmax(logits, axis=-1)
    out = jnp.einsum('bhrqk,bhrkd->bhrqd', p, vg.astype(jnp.float32))
    return out.reshape(B, H, T, dh).astype(v.dtype)


def trunk(x, c, t5_table, norm1_g, norm2_g, w_ada, b_ada, w_in, out_norm_a, out_norm_b,
          w_out, na_rpb, w_gate, w_up, w_down, final_g):
    B, T, _ = x.shape

    def heads(a, n):
        return a.reshape(B, T, n, HEAD_DIM).transpose(0, 2, 1, 3)

    def merge(a):
        return a.transpose(0, 2, 1, 3).reshape(B, T, -1)

    for l in range(DEPTH):
        mod = (jax.nn.silu(c) @ w_ada[l] + b_ada[l])[:, None, :]
        sh1, sc1, g1, sh2, sc2, g2 = jnp.split(mod, 6, axis=-1)
        h = rms_norm(x, norm1_g[l]) * (1.0 + sc1) + sh1
        qkv = h @ w_in[l]
        qa, ka, va, qb, kb, vb = jnp.split(
            qkv, [D_A, 2 * D_A, 3 * D_A, 3 * D_A + D_B, 3 * D_A + 2 * D_B], axis=-1)
        ya = dilated_mixture_attention(heads(qa, N_HEADS_A) * QK_SCALE, heads(ka, N_HEADS_A),
                                       heads(va, N_HEADS_A), t5_table)
        yb = neighborhood_attention(heads(qb, N_HEADS_B) * QK_SCALE, heads(kb, N_HEADS_B),
                                    heads(vb, N_HEADS_B), na_rpb[l])
        y = jnp.concatenate([rms_norm(merge(ya), out_norm_a[l]),
                             rms_norm(merge(yb), out_norm_b[l])], axis=-1) @ w_out[l]
        x = x + g1 * y
        h = rms_norm(x, norm2_g[l]) * (1.0 + sc2) + sh2
        x = x + g2 * ((jax.nn.silu(h @ w_gate[l]) * (h @ w_up[l])) @ w_down[l])
    return rms_norm(x, final_g)


def setup_inputs(seed: int = 0) -> dict:
    key = jax.random.key(seed)
    ks = jax.random.split(key, 18)
    f32 = jnp.float32

    def nrm(k, shape, scale):
        return jax.random.normal(k, shape, f32) * scale

    return {
        'x_prompt': nrm(ks[0], (BATCH, SEQ, D_MODEL), 1.0),
        'x_sample': nrm(ks[1], (DEC_BATCH, DEC_SEQ, D_MODEL), 1.0),
        'c_prompt': nrm(ks[2], (BATCH, D_MODEL), 1.0),
        'c_sample': nrm(ks[3], (DEC_BATCH, D_MODEL), 1.0),
        't5_table': nrm(ks[4], (N_BUCKETS, N_HEADS_A), 0.1),
        'norm1_g': 1.0 + nrm(ks[5], (DEPTH, D_MODEL), 0.02),
        'norm2_g': 1.0 + nrm(ks[6], (DEPTH, D_MODEL), 0.02),
        'w_ada': nrm(ks[7], (DEPTH, D_MODEL, 6 * D_MODEL), 0.5 * D_MODEL ** -0.5),
        'b_ada': nrm(ks[8], (DEPTH, 6 * D_MODEL), 0.02),
        'w_in': nrm(ks[9], (DEPTH, D_MODEL, 3 * D_MIX), D_MODEL ** -0.5),
        'out_norm_a': 1.0 + nrm(ks[10], (DEPTH, D_A), 0.02),
        'out_norm_b': 1.0 + nrm(ks[11], (DEPTH, D_B), 0.02),
        'w_out': nrm(ks[12], (DEPTH, D_MIX, D_MODEL), D_MIX ** -0.5),
        'na_rpb': nrm(ks[13], (DEPTH, N_HEADS_B, 2 * NA_ROWS - 1, 2 * NA_COLS - 1), 0.1),
        'w_gate': nrm(ks[14], (DEPTH, D_MODEL, D_FF), D_MODEL ** -0.5),
        'w_up': nrm(ks[15], (DEPTH, D_MODEL, D_FF), D_MODEL ** -0.5),
        'w_down': nrm(ks[16], (DEPTH, D_FF, D_MODEL), D_FF ** -0.5),
        'final_g': 1.0 + nrm(ks[17], (D_MODEL,), 0.02),
    }


def reference(x_prompt, x_sample, c_prompt, c_sample, t5_table, norm1_g, norm2_g, w_ada, b_ada,
              w_in, out_norm_a, out_norm_b, w_out, na_rpb, w_gate, w_up, w_down, final_g):
    y_prompt = trunk(x_prompt, c_prompt, t5_table, norm1_g, norm2_g, w_ada, b_ada, w_in,
                     out_norm_a, out_norm_b, w_out, na_rpb, w_gate, w_up, w_down, final_g)
    y_sample = trunk(x_sample, c_sample, t5_table, norm1_g, norm2_g, w_ada, b_ada, w_in,
                     out_norm_a, out_norm_b, w_out, na_rpb, w_gate, w_up, w_down, final_g)
    return (y_prompt, y_sample)
```

```python
import functools
import math

import numpy as np
import jax
import jax.numpy as jnp
from jax import lax
from jax.experimental import pallas as pl
from jax.experimental.pallas import tpu as pltpu

D_MODEL = 2048
HEAD_DIM = 128
N_HEADS_A = 8
N_HEADS_B = 8
D_A = N_HEADS_A * HEAD_DIM
D_B = N_HEADS_B * HEAD_DIM
QK_SCALE = HEAD_DIM ** -0.5
DIL_PATTERNS = ((128, 1), (512, 4), (2048, 16))
RADIUS = 64
N_BUCKETS = 32
MAX_DISTANCE = 1024
GRID_W = 64
NA_ROWS = 8
NA_COLS = 16
EPS = 1e-6
NEG = -1e30

DIL_BLOCK = 1024
NA_BLOCK_ROWS = 8
NA_BLOCK = NA_BLOCK_ROWS * GRID_W

F32 = jnp.float32
BF16 = jnp.bfloat16
MIB = 1024 * 1024


def _cparams(sem, vmem_mib):
    return pltpu.CompilerParams(dimension_semantics=sem, vmem_limit_bytes=vmem_mib * MIB)


def _seq_index(i, starts):
    idx = 0
    for s in starts[1:]:
        idx = idx + (i >= s).astype(jnp.int32)
    return idx


def _ada_kernel(c_ref, w_ref, b_ref, o_ref):
    c = c_ref[...]
    s = c / (1.0 + jnp.exp(-c))
    o_ref[0] = jnp.dot(s, w_ref[0], precision=lax.Precision.HIGHEST,
                       preferred_element_type=F32) + b_ref[0]


def _ada_modulation(c_pad, w_ada, b_ada):
    depth, d, n = w_ada.shape
    tn = 1024
    return pl.pallas_call(
        _ada_kernel,
        grid=(depth, n // tn),
        in_specs=[pl.BlockSpec((8, d), lambda l, j: (0, 0)),
                  pl.BlockSpec((1, d, tn), lambda l, j: (l, 0, j)),
                  pl.BlockSpec((1, 1, tn), lambda l, j: (l, 0, j))],
        out_specs=pl.BlockSpec((1, 8, tn), lambda l, j: (l, 0, j)),
        out_shape=jax.ShapeDtypeStruct((depth, 8, n), F32),
        compiler_params=_cparams(("parallel", "parallel"), 40),
        name="ada_modulation",
    )(c_pad, w_ada, b_ada.reshape(depth, 1, n))


def _modulated_norm(x, g, shift, scale):
    y = x * lax.rsqrt(jnp.mean(x * x, axis=-1, keepdims=True) + EPS) * g
    return y * (1.0 + scale) + shift


def _qkv_kernel(x_ref, mod_ref, g_ref, cs_ref, w_ref, o_ref, h_ref):
    @pl.when(pl.program_id(1) == 0)
    def _():
        h = _modulated_norm(x_ref[...], g_ref[...], mod_ref[0, 0:1, :], mod_ref[0, 1:2, :])
        h_ref[...] = h.astype(BF16)

    r = jnp.dot(h_ref[...], w_ref[...], preferred_element_type=F32) * cs_ref[...]
    for c in range(o_ref.shape[0]):
        o_ref[c] = r[:, c * HEAD_DIM:(c + 1) * HEAD_DIM].astype(BF16)


def _qkv_proj(x, mod_l, g, col_scale, w, seq_lens):
    nt, d = x.shape
    n = w.shape[1]
    tm, tn = 512, 512
    starts = tuple(int(s) // tm for s in np.cumsum((0,) + seq_lens[:-1]))
    hb = tn // HEAD_DIM
    return pl.pallas_call(
        _qkv_kernel,
        grid=(nt // tm, n // tn),
        in_specs=[pl.BlockSpec((tm, d), lambda i, j: (i, 0)),
                  pl.BlockSpec((1, 6, d), lambda i, j: (_seq_index(i, starts), 0, 0)),
                  pl.BlockSpec((1, d), lambda i, j: (0, 0)),
                  pl.BlockSpec((1, tn), lambda i, j: (0, j)),
                  pl.BlockSpec((d, tn), lambda i, j: (0, j))],
        out_specs=pl.BlockSpec((hb, tm, HEAD_DIM), lambda i, j: (j, i, 0)),
        out_shape=jax.ShapeDtypeStruct((n // HEAD_DIM, nt, HEAD_DIM), BF16),
        scratch_shapes=[pltpu.VMEM((tm, d), BF16)],
        compiler_params=_cparams(("parallel", "arbitrary"), 40),
        name="qkv_proj",
    )(x, mod_l, g, col_scale, w)


def _t5_bucket(rel):
    nb = N_BUCKETS // 2
    exact = nb // 2
    n = np.abs(rel)
    sign = np.where(rel > 0, nb, 0)
    large = exact + (np.log(np.maximum(n, 1) / exact) / math.log(MAX_DISTANCE / exact) * (nb - exact)).astype(np.int64)
    large = np.minimum(large, nb - 1)
    return (sign + np.where(n < exact, n, large)).astype(np.int32)


def _dilated_bias(t5_table):
    i = np.arange(RADIUS)[:, None]
    j = np.arange(3 * RADIUS)[None, :]
    rel = j - RADIUS - i
    band = np.abs(rel) <= RADIUS
    tiles = []
    for _, dil in DIL_PATTERNS:
        b = t5_table[_t5_bucket(rel * dil)].astype(F32)
        tiles.append(jnp.where(band[:, :, None], b, NEG))
    return jnp.transpose(jnp.stack(tiles), (3, 0, 1, 2))


def _rows(start, size, stride):
    return pl.ds(start, size) if stride == 1 else pl.ds(start, size, stride=stride)


def _dil_kernel(q_ref, kp_ref, kc_ref, kn_ref, vp_ref, vc_ref, vn_ref, bias_ref, o_ref,
                qf, kw, vw, num_s, m_s, l_s, *, first_blocks, last_blocks):
    n = pl.program_id(1)
    first = functools.reduce(jnp.logical_or, [n == s for s in first_blocks])
    last = functools.reduce(jnp.logical_or, [n == s for s in last_blocks])
    lo = jnp.where(first, DIL_BLOCK, 0)
    hi = jnp.where(last, 2 * DIL_BLOCK, 3 * DIL_BLOCK)

    qf[...] = q_ref[0].astype(F32)
    for t, (kr, vr) in enumerate(((kp_ref, vp_ref), (kc_ref, vc_ref), (kn_ref, vn_ref))):
        kw[t * DIL_BLOCK:(t + 1) * DIL_BLOCK, :] = kr[0].astype(F32)
        vw[t * DIL_BLOCK:(t + 1) * DIL_BLOCK, :] = vr[0].astype(F32)

    col = lax.broadcasted_iota(jnp.int32, (1, 3 * RADIUS), 1)
    for p, (_, dil) in enumerate(DIL_PATTERNS):
        for r in range(dil):
            for b in range(DIL_BLOCK // (RADIUS * dil)):
                qs = r + dil * RADIUS * b
                ks = DIL_BLOCK + qs - dil * RADIUS
                q = qf[_rows(qs, RADIUS, dil), :].astype(BF16)
                k = kw[_rows(ks, 3 * RADIUS, dil), :].astype(BF16)
                v = vw[_rows(ks, 3 * RADIUS, dil), :].astype(BF16)
                s = lax.dot_general(q, k, (((1,), (1,)), ((), ())), preferred_element_type=F32)
                s = s + bias_ref[0, p]
                kt = ks + dil * col
                s = jnp.where((kt >= lo) & (kt < hi), s, NEG)
                m = jnp.max(s, axis=-1, keepdims=True)
                e = jnp.exp(s - m)
                l = jnp.sum(e, axis=-1, keepdims=True)
                num = jnp.dot(e.astype(BF16), v, preferred_element_type=F32)
                num_s[p, _rows(qs, RADIUS, dil), :] = num
                m_s[p, _rows(qs, RADIUS, dil), :] = jnp.broadcast_to(m, (RADIUS, HEAD_DIM))
                l_s[p, _rows(qs, RADIUS, dil), :] = jnp.broadcast_to(l, (RADIUS, HEAD_DIM))

    m_all = jnp.maximum(jnp.maximum(m_s[0], m_s[1]), m_s[2])
    num = jnp.zeros((DIL_BLOCK, HEAD_DIM), F32)
    den = jnp.zeros((DIL_BLOCK, HEAD_DIM), F32)
    for p in range(len(DIL_PATTERNS)):
        w = jnp.exp(m_s[p] - m_all)
        num = num + w * num_s[p]
        den = den + w * l_s[p]
    o_ref[...] = (num / den).astype(o_ref.dtype)


def _dilated_attention(qkv, bias, seq_lens):
    nt = qkv.shape[1]
    nblk = nt // DIL_BLOCK
    bounds = np.cumsum((0,) + seq_lens) // DIL_BLOCK
    first_blocks = tuple(int(b) for b in bounds[:-1])
    last_blocks = tuple(int(b) - 1 for b in bounds[1:])
    blk = (1, DIL_BLOCK, HEAD_DIM)
    prev = lambda n: jnp.maximum(n - 1, 0)
    nxt = lambda n: jnp.minimum(n + 1, nblk - 1)
    npat = len(DIL_PATTERNS)
    kernel = functools.partial(_dil_kernel, first_blocks=first_blocks, last_blocks=last_blocks)
    return pl.pallas_call(
        kernel,
        grid=(N_HEADS_A, nblk),
        in_specs=[pl.BlockSpec(blk, lambda h, n: (h, n, 0)),
                  pl.BlockSpec(blk, lambda h, n: (N_HEADS_A + h, prev(n), 0)),
                  pl.BlockSpec(blk, lambda h, n: (N_HEADS_A + h, n, 0)),
                  pl.BlockSpec(blk, lambda h, n: (N_HEADS_A + h, nxt(n), 0)),
                  pl.BlockSpec(blk, lambda h, n: (2 * N_HEADS_A + h, prev(n), 0)),
                  pl.BlockSpec(blk, lambda h, n: (2 * N_HEADS_A + h, n, 0)),
                  pl.BlockSpec(blk, lambda h, n: (2 * N_HEADS_A + h, nxt(n), 0)),
                  pl.BlockSpec((1, npat, RADIUS, 3 * RADIUS), lambda h, n: (h, 0, 0, 0))],
        out_specs=pl.BlockSpec((DIL_BLOCK, HEAD_DIM), lambda h, n: (n, h)),
        out_shape=jax.ShapeDtypeStruct((nt, D_A), BF16),
        scratch_shapes=[pltpu.VMEM((DIL_BLOCK, HEAD_DIM), F32),
                        pltpu.VMEM((3 * DIL_BLOCK, HEAD_DIM), F32),
                        pltpu.VMEM((3 * DIL_BLOCK, HEAD_DIM), F32),
                        pltpu.VMEM((npat, DIL_BLOCK, HEAD_DIM), F32),
                        pltpu.VMEM((npat, DIL_BLOCK, HEAD_DIM), F32),
                        pltpu.VMEM((npat, DIL_BLOCK, HEAD_DIM), F32)],
        compiler_params=_cparams(("parallel", "parallel"), 32),
        name="dilated_attention",
    )(qkv, qkv, qkv, qkv, qkv, qkv, qkv, bias)


def _na_bias(rpb):
    c = np.arange(GRID_W)
    cstart = np.clip(c - NA_COLS // 2, 0, GRID_W - NA_COLS)
    cmask = (c[None, :] >= cstart[:, None]) & (c[None, :] < cstart[:, None] + NA_COLS)
    coff = np.clip(c[None, :] - c[:, None], -(NA_COLS - 1), NA_COLS - 1) + NA_COLS - 1
    kr = np.arange(NA_ROWS)
    dd = np.arange(NA_ROWS)
    roff = kr[None, :] - dd[:, None] + NA_ROWS - 1
    b = rpb[:, roff[:, None, :, None], coff[None, :, None, :]].astype(F32)
    b = jnp.where(cmask[None, None, :, None, :], b, NEG)
    return b.reshape(rpb.shape[0], NA_ROWS, GRID_W, NA_ROWS * GRID_W)


def _na_kernel(q_ref, kp_ref, kc_ref, kn_ref, vp_ref, vc_ref, vn_ref, bias_ref, o_ref, kw, vw,
               *, block_starts, seq_rows):
    n = pl.program_id(1)
    nstart = jnp.int32(block_starts[0])
    rows = jnp.int32(seq_rows[0])
    for s, r in zip(block_starts[1:], seq_rows[1:]):
        nstart = jnp.where(n >= s, s, nstart)
        rows = jnp.where(n >= s, r, rows)

    for t, (kr, vr) in enumerate(((kp_ref, vp_ref), (kc_ref, vc_ref), (kn_ref, vn_ref))):
        kw[t * NA_BLOCK:(t + 1) * NA_BLOCK, :] = kr[0]
        vw[t * NA_BLOCK:(t + 1) * NA_BLOCK, :] = vr[0]

    for i in range(NA_BLOCK_ROWS):
        r = (n - nstart) * NA_BLOCK_ROWS + i
        dd = r - jnp.clip(r - NA_ROWS // 2, 0, rows - NA_ROWS)
        start = pl.multiple_of((NA_BLOCK_ROWS + i - dd) * GRID_W, GRID_W)
        k = kw[pl.ds(start, NA_ROWS * GRID_W), :]
        v = vw[pl.ds(start, NA_ROWS * GRID_W), :]
        q = q_ref[0, i * GRID_W:(i + 1) * GRID_W, :]
        s = lax.dot_general(q, k, (((1,), (1,)), ((), ())), preferred_element_type=F32)
        s = s + bias_ref[0, dd]
        m = jnp.max(s, axis=-1, keepdims=True)
        e = jnp.exp(s - m)
        l = jnp.sum(e, axis=-1, keepdims=True)
        o = jnp.dot(e.astype(BF16), v, preferred_element_type=F32) / l
        o_ref[i * GRID_W:(i + 1) * GRID_W, :] = o.astype(o_ref.dtype)


def _neighborhood_attention(qkv, bias, seq_lens):
    nt = qkv.shape[1]
    nblk = nt // NA_BLOCK
    block_starts = tuple(int(s) // NA_BLOCK for s in np.cumsum((0,) + seq_lens[:-1]))
    seq_rows = tuple(int(s) // GRID_W for s in seq_lens)
    assert all(r >= NA_ROWS and r % NA_BLOCK_ROWS == 0 for r in seq_rows)
    blk = (1, NA_BLOCK, HEAD_DIM)
    prev = lambda n: jnp.maximum(n - 1, 0)
    nxt = lambda n: jnp.minimum(n + 1, nblk - 1)
    qh, kh, vh = 3 * N_HEADS_A, 3 * N_HEADS_A + N_HEADS_B, 3 * N_HEADS_A + 2 * N_HEADS_B
    kernel = functools.partial(_na_kernel, block_starts=block_starts, seq_rows=seq_rows)
    return pl.pallas_call(
        kernel,
        grid=(N_HEADS_B, nblk),
        in_specs=[pl.BlockSpec(blk, lambda h, n: (qh + h, n, 0)),
                  pl.BlockSpec(blk, lambda h, n: (kh + h, prev(n), 0)),
                  pl.BlockSpec(blk, lambda h, n: (kh + h, n, 0)),
                  pl.BlockSpec(blk, lambda h, n: (kh + h, nxt(n), 0)),
                  pl.BlockSpec(blk, lambda h, n: (vh + h, prev(n), 0)),
                  pl.BlockSpec(blk, lambda h, n: (vh + h, n, 0)),
                  pl.BlockSpec(blk, lambda h, n: (vh + h, nxt(n), 0)),
                  pl.BlockSpec((1, NA_ROWS, GRID_W, NA_ROWS * GRID_W), lambda h, n: (h, 0, 0, 0))],
        out_specs=pl.BlockSpec((NA_BLOCK, HEAD_DIM), lambda h, n: (n, h)),
        out_shape=jax.ShapeDtypeStruct((nt, D_B), BF16),
        scratch_shapes=[pltpu.VMEM((3 * NA_BLOCK, HEAD_DIM), BF16),
                        pltpu.VMEM((3 * NA_BLOCK, HEAD_DIM), BF16)],
        compiler_params=_cparams(("parallel", "parallel"), 32),
        name="neighborhood_attention",
    )(qkv, qkv, qkv, qkv, qkv, qkv, qkv, bias)


def _plain_norm(y, g):
    y = y.astype(F32)
    return (y * lax.rsqrt(jnp.mean(y * y, axis=-1, keepdims=True) + EPS) * g).astype(BF16)


def _out_kernel(ya_ref, yb_ref, ga_ref, gb_ref, wa_ref, wb_ref, x_ref, mod_ref, g2_ref, xo_ref, h_ref):
    y = jnp.dot(_plain_norm(ya_ref[...], ga_ref[...]), wa_ref[...], preferred_element_type=F32)
    y = y + jnp.dot(_plain_norm(yb_ref[...], gb_ref[...]), wb_ref[...], preferred_element_type=F32)
    x = x_ref[...] + mod_ref[0, 2:3, :] * y
    xo_ref[...] = x
    h = _modulated_norm(x, g2_ref[...], mod_ref[0, 3:4, :], mod_ref[0, 4:5, :])
    h_ref[...] = h.astype(BF16)


def _out_proj(ya, yb, ga, gb, wa, wb, x, mod_l, g2, seq_lens):
    nt, d = x.shape
    tm = 256
    starts = tuple(int(s) // tm for s in np.cumsum((0,) + seq_lens[:-1]))
    const = lambda i: (0, 0)
    return pl.pallas_call(
        _out_kernel,
        grid=(nt // tm,),
        in_specs=[pl.BlockSpec((tm, D_A), lambda i: (i, 0)),
                  pl.BlockSpec((tm, D_B), lambda i: (i, 0)),
                  pl.BlockSpec((1, D_A), const),
                  pl.BlockSpec((1, D_B), const),
                  pl.BlockSpec((D_A, d), const),
                  pl.BlockSpec((D_B, d), const),
                  pl.BlockSpec((tm, d), lambda i: (i, 0)),
                  pl.BlockSpec((1, 6, d), lambda i: (_seq_index(i, starts), 0, 0)),
                  pl.BlockSpec((1, d), const)],
        out_specs=[pl.BlockSpec((tm, d), lambda i: (i, 0)),
                   pl.BlockSpec((tm, d), lambda i: (i, 0))],
        out_shape=[jax.ShapeDtypeStruct((nt, d), F32), jax.ShapeDtypeStruct((nt, d), BF16)],
        compiler_params=_cparams(("parallel",), 48),
        name="out_proj",
    )(ya, yb, ga, gb, wa, wb, x, mod_l, g2)


def _ffn_kernel(h_ref, wg_ref, wu_ref, wd_ref, x_ref, mod_ref, o_ref, acc_ref):
    f = pl.program_id(1)
    h = h_ref[...]
    a = jnp.dot(h, wg_ref[...], preferred_element_type=F32)
    b = jnp.dot(h, wu_ref[...], preferred_element_type=F32)
    act = (a / (1.0 + jnp.exp(-a)) * b).astype(BF16)
    part = jnp.dot(act, wd_ref[...], preferred_element_type=F32)

    @pl.when(f == 0)
    def _():
        acc_ref[...] = part

    @pl.when(f > 0)
    def _():
        acc_ref[...] += part

    @pl.when(f == pl.num_programs(1) - 1)
    def _():
        o_ref[...] = x_ref[...] + mod_ref[0, 5:6, :] * acc_ref[...]


def _ffn(h, wg, wu, wd, x, mod_l, seq_lens):
    nt, d = x.shape
    dff = wg.shape[1]
    tm, tf = 512, 512
    starts = tuple(int(s) // tm for s in np.cumsum((0,) + seq_lens[:-1]))
    return pl.pallas_call(
        _ffn_kernel,
        grid=(nt // tm, dff // tf),
        in_specs=[pl.BlockSpec((tm, d), lambda i, f: (i, 0)),
                  pl.BlockSpec((d, tf), lambda i, f: (0, f)),
                  pl.BlockSpec((d, tf), lambda i, f: (0, f)),
                  pl.BlockSpec((tf, d), lambda i, f: (f, 0)),
                  pl.BlockSpec((tm, d), lambda i, f: (i, 0)),
                  pl.BlockSpec((1, 6, d), lambda i, f: (_seq_index(i, starts), 0, 0))],
        out_specs=pl.BlockSpec((tm, d), lambda i, f: (i, 0)),
        out_shape=jax.ShapeDtypeStruct((nt, d), F32),
        scratch_shapes=[pltpu.VMEM((tm, d), F32)],
        compiler_params=_cparams(("parallel", "arbitrary"), 48),
        name="ffn",
    )(h, wg, wu, wd, x, mod_l)


def _final_kernel(x_ref, g_ref, o_ref):
    x = x_ref[...]
    o_ref[...] = x * lax.rsqrt(jnp.mean(x * x, axis=-1, keepdims=True) + EPS) * g_ref[...]


def _final_norm(x, g):
    nt, d = x.shape
    tm = 512
    return pl.pallas_call(
        _final_kernel,
        grid=(nt // tm,),
        in_specs=[pl.BlockSpec((tm, d), lambda i: (i, 0)), pl.BlockSpec((1, d), lambda i: (0, 0))],
        out_specs=pl.BlockSpec((tm, d), lambda i: (i, 0)),
        out_shape=jax.ShapeDtypeStruct((nt, d), F32),
        compiler_params=_cparams(("parallel",), 32),
        name="final_norm",
    )(x, g)


def _trunk(x, c, seq_lens, t5_table, norm1_g, norm2_g, w_ada, b_ada, w_in, out_norm_a, out_norm_b,
           w_out, na_rpb, w_gate, w_up, w_down, final_g):
    depth = w_in.shape[0]
    nseq = len(seq_lens)
    d = x.shape[1]
    assert all(s % DIL_BLOCK == 0 for s in seq_lens)

    c_pad = jnp.zeros((8, d), F32).at[:nseq].set(c)
    mod = _ada_modulation(c_pad, w_ada, b_ada)[:, :nseq].reshape(depth, nseq, 6, d)

    col_scale = jnp.ones((6, D_A), F32).at[0].set(QK_SCALE).at[3].set(QK_SCALE).reshape(1, 6 * D_A)
    dil_bias = _dilated_bias(t5_table)
    for l in range(depth):
        qkv = _qkv_proj(x, mod[l], norm1_g[l][None], col_scale, w_in[l].astype(BF16), seq_lens)
        ya = _dilated_attention(qkv, dil_bias, seq_lens)
        yb = _neighborhood_attention(qkv, _na_bias(na_rpb[l]), seq_lens)
        wo = w_out[l].astype(BF16)
        x, h = _out_proj(ya, yb, out_norm_a[l][None], out_norm_b[l][None], wo[:D_A], wo[D_A:],
                         x, mod[l], norm2_g[l][None], seq_lens)
        x = _ffn(h, w_gate[l].astype(BF16), w_up[l].astype(BF16), w_down[l].astype(BF16), x, mod[l],
                 seq_lens)
    return _final_norm(x, final_g[None])


def kernel(x_prompt, x_sample, c_prompt, c_sample, t5_table, norm1_g, norm2_g, w_ada, b_ada, w_in,
           out_norm_a, out_norm_b, w_out, na_rpb, w_gate, w_up, w_down, final_g):
    bp, tp, d = x_prompt.shape
    bs, ts, _ = x_sample.shape
    seq_lens = (tp,) * bp + (ts,) * bs
    x = jnp.concatenate([x_prompt.reshape(bp * tp, d), x_sample.reshape(bs * ts, d)], axis=0)
    c = jnp.concatenate([c_prompt, c_sample], axis=0)
    y = _trunk(x, c, seq_lens, t5_table, norm1_g, norm2_g, w_ada, b_ada, w_in, out_norm_a,
               out_norm_b, w_out, na_rpb, w_gate, w_up, w_down, final_g)
    return (y[:bp * tp].reshape(bp, tp, d), y[bp * tp:].reshape(bs, ts, d))
```

```python
import functools
import math

import numpy as np
import jax
import jax.numpy as jnp
from jax import lax
from jax.experimental import pallas as pl
from jax.experimental.pallas import tpu as pltpu

D_MODEL = 2048
HEAD_DIM = 128
N_HEADS_A = 8
N_HEADS_B = 8
D_A = N_HEADS_A * HEAD_DIM
D_B = N_HEADS_B * HEAD_DIM
QK_SCALE = HEAD_DIM ** -0.5
DIL_PATTERNS = ((128, 1), (512, 4), (2048, 16))
RADIUS = 64
N_BUCKETS = 32
MAX_DISTANCE = 1024
GRID_W = 64
NA_ROWS = 8
NA_COLS = 16
EPS = 1e-6
NEG = -1e30

DIL_BLOCK = 1024
NA_BLOCK_ROWS = 8
NA_BLOCK = NA_BLOCK_ROWS * GRID_W

F32 = jnp.float32
BF16 = jnp.bfloat16
MIB = 1024 * 1024


def _cparams(sem, vmem_mib):
    return pltpu.CompilerParams(dimension_semantics=sem, vmem_limit_bytes=vmem_mib * MIB)


def _seq_index(i, starts):
    idx = 0
    for s in starts[1:]:
        idx = idx + (i >= s).astype(jnp.int32)
    return idx


def _ada_kernel(c_ref, w_ref, b_ref, o_ref):
    c = c_ref[...]
    s = c / (1.0 + jnp.exp(-c))
    o_ref[0] = jnp.dot(s, w_ref[0], precision=lax.Precision.HIGHEST,
                       preferred_element_type=F32) + b_ref[0]


def _ada_modulation(c_pad, w_ada, b_ada):
    depth, d, n = w_ada.shape
    tn = 1024
    return pl.pallas_call(
        _ada_kernel,
        grid=(depth, n // tn),
        in_specs=[pl.BlockSpec((8, d), lambda l, j: (0, 0)),
                  pl.BlockSpec((1, d, tn), lambda l, j: (l, 0, j)),
                  pl.BlockSpec((1, 1, tn), lambda l, j: (l, 0, j))],
        out_specs=pl.BlockSpec((1, 8, tn), lambda l, j: (l, 0, j)),
        out_shape=jax.ShapeDtypeStruct((depth, 8, n), F32),
        compiler_params=_cparams(("parallel", "parallel"), 40),
        name="ada_modulation",
    )(c_pad, w_ada, b_ada.reshape(depth, 1, n))


def _modulated_norm(x, g, shift, scale):
    y = x * lax.rsqrt(jnp.mean(x * x, axis=-1, keepdims=True) + EPS) * g
    return y * (1.0 + scale) + shift


def _qkv_kernel(x_ref, mod_ref, g_ref, cs_ref, w_ref, o_ref, h_ref):
    @pl.when(pl.program_id(1) == 0)
    def _():
        h = _modulated_norm(x_ref[...], g_ref[...], mod_ref[0, 0:1, :], mod_ref[0, 1:2, :])
        h_ref[...] = h.astype(BF16)

    r = jnp.dot(h_ref[...], w_ref[...], preferred_element_type=F32) * cs_ref[...]
    for c in range(o_ref.shape[0]):
        o_ref[c] = r[:, c * HEAD_DIM:(c + 1) * HEAD_DIM].astype(BF16)


def _qkv_proj(x, mod_l, g, col_scale, w, seq_lens):
    nt, d = x.shape
    n = w.shape[1]
    tm, tn = 512, 512
    starts = tuple(int(s) // tm for s in np.cumsum((0,) + seq_lens[:-1]))
    hb = tn // HEAD_DIM
    return pl.pallas_call(
        _qkv_kernel,
        grid=(nt // tm, n // tn),
        in_specs=[pl.BlockSpec((tm, d), lambda i, j: (i, 0)),
                  pl.BlockSpec((1, 6, d), lambda i, j: (_seq_index(i, starts), 0, 0)),
                  pl.BlockSpec((1, d), lambda i, j: (0, 0)),
                  pl.BlockSpec((1, tn), lambda i, j: (0, j)),
                  pl.BlockSpec((d, tn), lambda i, j: (0, j))],
        out_specs=pl.BlockSpec((hb, tm, HEAD_DIM), lambda i, j: (j, i, 0)),
        out_shape=jax.ShapeDtypeStruct((n // HEAD_DIM, nt, HEAD_DIM), BF16),
        scratch_shapes=[pltpu.VMEM((tm, d), BF16)],
        compiler_params=_cparams(("parallel", "arbitrary"), 40),
        name="qkv_proj",
    )(x, mod_l, g, col_scale, w)


def _t5_bucket(rel):
    nb = N_BUCKETS // 2
    exact = nb // 2
    n = np.abs(rel)
    sign = np.where(rel > 0, nb, 0)
    large = exact + (np.log(np.maximum(n, 1) / exact) / math.log(MAX_DISTANCE / exact) * (nb - exact)).astype(np.int64)
    large = np.minimum(large, nb - 1)
    return (sign + np.where(n < exact, n, large)).astype(np.int32)


def _dilated_bias(t5_table):
    i = np.arange(RADIUS)[:, None]
    j = np.arange(3 * RADIUS)[None, :]
    rel = j - RADIUS - i
    band = np.abs(rel) <= RADIUS
    tiles = []
    table = t5_table.astype(F32).T
    for _, dil in DIL_PATTERNS:
        bucket = _t5_bucket(rel * dil)
        b = jnp.full((table.shape[0],) + rel.shape, NEG, F32)
        for o in range(N_BUCKETS):
            b = jnp.where((band & (bucket == o))[None], table[:, o, None, None], b)
        tiles.append(b)
    return jnp.stack(tiles, axis=1)


def _rows(start, size, stride):
    return pl.ds(start, size) if stride == 1 else pl.ds(start, size, stride=stride)


def _dil_kernel(q_ref, kp_ref, kc_ref, kn_ref, vp_ref, vc_ref, vn_ref, bias_ref, o_ref,
                qf, kw, vw, num_s, m_s, l_s, s_s, e_s, *, first_blocks, last_blocks):
    n = pl.program_id(1)
    first = functools.reduce(jnp.logical_or, [n == s for s in first_blocks])
    last = functools.reduce(jnp.logical_or, [n == s for s in last_blocks])
    lo = jnp.where(first, DIL_BLOCK, 0)
    hi = jnp.where(last, 2 * DIL_BLOCK, 3 * DIL_BLOCK)

    qf[...] = q_ref[0].astype(F32)
    for t, (kr, vr) in enumerate(((kp_ref, vp_ref), (kc_ref, vc_ref), (kn_ref, vn_ref))):
        kw[t * DIL_BLOCK:(t + 1) * DIL_BLOCK, :] = kr[0].astype(F32)
        vw[t * DIL_BLOCK:(t + 1) * DIL_BLOCK, :] = vr[0].astype(F32)

    col = lax.broadcasted_iota(jnp.int32, (1, 3 * RADIUS), 1)

    def tiles(dil):
        out = []
        for r in range(dil):
            for b in range(DIL_BLOCK // (RADIUS * dil)):
                qs = r + dil * RADIUS * b
                out.append((len(out), qs, DIL_BLOCK + qs - dil * RADIUS))
        return out

    def logits_pass(p, dil):
        for t, qs, ks in tiles(dil):
            q = qf[_rows(qs, RADIUS, dil), :].astype(BF16)
            k = kw[_rows(ks, 3 * RADIUS, dil), :].astype(BF16)
            s = lax.dot_general(q, k, (((1,), (1,)), ((), ())), preferred_element_type=F32)
            s = s + bias_ref[0, p]
            kt = ks + dil * col
            s_s[p, t] = jnp.where((kt >= lo) & (kt < hi), s, NEG)

    def softmax_pass(p, dil):
        for t, qs, ks in tiles(dil):
            s = s_s[p, t]
            m = jnp.max(s, axis=-1, keepdims=True)
            e = jnp.exp(s - m)
            l = jnp.sum(e, axis=-1, keepdims=True)
            e_s[p, t] = e.astype(BF16)
            m_s[p, _rows(qs, RADIUS, dil), :] = jnp.broadcast_to(m, (RADIUS, HEAD_DIM))
            l_s[p, _rows(qs, RADIUS, dil), :] = jnp.broadcast_to(l, (RADIUS, HEAD_DIM))

    def values_pass(p, dil):
        for t, qs, ks in tiles(dil):
            v = vw[_rows(ks, 3 * RADIUS, dil), :].astype(BF16)
            num_s[p, _rows(qs, RADIUS, dil), :] = jnp.dot(e_s[p, t], v, preferred_element_type=F32)

    dils = [dil for _, dil in DIL_PATTERNS]
    logits_pass(0, dils[0])
    logits_pass(1, dils[1])
    softmax_pass(0, dils[0])
    values_pass(0, dils[0])
    logits_pass(2, dils[2])
    softmax_pass(1, dils[1])
    values_pass(1, dils[1])
    softmax_pass(2, dils[2])
    values_pass(2, dils[2])

    m_all = jnp.maximum(jnp.maximum(m_s[0], m_s[1]), m_s[2])
    num = jnp.zeros((DIL_BLOCK, HEAD_DIM), F32)
    den = jnp.zeros((DIL_BLOCK, HEAD_DIM), F32)
    for p in range(len(DIL_PATTERNS)):
        w = jnp.exp(m_s[p] - m_all)
        num = num + w * num_s[p]
        den = den + w * l_s[p]
    o_ref[...] = (num / den).astype(o_ref.dtype)


def _dilated_attention(qkv, bias, seq_lens):
    nt = qkv.shape[1]
    nblk = nt // DIL_BLOCK
    bounds = np.cumsum((0,) + seq_lens) // DIL_BLOCK
    first_blocks = tuple(int(b) for b in bounds[:-1])
    last_blocks = tuple(int(b) - 1 for b in bounds[1:])
    blk = (1, DIL_BLOCK, HEAD_DIM)
    prev = lambda n: jnp.maximum(n - 1, 0)
    nxt = lambda n: jnp.minimum(n + 1, nblk - 1)
    npat = len(DIL_PATTERNS)
    kernel = functools.partial(_dil_kernel, first_blocks=first_blocks, last_blocks=last_blocks)
    return pl.pallas_call(
        kernel,
        grid=(N_HEADS_A, nblk),
        in_specs=[pl.BlockSpec(blk, lambda h, n: (h, n, 0)),
                  pl.BlockSpec(blk, lambda h, n: (N_HEADS_A + h, prev(n), 0)),
                  pl.BlockSpec(blk, lambda h, n: (N_HEADS_A + h, n, 0)),
                  pl.BlockSpec(blk, lambda h, n: (N_HEADS_A + h, nxt(n), 0)),
                  pl.BlockSpec(blk, lambda h, n: (2 * N_HEADS_A + h, prev(n), 0)),
                  pl.BlockSpec(blk, lambda h, n: (2 * N_HEADS_A + h, n, 0)),
                  pl.BlockSpec(blk, lambda h, n: (2 * N_HEADS_A + h, nxt(n), 0)),
                  pl.BlockSpec((1, npat, RADIUS, 3 * RADIUS), lambda h, n: (h, 0, 0, 0))],
        out_specs=pl.BlockSpec((DIL_BLOCK, HEAD_DIM), lambda h, n: (n, h)),
        out_shape=jax.ShapeDtypeStruct((nt, D_A), BF16),
        scratch_shapes=[pltpu.VMEM((DIL_BLOCK, HEAD_DIM), F32),
                        pltpu.VMEM((3 * DIL_BLOCK, HEAD_DIM), F32),
                        pltpu.VMEM((3 * DIL_BLOCK, HEAD_DIM), F32),
                        pltpu.VMEM((npat, DIL_BLOCK, HEAD_DIM), F32),
                        pltpu.VMEM((npat, DIL_BLOCK, HEAD_DIM), F32),
                        pltpu.VMEM((npat, DIL_BLOCK, HEAD_DIM), F32),
                        pltpu.VMEM((npat, DIL_BLOCK // RADIUS, RADIUS, 3 * RADIUS), F32),
                        pltpu.VMEM((npat, DIL_BLOCK // RADIUS, RADIUS, 3 * RADIUS), BF16)],
        compiler_params=_cparams(("parallel", "parallel"), 32),
        name="dilated_attention",
    )(qkv, qkv, qkv, qkv, qkv, qkv, qkv, bias)


def _na_bias(rpb):
    c = np.arange(GRID_W)
    cstart = np.clip(c - NA_COLS // 2, 0, GRID_W - NA_COLS)
    cmask = (c[None, :] >= cstart[:, None]) & (c[None, :] < cstart[:, None] + NA_COLS)
    coff = np.clip(c[None, :] - c[:, None], -(NA_COLS - 1), NA_COLS - 1) + NA_COLS - 1
    cols = jnp.full(rpb.shape[:2] + (GRID_W, GRID_W), NEG, F32)
    for o in range(2 * NA_COLS - 1):
        cols = jnp.where((cmask & (coff == o))[None, None], rpb[:, :, o, None, None].astype(F32), cols)
    tiles = [cols[:, NA_ROWS - 1 - dd:2 * NA_ROWS - 1 - dd] for dd in range(NA_ROWS)]
    b = jnp.stack(tiles, axis=1).transpose(0, 1, 3, 2, 4)
    return b.reshape(rpb.shape[0], NA_ROWS, GRID_W, NA_ROWS * GRID_W)


def _na_kernel(q_ref, kp_ref, kc_ref, kn_ref, vp_ref, vc_ref, vn_ref, bias_ref, o_ref, kw, vw,
               s_s, e_s, l_s, *, block_starts, seq_rows):
    n = pl.program_id(1)
    nstart = jnp.int32(block_starts[0])
    rows = jnp.int32(seq_rows[0])
    for s, r in zip(block_starts[1:], seq_rows[1:]):
        nstart = jnp.where(n >= s, s, nstart)
        rows = jnp.where(n >= s, r, rows)

    for t, (kr, vr) in enumerate(((kp_ref, vp_ref), (kc_ref, vc_ref), (kn_ref, vn_ref))):
        kw[t * NA_BLOCK:(t + 1) * NA_BLOCK, :] = kr[0]
        vw[t * NA_BLOCK:(t + 1) * NA_BLOCK, :] = vr[0]

    shifts, starts = [], []
    for i in range(NA_BLOCK_ROWS):
        r = (n - nstart) * NA_BLOCK_ROWS + i
        dd = r - jnp.clip(r - NA_ROWS // 2, 0, rows - NA_ROWS)
        shifts.append(dd)
        starts.append(pl.multiple_of((NA_BLOCK_ROWS + i - dd) * GRID_W, GRID_W))

    for i in range(NA_BLOCK_ROWS):
        k = kw[pl.ds(starts[i], NA_ROWS * GRID_W), :]
        q = q_ref[0, i * GRID_W:(i + 1) * GRID_W, :]
        s = lax.dot_general(q, k, (((1,), (1,)), ((), ())), preferred_element_type=F32)
        s_s[i] = s + bias_ref[0, shifts[i]]
    for i in range(NA_BLOCK_ROWS):
        s = s_s[i]
        m = jnp.max(s, axis=-1, keepdims=True)
        e = jnp.exp(s - m)
        l_s[i] = jnp.broadcast_to(jnp.sum(e, axis=-1, keepdims=True), (GRID_W, HEAD_DIM))
        e_s[i] = e.astype(BF16)
    for i in range(NA_BLOCK_ROWS):
        v = vw[pl.ds(starts[i], NA_ROWS * GRID_W), :]
        o = jnp.dot(e_s[i], v, preferred_element_type=F32) / l_s[i]
        o_ref[i * GRID_W:(i + 1) * GRID_W, :] = o.astype(o_ref.dtype)


def _neighborhood_attention(qkv, bias, seq_lens):
    nt = qkv.shape[1]
    nblk = nt // NA_BLOCK
    block_starts = tuple(int(s) // NA_BLOCK for s in np.cumsum((0,) + seq_lens[:-1]))
    seq_rows = tuple(int(s) // GRID_W for s in seq_lens)
    assert all(r >= NA_ROWS and r % NA_BLOCK_ROWS == 0 for r in seq_rows)
    blk = (1, NA_BLOCK, HEAD_DIM)
    prev = lambda n: jnp.maximum(n - 1, 0)
    nxt = lambda n: jnp.minimum(n + 1, nblk - 1)
    qh, kh, vh = 3 * N_HEADS_A, 3 * N_HEADS_A + N_HEADS_B, 3 * N_HEADS_A + 2 * N_HEADS_B
    kernel = functools.partial(_na_kernel, block_starts=block_starts, seq_rows=seq_rows)
    return pl.pallas_call(
        kernel,
        grid=(N_HEADS_B, nblk),
        in_specs=[pl.BlockSpec(blk, lambda h, n: (qh + h, n, 0)),
                  pl.BlockSpec(blk, lambda h, n: (kh + h, prev(n), 0)),
                  pl.BlockSpec(blk, lambda h, n: (kh + h, n, 0)),
                  pl.BlockSpec(blk, lambda h, n: (kh + h, nxt(n), 0)),
                  pl.BlockSpec(blk, lambda h, n: (vh + h, prev(n), 0)),
                  pl.BlockSpec(blk, lambda h, n: (vh + h, n, 0)),
                  pl.BlockSpec(blk, lambda h, n: (vh + h, nxt(n), 0)),
                  pl.BlockSpec((1, NA_ROWS, GRID_W, NA_ROWS * GRID_W), lambda h, n: (h, 0, 0, 0))],
        out_specs=pl.BlockSpec((NA_BLOCK, HEAD_DIM), lambda h, n: (n, h)),
        out_shape=jax.ShapeDtypeStruct((nt, D_B), BF16),
        scratch_shapes=[pltpu.VMEM((3 * NA_BLOCK, HEAD_DIM), BF16),
                        pltpu.VMEM((3 * NA_BLOCK, HEAD_DIM), BF16),
                        pltpu.VMEM((NA_BLOCK_ROWS, GRID_W, NA_ROWS * GRID_W), F32),
                        pltpu.VMEM((NA_BLOCK_ROWS, GRID_W, NA_ROWS * GRID_W), BF16),
                        pltpu.VMEM((NA_BLOCK_ROWS, GRID_W, HEAD_DIM), F32)],
        compiler_params=_cparams(("parallel", "parallel"), 32),
        name="neighborhood_attention",
    )(qkv, qkv, qkv, qkv, qkv, qkv, qkv, bias)


def _plain_norm(y, g):
    y = y.astype(F32)
    return (y * lax.rsqrt(jnp.mean(y * y, axis=-1, keepdims=True) + EPS) * g).astype(BF16)


def _out_kernel(ya_ref, yb_ref, ga_ref, gb_ref, wa_ref, wb_ref, x_ref, mod_ref, g2_ref, xo_ref, h_ref):
    y = jnp.dot(_plain_norm(ya_ref[...], ga_ref[...]), wa_ref[...], preferred_element_type=F32)
    y = y + jnp.dot(_plain_norm(yb_ref[...], gb_ref[...]), wb_ref[...], preferred_element_type=F32)
    x = x_ref[...] + mod_ref[0, 2:3, :] * y
    xo_ref[...] = x
    h = _modulated_norm(x, g2_ref[...], mod_ref[0, 3:4, :], mod_ref[0, 4:5, :])
    h_ref[...] = h.astype(BF16)


def _out_proj(ya, yb, ga, gb, wa, wb, x, mod_l, g2, seq_lens):
    nt, d = x.shape
    tm = 256
    starts = tuple(int(s) // tm for s in np.cumsum((0,) + seq_lens[:-1]))
    const = lambda i: (0, 0)
    return pl.pallas_call(
        _out_kernel,
        grid=(nt // tm,),
        in_specs=[pl.BlockSpec((tm, D_A), lambda i: (i, 0)),
                  pl.BlockSpec((tm, D_B), lambda i: (i, 0)),
                  pl.BlockSpec((1, D_A), const),
                  pl.BlockSpec((1, D_B), const),
                  pl.BlockSpec((D_A, d), const),
                  pl.BlockSpec((D_B, d), const),
                  pl.BlockSpec((tm, d), lambda i: (i, 0)),
                  pl.BlockSpec((1, 6, d), lambda i: (_seq_index(i, starts), 0, 0)),
                  pl.BlockSpec((1, d), const)],
        out_specs=[pl.BlockSpec((tm, d), lambda i: (i, 0)),
                   pl.BlockSpec((tm, d), lambda i: (i, 0))],
        out_shape=[jax.ShapeDtypeStruct((nt, d), F32), jax.ShapeDtypeStruct((nt, d), BF16)],
        compiler_params=_cparams(("parallel",), 48),
        name="out_proj",
    )(ya, yb, ga, gb, wa, wb, x, mod_l, g2)


def _ffn_kernel(h_ref, wg_ref, wu_ref, wd_ref, x_ref, mod_ref, o_ref, acc_ref):
    f = pl.program_id(1)
    h = h_ref[...]
    a = jnp.dot(h, wg_ref[...], preferred_element_type=F32)
    b = jnp.dot(h, wu_ref[...], preferred_element_type=F32)
    act = (a / (1.0 + jnp.exp(-a)) * b).astype(BF16)
    part = jnp.dot(act, wd_ref[...], preferred_element_type=F32)

    @pl.when(f == 0)
    def _():
        acc_ref[...] = part

    @pl.when(f > 0)
    def _():
        acc_ref[...] += part

    @pl.when(f == pl.num_programs(1) - 1)
    def _():
        o_ref[...] = x_ref[...] + mod_ref[0, 5:6, :] * acc_ref[...]


def _ffn(h, wg, wu, wd, x, mod_l, seq_lens):
    nt, d = x.shape
    dff = wg.shape[1]
    tm, tf = 512, 512
    starts = tuple(int(s) // tm for s in np.cumsum((0,) + seq_lens[:-1]))
    return pl.pallas_call(
        _ffn_kernel,
        grid=(nt // tm, dff // tf),
        in_specs=[pl.BlockSpec((tm, d), lambda i, f: (i, 0)),
                  pl.BlockSpec((d, tf), lambda i, f: (0, f)),
                  pl.BlockSpec((d, tf), lambda i, f: (0, f)),
                  pl.BlockSpec((tf, d), lambda i, f: (f, 0)),
                  pl.BlockSpec((tm, d), lambda i, f: (i, 0)),
                  pl.BlockSpec((1, 6, d), lambda i, f: (_seq_index(i, starts), 0, 0))],
        out_specs=pl.BlockSpec((tm, d), lambda i, f: (i, 0)),
        out_shape=jax.ShapeDtypeStruct((nt, d), F32),
        scratch_shapes=[pltpu.VMEM((tm, d), F32)],
        compiler_params=_cparams(("parallel", "arbitrary"), 48),
        name="ffn",
    )(h, wg, wu, wd, x, mod_l)


def _final_kernel(x_ref, g_ref, o_ref):
    x = x_ref[...]
    o_ref[...] = x * lax.rsqrt(jnp.mean(x * x, axis=-1, keepdims=True) + EPS) * g_ref[...]


def _final_norm(x, g):
    nt, d = x.shape
    tm = 512
    return pl.pallas_call(
        _final_kernel,
        grid=(nt // tm,),
        in_specs=[pl.BlockSpec((tm, d), lambda i: (i, 0)), pl.BlockSpec((1, d), lambda i: (0, 0))],
        out_specs=pl.BlockSpec((tm, d), lambda i: (i, 0)),
        out_shape=jax.ShapeDtypeStruct((nt, d), F32),
        compiler_params=_cparams(("parallel",), 32),
        name="final_norm",
    )(x, g)


def _trunk(x, c, seq_lens, t5_table, norm1_g, norm2_g, w_ada, b_ada, w_in, out_norm_a, out_norm_b,
           w_out, na_rpb, w_gate, w_up, w_down, final_g):
    depth = w_in.shape[0]
    nseq = len(seq_lens)
    d = x.shape[1]
    assert all(s % DIL_BLOCK == 0 for s in seq_lens)

    c_pad = jnp.zeros((8, d), F32).at[:nseq].set(c)
    mod = _ada_modulation(c_pad, w_ada, b_ada)[:, :nseq].reshape(depth, nseq, 6, d)

    col_scale = jnp.ones((6, D_A), F32).at[0].set(QK_SCALE).at[3].set(QK_SCALE).reshape(1, 6 * D_A)
    dil_bias = _dilated_bias(t5_table)
    for l in range(depth):
        qkv = _qkv_proj(x, mod[l], norm1_g[l][None], col_scale, w_in[l].astype(BF16), seq_lens)
        ya = _dilated_attention(qkv, dil_bias, seq_lens)
        yb = _neighborhood_attention(qkv, _na_bias(na_rpb[l]), seq_lens)
        wo = w_out[l].astype(BF16)
        x, h = _out_proj(ya, yb, out_norm_a[l][None], out_norm_b[l][None], wo[:D_A], wo[D_A:],
                         x, mod[l], norm2_g[l][None], seq_lens)
        x = _ffn(h, w_gate[l].astype(BF16), w_up[l].astype(BF16), w_down[l].astype(BF16), x, mod[l],
                 seq_lens)
    return _final_norm(x, final_g[None])


def kernel(x_prompt, x_sample, c_prompt, c_sample, t5_table, norm1_g, norm2_g, w_ada, b_ada, w_in,
           out_norm_a, out_norm_b, w_out, na_rpb, w_gate, w_up, w_down, final_g):
    bp, tp, d = x_prompt.shape
    bs, ts, _ = x_sample.shape
    seq_lens = (tp,) * bp + (ts,) * bs
    x = jnp.concatenate([x_prompt.reshape(bp * tp, d), x_sample.reshape(bs * ts, d)], axis=0)
    c = jnp.concatenate([c_prompt, c_sample], axis=0)
    y = _trunk(x, c, seq_lens, t5_table, norm1_g, norm2_g, w_ada, b_ada, w_in, out_norm_a,
               out_norm_b, w_out, na_rpb, w_gate, w_up, w_down, final_g)
    return (y[:bp * tp].reshape(bp, tp, d), y[bp * tp:].reshape(bs, ts, d))
```

```python
import functools
import math

import numpy as np
import jax
import jax.numpy as jnp
from jax import lax
from jax.experimental import pallas as pl
from jax.experimental.pallas import tpu as pltpu

D_MODEL = 2048
HEAD_DIM = 128
N_HEADS_A = 8
N_HEADS_B = 8
D_A = N_HEADS_A * HEAD_DIM
D_B = N_HEADS_B * HEAD_DIM
QK_SCALE = HEAD_DIM ** -0.5
DIL_PATTERNS = ((128, 1), (512, 4), (2048, 16))
RADIUS = 64
N_BUCKETS = 32
MAX_DISTANCE = 1024
GRID_W = 64
NA_ROWS = 8
NA_COLS = 16
EPS = 1e-6
NEG = -1e30

DIL_BLOCK = 1024
NA_BLOCK_ROWS = 8
NA_BLOCK = NA_BLOCK_ROWS * GRID_W

F32 = jnp.float32
BF16 = jnp.bfloat16
MIB = 1024 * 1024


def _cparams(sem, vmem_mib):
    return pltpu.CompilerParams(dimension_semantics=sem, vmem_limit_bytes=vmem_mib * MIB)


def _seq_index(i, starts):
    idx = 0
    for s in starts[1:]:
        idx = idx + (i >= s).astype(jnp.int32)
    return idx


def _ada_kernel(c_ref, w_ref, b_ref, o_ref):
    c = c_ref[...]
    s = c / (1.0 + jnp.exp(-c))
    o_ref[0] = jnp.dot(s, w_ref[0], precision=lax.Precision.HIGHEST,
                       preferred_element_type=F32) + b_ref[0]


def _ada_modulation(c_pad, w_ada, b_ada):
    depth, d, n = w_ada.shape
    tn = 1024
    return pl.pallas_call(
        _ada_kernel,
        grid=(depth, n // tn),
        in_specs=[pl.BlockSpec((8, d), lambda l, j: (0, 0)),
                  pl.BlockSpec((1, d, tn), lambda l, j: (l, 0, j)),
                  pl.BlockSpec((1, 1, tn), lambda l, j: (l, 0, j))],
        out_specs=pl.BlockSpec((1, 8, tn), lambda l, j: (l, 0, j)),
        out_shape=jax.ShapeDtypeStruct((depth, 8, n), F32),
        compiler_params=_cparams(("parallel", "parallel"), 40),
        name="ada_modulation",
    )(c_pad, w_ada, b_ada.reshape(depth, 1, n))


def _modulated_norm(x, g, shift, scale):
    y = x * lax.rsqrt(jnp.mean(x * x, axis=-1, keepdims=True) + EPS) * g
    return y * (1.0 + scale) + shift


def _prenorm_kernel(x_ref, mod_ref, g_ref, h_ref):
    h = _modulated_norm(x_ref[...], g_ref[...], mod_ref[0, 0:1, :], mod_ref[0, 1:2, :])
    h_ref[...] = h.astype(BF16)


def _prenorm(x, mod_l, g, seq_lens):
    nt, d = x.shape
    tm = 512
    starts = tuple(int(s) // tm for s in np.cumsum((0,) + seq_lens[:-1]))
    return pl.pallas_call(
        _prenorm_kernel,
        grid=(nt // tm,),
        in_specs=[pl.BlockSpec((tm, d), lambda i: (i, 0)),
                  pl.BlockSpec((1, 6, d), lambda i: (_seq_index(i, starts), 0, 0)),
                  pl.BlockSpec((1, d), lambda i: (0, 0))],
        out_specs=pl.BlockSpec((tm, d), lambda i: (i, 0)),
        out_shape=jax.ShapeDtypeStruct((nt, d), BF16),
        compiler_params=_cparams(("parallel",), 32),
        name="prenorm",
    )(x, mod_l, g)


def _qkv_kernel(h_ref, cs_ref, w_ref, o_ref):
    r = jnp.dot(h_ref[...], w_ref[...], preferred_element_type=F32) * cs_ref[...]
    for c in range(o_ref.shape[0]):
        o_ref[c] = r[:, c * HEAD_DIM:(c + 1) * HEAD_DIM].astype(BF16)


def _qkv_proj(h, col_scale, w):
    nt, d = h.shape
    n = w.shape[1]
    tm, tn = 1024, 1024
    hb = tn // HEAD_DIM
    return pl.pallas_call(
        _qkv_kernel,
        grid=(nt // tm, n // tn),
        in_specs=[pl.BlockSpec((tm, d), lambda i, j: (i, 0)),
                  pl.BlockSpec((1, tn), lambda i, j: (0, j)),
                  pl.BlockSpec((d, tn), lambda i, j: (0, j))],
        out_specs=pl.BlockSpec((hb, tm, HEAD_DIM), lambda i, j: (j, i, 0)),
        out_shape=jax.ShapeDtypeStruct((n // HEAD_DIM, nt, HEAD_DIM), BF16),
        compiler_params=_cparams(("parallel", "parallel"), 40),
        name="qkv_proj",
    )(h, col_scale, w)


def _t5_bucket(rel):
    nb = N_BUCKETS // 2
    exact = nb // 2
    n = np.abs(rel)
    sign = np.where(rel > 0, nb, 0)
    large = exact + (np.log(np.maximum(n, 1) / exact) / math.log(MAX_DISTANCE / exact) * (nb - exact)).astype(np.int64)
    large = np.minimum(large, nb - 1)
    return (sign + np.where(n < exact, n, large)).astype(np.int32)


def _dilated_bias(t5_table):
    i = np.arange(RADIUS)[:, None]
    j = np.arange(3 * RADIUS)[None, :]
    rel = j - RADIUS - i
    band = np.abs(rel) <= RADIUS
    tiles = []
    table = t5_table.astype(F32).T
    for _, dil in DIL_PATTERNS:
        bucket = _t5_bucket(rel * dil)
        b = jnp.full((table.shape[0],) + rel.shape, NEG, F32)
        for o in range(N_BUCKETS):
            b = jnp.where((band & (bucket == o))[None], table[:, o, None, None], b)
        tiles.append(b)
    return jnp.stack(tiles, axis=1)


def _rows(start, size, stride):
    return pl.ds(start, size) if stride == 1 else pl.ds(start, size, stride=stride)


def _dil_kernel(q_ref, kp_ref, kc_ref, kn_ref, vp_ref, vc_ref, vn_ref, bias_ref, o_ref,
                qf, kw, vw, num_s, m_s, l_s, s_s, e_s, *, first_blocks, last_blocks):
    n = pl.program_id(1)
    first = functools.reduce(jnp.logical_or, [n == s for s in first_blocks])
    last = functools.reduce(jnp.logical_or, [n == s for s in last_blocks])
    lo = jnp.where(first, DIL_BLOCK, 0)
    hi = jnp.where(last, 2 * DIL_BLOCK, 3 * DIL_BLOCK)

    qf[...] = q_ref[0].astype(F32)
    for t, (kr, vr) in enumerate(((kp_ref, vp_ref), (kc_ref, vc_ref), (kn_ref, vn_ref))):
        kw[t * DIL_BLOCK:(t + 1) * DIL_BLOCK, :] = kr[0].astype(F32)
        vw[t * DIL_BLOCK:(t + 1) * DIL_BLOCK, :] = vr[0].astype(F32)

    col = lax.broadcasted_iota(jnp.int32, (1, 3 * RADIUS), 1)

    def tiles(dil):
        out = []
        for r in range(dil):
            for b in range(DIL_BLOCK // (RADIUS * dil)):
                qs = r + dil * RADIUS * b
                out.append((len(out), qs, DIL_BLOCK + qs - dil * RADIUS))
        return out

    def logits_pass(p, dil):
        for t, qs, ks in tiles(dil):
            q = qf[_rows(qs, RADIUS, dil), :].astype(BF16)
            k = kw[_rows(ks, 3 * RADIUS, dil), :].astype(BF16)
            s = lax.dot_general(q, k, (((1,), (1,)), ((), ())), preferred_element_type=F32)
            s = s + bias_ref[0, p]
            kt = ks + dil * col
            s_s[p, t] = jnp.where((kt >= lo) & (kt < hi), s, NEG)

    def softmax_pass(p, dil):
        for t, qs, ks in tiles(dil):
            s = s_s[p, t]
            m = jnp.max(s, axis=-1, keepdims=True)
            e = jnp.exp(s - m)
            l = jnp.sum(e, axis=-1, keepdims=True)
            e_s[p, t] = e.astype(BF16)
            m_s[p, _rows(qs, RADIUS, dil), :] = jnp.broadcast_to(m, (RADIUS, HEAD_DIM))
            l_s[p, _rows(qs, RADIUS, dil), :] = jnp.broadcast_to(l, (RADIUS, HEAD_DIM))

    def values_pass(p, dil):
        for t, qs, ks in tiles(dil):
            v = vw[_rows(ks, 3 * RADIUS, dil), :].astype(BF16)
            num_s[p, _rows(qs, RADIUS, dil), :] = jnp.dot(e_s[p, t], v, preferred_element_type=F32)

    dils = [dil for _, dil in DIL_PATTERNS]
    logits_pass(0, dils[0])
    logits_pass(1, dils[1])
    softmax_pass(0, dils[0])
    values_pass(0, dils[0])
    logits_pass(2, dils[2])
    softmax_pass(1, dils[1])
    values_pass(1, dils[1])
    softmax_pass(2, dils[2])
    values_pass(2, dils[2])

    m_all = jnp.maximum(jnp.maximum(m_s[0], m_s[1]), m_s[2])
    num = jnp.zeros((DIL_BLOCK, HEAD_DIM), F32)
    den = jnp.zeros((DIL_BLOCK, HEAD_DIM), F32)
    for p in range(len(DIL_PATTERNS)):
        w = jnp.exp(m_s[p] - m_all)
        num = num + w * num_s[p]
        den = den + w * l_s[p]
    o_ref[...] = (num / den).astype(o_ref.dtype)


def _dilated_attention(qkv, bias, seq_lens):
    nt = qkv.shape[1]
    nblk = nt // DIL_BLOCK
    bounds = np.cumsum((0,) + seq_lens) // DIL_BLOCK
    first_blocks = tuple(int(b) for b in bounds[:-1])
    last_blocks = tuple(int(b) - 1 for b in bounds[1:])
    blk = (1, DIL_BLOCK, HEAD_DIM)
    prev = lambda n: jnp.maximum(n - 1, 0)
    nxt = lambda n: jnp.minimum(n + 1, nblk - 1)
    npat = len(DIL_PATTERNS)
    kernel = functools.partial(_dil_kernel, first_blocks=first_blocks, last_blocks=last_blocks)
    return pl.pallas_call(
        kernel,
        grid=(N_HEADS_A, nblk),
        in_specs=[pl.BlockSpec(blk, lambda h, n: (h, n, 0)),
                  pl.BlockSpec(blk, lambda h, n: (N_HEADS_A + h, prev(n), 0)),
                  pl.BlockSpec(blk, lambda h, n: (N_HEADS_A + h, n, 0)),
                  pl.BlockSpec(blk, lambda h, n: (N_HEADS_A + h, nxt(n), 0)),
                  pl.BlockSpec(blk, lambda h, n: (2 * N_HEADS_A + h, prev(n), 0)),
                  pl.BlockSpec(blk, lambda h, n: (2 * N_HEADS_A + h, n, 0)),
                  pl.BlockSpec(blk, lambda h, n: (2 * N_HEADS_A + h, nxt(n), 0)),
                  pl.BlockSpec((1, npat, RADIUS, 3 * RADIUS), lambda h, n: (h, 0, 0, 0))],
        out_specs=pl.BlockSpec((DIL_BLOCK, HEAD_DIM), lambda h, n: (n, h)),
        out_shape=jax.ShapeDtypeStruct((nt, D_A), BF16),
        scratch_shapes=[pltpu.VMEM((DIL_BLOCK, HEAD_DIM), F32),
                        pltpu.VMEM((3 * DIL_BLOCK, HEAD_DIM), F32),
                        pltpu.VMEM((3 * DIL_BLOCK, HEAD_DIM), F32),
                        pltpu.VMEM((npat, DIL_BLOCK, HEAD_DIM), F32),
                        pltpu.VMEM((npat, DIL_BLOCK, HEAD_DIM), F32),
                        pltpu.VMEM((npat, DIL_BLOCK, HEAD_DIM), F32),
                        pltpu.VMEM((npat, DIL_BLOCK // RADIUS, RADIUS, 3 * RADIUS), F32),
                        pltpu.VMEM((npat, DIL_BLOCK // RADIUS, RADIUS, 3 * RADIUS), BF16)],
        compiler_params=_cparams(("parallel", "parallel"), 32),
        name="dilated_attention",
    )(qkv, qkv, qkv, qkv, qkv, qkv, qkv, bias)


def _na_bias(rpb):
    c = np.arange(GRID_W)
    cstart = np.clip(c - NA_COLS // 2, 0, GRID_W - NA_COLS)
    cmask = (c[None, :] >= cstart[:, None]) & (c[None, :] < cstart[:, None] + NA_COLS)
    coff = np.clip(c[None, :] - c[:, None], -(NA_COLS - 1), NA_COLS - 1) + NA_COLS - 1
    cols = jnp.full(rpb.shape[:2] + (GRID_W, GRID_W), NEG, F32)
    for o in range(2 * NA_COLS - 1):
        cols = jnp.where((cmask & (coff == o))[None, None], rpb[:, :, o, None, None].astype(F32), cols)
    tiles = [cols[:, NA_ROWS - 1 - dd:2 * NA_ROWS - 1 - dd] for dd in range(NA_ROWS)]
    b = jnp.stack(tiles, axis=1).transpose(0, 1, 3, 2, 4)
    return b.reshape(rpb.shape[0], NA_ROWS, GRID_W, NA_ROWS * GRID_W)


def _na_kernel(q_ref, kp_ref, kc_ref, kn_ref, vp_ref, vc_ref, vn_ref, bias_ref, o_ref, kw, vw,
               s_s, e_s, l_s, *, block_starts, seq_rows):
    n = pl.program_id(1)
    nstart = jnp.int32(block_starts[0])
    rows = jnp.int32(seq_rows[0])
    for s, r in zip(block_starts[1:], seq_rows[1:]):
        nstart = jnp.where(n >= s, s, nstart)
        rows = jnp.where(n >= s, r, rows)

    for t, (kr, vr) in enumerate(((kp_ref, vp_ref), (kc_ref, vc_ref), (kn_ref, vn_ref))):
        kw[t * NA_BLOCK:(t + 1) * NA_BLOCK, :] = kr[0]
        vw[t * NA_BLOCK:(t + 1) * NA_BLOCK, :] = vr[0]

    shifts, starts = [], []
    for i in range(NA_BLOCK_ROWS):
        r = (n - nstart) * NA_BLOCK_ROWS + i
        dd = r - jnp.clip(r - NA_ROWS // 2, 0, rows - NA_ROWS)
        shifts.append(dd)
        starts.append(pl.multiple_of((NA_BLOCK_ROWS + i - dd) * GRID_W, GRID_W))

    for i in range(NA_BLOCK_ROWS):
        k = kw[pl.ds(starts[i], NA_ROWS * GRID_W), :]
        q = q_ref[0, i * GRID_W:(i + 1) * GRID_W, :]
        s = lax.dot_general(q, k, (((1,), (1,)), ((), ())), preferred_element_type=F32)
        s_s[i] = s + bias_ref[0, shifts[i]]
    for i in range(NA_BLOCK_ROWS):
        s = s_s[i]
        m = jnp.max(s, axis=-1, keepdims=True)
        e = jnp.exp(s - m)
        l_s[i] = jnp.broadcast_to(jnp.sum(e, axis=-1, keepdims=True), (GRID_W, HEAD_DIM))
        e_s[i] = e.astype(BF16)
    for i in range(NA_BLOCK_ROWS):
        v = vw[pl.ds(starts[i], NA_ROWS * GRID_W), :]
        o = jnp.dot(e_s[i], v, preferred_element_type=F32) / l_s[i]
        o_ref[i * GRID_W:(i + 1) * GRID_W, :] = o.astype(o_ref.dtype)


def _neighborhood_attention(qkv, bias, seq_lens):
    nt = qkv.shape[1]
    nblk = nt // NA_BLOCK
    block_starts = tuple(int(s) // NA_BLOCK for s in np.cumsum((0,) + seq_lens[:-1]))
    seq_rows = tuple(int(s) // GRID_W for s in seq_lens)
    assert all(r >= NA_ROWS and r % NA_BLOCK_ROWS == 0 for r in seq_rows)
    blk = (1, NA_BLOCK, HEAD_DIM)
    prev = lambda n: jnp.maximum(n - 1, 0)
    nxt = lambda n: jnp.minimum(n + 1, nblk - 1)
    qh, kh, vh = 3 * N_HEADS_A, 3 * N_HEADS_A + N_HEADS_B, 3 * N_HEADS_A + 2 * N_HEADS_B
    kernel = functools.partial(_na_kernel, block_starts=block_starts, seq_rows=seq_rows)
    return pl.pallas_call(
        kernel,
        grid=(N_HEADS_B, nblk),
        in_specs=[pl.BlockSpec(blk, lambda h, n: (qh + h, n, 0)),
                  pl.BlockSpec(blk, lambda h, n: (kh + h, prev(n), 0)),
                  pl.BlockSpec(blk, lambda h, n: (kh + h, n, 0)),
                  pl.BlockSpec(blk, lambda h, n: (kh + h, nxt(n), 0)),
                  pl.BlockSpec(blk, lambda h, n: (vh + h, prev(n), 0)),
                  pl.BlockSpec(blk, lambda h, n: (vh + h, n, 0)),
                  pl.BlockSpec(blk, lambda h, n: (vh + h, nxt(n), 0)),
                  pl.BlockSpec((1, NA_ROWS, GRID_W, NA_ROWS * GRID_W), lambda h, n: (h, 0, 0, 0))],
        out_specs=pl.BlockSpec((NA_BLOCK, HEAD_DIM), lambda h, n: (n, h)),
        out_shape=jax.ShapeDtypeStruct((nt, D_B), BF16),
        scratch_shapes=[pltpu.VMEM((3 * NA_BLOCK, HEAD_DIM), BF16),
                        pltpu.VMEM((3 * NA_BLOCK, HEAD_DIM), BF16),
                        pltpu.VMEM((NA_BLOCK_ROWS, GRID_W, NA_ROWS * GRID_W), F32),
                        pltpu.VMEM((NA_BLOCK_ROWS, GRID_W, NA_ROWS * GRID_W), BF16),
                        pltpu.VMEM((NA_BLOCK_ROWS, GRID_W, HEAD_DIM), F32)],
        compiler_params=_cparams(("parallel", "parallel"), 32),
        name="neighborhood_attention",
    )(qkv, qkv, qkv, qkv, qkv, qkv, qkv, bias)


def _plain_norm(y, g):
    y = y.astype(F32)
    return (y * lax.rsqrt(jnp.mean(y * y, axis=-1, keepdims=True) + EPS) * g).astype(BF16)


OUT_ROW_CHUNK = 256


def _out_kernel(ya_ref, yb_ref, ga_ref, gb_ref, w_ref, x_ref, mod_ref, g2_ref, xo_ref, h_ref):
    for r0 in range(0, x_ref.shape[0], OUT_ROW_CHUNK):
        rows = slice(r0, r0 + OUT_ROW_CHUNK)
        y = jnp.dot(_plain_norm(ya_ref[rows, :], ga_ref[...]), w_ref[:D_A, :], preferred_element_type=F32)
        y = y + jnp.dot(_plain_norm(yb_ref[rows, :], gb_ref[...]), w_ref[D_A:, :],
                        preferred_element_type=F32)
        x = x_ref[rows, :] + mod_ref[0, 2:3, :] * y
        xo_ref[rows, :] = x
        h = _modulated_norm(x, g2_ref[...], mod_ref[0, 3:4, :], mod_ref[0, 4:5, :])
        h_ref[rows, :] = h.astype(BF16)


def _out_proj(ya, yb, ga, gb, w, x, mod_l, g2, seq_lens):
    nt, d = x.shape
    tm = 512
    starts = tuple(int(s) // tm for s in np.cumsum((0,) + seq_lens[:-1]))
    const = lambda i: (0, 0)
    return pl.pallas_call(
        _out_kernel,
        grid=(nt // tm,),
        in_specs=[pl.BlockSpec((tm, D_A), lambda i: (i, 0)),
                  pl.BlockSpec((tm, D_B), lambda i: (i, 0)),
                  pl.BlockSpec((1, D_A), const),
                  pl.BlockSpec((1, D_B), const),
                  pl.BlockSpec((D_A + D_B, d), const),
                  pl.BlockSpec((tm, d), lambda i: (i, 0)),
                  pl.BlockSpec((1, 6, d), lambda i: (_seq_index(i, starts), 0, 0)),
                  pl.BlockSpec((1, d), const)],
        out_specs=[pl.BlockSpec((tm, d), lambda i: (i, 0)),
                   pl.BlockSpec((tm, d), lambda i: (i, 0))],
        out_shape=[jax.ShapeDtypeStruct((nt, d), F32), jax.ShapeDtypeStruct((nt, d), BF16)],
        compiler_params=_cparams(("parallel",), 48),
        name="out_proj",
    )(ya, yb, ga, gb, w, x, mod_l, g2)


def _ffn_kernel(h_ref, wgu_ref, wd_ref, x_ref, mod_ref, ng_ref, *refs, final):
    acc_ref = refs[-1]
    f = pl.program_id(1)

    @pl.when(f == 0)
    def _():
        acc_ref[...] = jnp.zeros_like(acc_ref)

    r = jnp.dot(h_ref[...], wgu_ref[...], preferred_element_type=F32)
    acts = []
    for j in range(wd_ref.shape[0] // HEAD_DIM):
        a = r[:, 2 * j * HEAD_DIM:(2 * j + 1) * HEAD_DIM]
        b = r[:, (2 * j + 1) * HEAD_DIM:(2 * j + 2) * HEAD_DIM]
        acts.append((a / (1.0 + jnp.exp(-a)) * b).astype(BF16))
    act = jnp.concatenate(acts, axis=1)
    acc_ref[...] += jnp.dot(act, wd_ref[...], preferred_element_type=F32)

    @pl.when(f == pl.num_programs(1) - 1)
    def _():
        x = x_ref[...] + mod_ref[0, 5:6, :] * acc_ref[...]
        if final:
            y_ref, = refs[:-1]
            y_ref[...] = x * lax.rsqrt(jnp.mean(x * x, axis=-1, keepdims=True) + EPS) * ng_ref[...]
        else:
            nmod_ref, xo_ref, hn_ref = refs[:-1]
            xo_ref[...] = x
            hn = _modulated_norm(x, ng_ref[...], nmod_ref[0, 0:1, :], nmod_ref[0, 1:2, :])
            hn_ref[...] = hn.astype(BF16)


def _interleave_gate_up(wg, wu):
    d, f = wg.shape
    g = wg.astype(BF16).reshape(d, f // HEAD_DIM, 1, HEAD_DIM)
    u = wu.astype(BF16).reshape(d, f // HEAD_DIM, 1, HEAD_DIM)
    return jnp.concatenate([g, u], axis=2).reshape(d, 2 * f)


def _ffn(h, wgu, wd, x, mod_l, next_g, next_mod, seq_lens):
    nt, d = x.shape
    dff = wd.shape[0]
    tm, tf = 512, 512
    final = next_mod is None
    starts = tuple(int(s) // tm for s in np.cumsum((0,) + seq_lens[:-1]))
    row_blk = pl.BlockSpec((tm, d), lambda i, f: (i, 0))
    mod_blk = pl.BlockSpec((1, 6, d), lambda i, f: (_seq_index(i, starts), 0, 0))
    in_specs = [row_blk,
                pl.BlockSpec((d, 2 * tf), lambda i, f: (0, f)),
                pl.BlockSpec((tf, d), lambda i, f: (f, 0)),
                row_blk,
                mod_blk,
                pl.BlockSpec((1, d), lambda i, f: (0, 0))]
    operands = [h, wgu, wd, x, mod_l, next_g]
    if final:
        out_specs, out_shape = row_blk, jax.ShapeDtypeStruct((nt, d), F32)
    else:
        in_specs.append(mod_blk)
        operands.append(next_mod)
        out_specs = [row_blk, row_blk]
        out_shape = [jax.ShapeDtypeStruct((nt, d), F32), jax.ShapeDtypeStruct((nt, d), BF16)]
    return pl.pallas_call(
        functools.partial(_ffn_kernel, final=final),
        grid=(nt // tm, dff // tf),
        in_specs=in_specs,
        out_specs=out_specs,
        out_shape=out_shape,
        scratch_shapes=[pltpu.VMEM((tm, d), F32)],
        compiler_params=_cparams(("parallel", "arbitrary"), 48),
        name="ffn_final" if final else "ffn",
    )(*operands)


def _trunk(x, c, seq_lens, t5_table, norm1_g, norm2_g, w_ada, b_ada, w_in, out_norm_a, out_norm_b,
           w_out, na_rpb, w_gate, w_up, w_down, final_g):
    depth = w_in.shape[0]
    nseq = len(seq_lens)
    d = x.shape[1]
    assert all(s % DIL_BLOCK == 0 for s in seq_lens)

    c_pad = jnp.zeros((8, d), F32).at[:nseq].set(c)
    mod = _ada_modulation(c_pad, w_ada, b_ada)[:, :nseq].reshape(depth, nseq, 6, d)

    col_scale = jnp.ones((6, D_A), F32).at[0].set(QK_SCALE).at[3].set(QK_SCALE).reshape(1, 6 * D_A)
    dil_bias = _dilated_bias(t5_table)
    h = _prenorm(x, mod[0], norm1_g[0][None], seq_lens)
    for l in range(depth):
        final = l == depth - 1
        qkv = _qkv_proj(h, col_scale, w_in[l].astype(BF16))
        ya = _dilated_attention(qkv, dil_bias, seq_lens)
        yb = _neighborhood_attention(qkv, _na_bias(na_rpb[l]), seq_lens)
        x, h2 = _out_proj(ya, yb, out_norm_a[l][None], out_norm_b[l][None], w_out[l].astype(BF16),
                          x, mod[l], norm2_g[l][None], seq_lens)
        next_g = final_g[None] if final else norm1_g[l + 1][None]
        next_mod = None if final else mod[l + 1]
        out = _ffn(h2, _interleave_gate_up(w_gate[l], w_up[l]), w_down[l].astype(BF16), x, mod[l],
                   next_g, next_mod, seq_lens)
        if not final:
            x, h = out
    return out


def kernel(x_prompt, x_sample, c_prompt, c_sample, t5_table, norm1_g, norm2_g, w_ada, b_ada, w_in,
           out_norm_a, out_norm_b, w_out, na_rpb, w_gate, w_up, w_down, final_g):
    bp, tp, d = x_prompt.shape
    bs, ts, _ = x_sample.shape
    seq_lens = (tp,) * bp + (ts,) * bs
    x = jnp.concatenate([x_prompt.reshape(bp * tp, d), x_sample.reshape(bs * ts, d)], axis=0)
    c = jnp.concatenate([c_prompt, c_sample], axis=0)
    y = _trunk(x, c, seq_lens, t5_table, norm1_g, norm2_g, w_ada, b_ada, w_in, out_norm_a,
               out_norm_b, w_out, na_rpb, w_gate, w_up, w_down, final_g)
    return (y[:bp * tp].reshape(bp, tp, d), y[bp * tp:].reshape(bs, ts, d))
```

```python
import functools
import math

import numpy as np
import jax
import jax.numpy as jnp
from jax import lax
from jax.experimental import pallas as pl
from jax.experimental.pallas import tpu as pltpu

D_MODEL = 2048
HEAD_DIM = 128
N_HEADS_A = 8
N_HEADS_B = 8
D_A = N_HEADS_A * HEAD_DIM
D_B = N_HEADS_B * HEAD_DIM
QK_SCALE = HEAD_DIM ** -0.5
DILATIONS = (1, 4, 16)
RADIUS = 64
N_BUCKETS = 32
MAX_DISTANCE = 1024
GRID_W = 64
NA_ROWS = 8
NA_COLS = 16
EPS = 1e-6
NEG = -1e30

DIL_BLOCK = 1024
DIL_VIEW = 16
VIEW_ROWS = DIL_BLOCK // DIL_VIEW
TILES = DIL_BLOCK // RADIUS
NA_BLOCK_ROWS = 16
NA_HALO_ROWS = 8
NA_BLOCK = NA_BLOCK_ROWS * GRID_W
NA_HALO = NA_HALO_ROWS * GRID_W

F32 = jnp.float32
BF16 = jnp.bfloat16
MIB = 1024 * 1024


def _cparams(sem, vmem_mib):
    return pltpu.CompilerParams(dimension_semantics=sem, vmem_limit_bytes=vmem_mib * MIB)


def _seq_index(i, starts):
    idx = 0
    for s in starts[1:]:
        idx = idx + (i >= s).astype(jnp.int32)
    return idx


def _lanes(r):
    return slice(r * HEAD_DIM, (r + 1) * HEAD_DIM)


def _ada_kernel(c_ref, w_ref, b_ref, o_ref):
    c = c_ref[...]
    s = c / (1.0 + jnp.exp(-c))
    o_ref[0] = jnp.dot(s, w_ref[0], precision=lax.Precision.HIGHEST,
                       preferred_element_type=F32) + b_ref[0]


def _ada_modulation(c_pad, w_ada, b_ada):
    depth, d, n = w_ada.shape
    tn = 1024
    return pl.pallas_call(
        _ada_kernel,
        grid=(depth, n // tn),
        in_specs=[pl.BlockSpec((8, d), lambda l, j: (0, 0)),
                  pl.BlockSpec((1, d, tn), lambda l, j: (l, 0, j)),
                  pl.BlockSpec((1, 1, tn), lambda l, j: (l, 0, j))],
        out_specs=pl.BlockSpec((1, 8, tn), lambda l, j: (l, 0, j)),
        out_shape=jax.ShapeDtypeStruct((depth, 8, n), F32),
        compiler_params=_cparams(("parallel", "parallel"), 40),
        name="ada_modulation",
    )(c_pad, w_ada, b_ada.reshape(depth, 1, n))


def _modulated_norm(x, g, shift, scale):
    y = x * lax.rsqrt(jnp.mean(x * x, axis=-1, keepdims=True) + EPS) * g
    return y * (1.0 + scale) + shift


def _prenorm_kernel(x_ref, mod_ref, g_ref, h_ref):
    h = _modulated_norm(x_ref[...], g_ref[...], mod_ref[0, 0:1, :], mod_ref[0, 1:2, :])
    h_ref[...] = h.astype(BF16)


def _prenorm(x, mod_l, g, seq_lens):
    nt, d = x.shape
    tm = 512
    starts = tuple(int(s) // tm for s in np.cumsum((0,) + seq_lens[:-1]))
    return pl.pallas_call(
        _prenorm_kernel,
        grid=(nt // tm,),
        in_specs=[pl.BlockSpec((tm, d), lambda i: (i, 0)),
                  pl.BlockSpec((1, 6, d), lambda i: (_seq_index(i, starts), 0, 0)),
                  pl.BlockSpec((1, d), lambda i: (0, 0))],
        out_specs=pl.BlockSpec((tm, d), lambda i: (i, 0)),
        out_shape=jax.ShapeDtypeStruct((nt, d), BF16),
        compiler_params=_cparams(("parallel",), 32),
        name="prenorm",
    )(x, mod_l, g)


SLOT_KA, SLOT_VA, SLOT_QA, SLOT_QB, SLOT_KB, SLOT_VB = range(6)
QKV_SLOTS = 6


def _section_slot(j):
    return jnp.where(j == 0, SLOT_QA, jnp.where(j < 3, j - 1, j))


def _qkv_kernel(h_ref, cs_ref, w_ref, o_ref):
    r = jnp.dot(h_ref[...], w_ref[...], preferred_element_type=F32) * cs_ref[...]
    for c in range(o_ref.shape[0]):
        o_ref[c, 0] = r[:, _lanes(c)].astype(BF16)


def _qkv_proj(h, col_scale, w):
    nt, d = h.shape
    tm = 1024
    assert w.shape[1] == QKV_SLOTS * D_A and N_HEADS_A == N_HEADS_B
    return pl.pallas_call(
        _qkv_kernel,
        grid=(nt // tm, QKV_SLOTS),
        in_specs=[pl.BlockSpec((tm, d), lambda i, j: (i, 0)),
                  pl.BlockSpec((1, D_A), lambda i, j: (0, j)),
                  pl.BlockSpec((d, D_A), lambda i, j: (0, j))],
        out_specs=pl.BlockSpec((N_HEADS_A, 1, tm, HEAD_DIM), lambda i, j: (0, _section_slot(j), i, 0)),
        out_shape=jax.ShapeDtypeStruct((N_HEADS_A, QKV_SLOTS, nt, HEAD_DIM), BF16),
        compiler_params=_cparams(("parallel", "parallel"), 40),
        name="qkv_proj",
    )(h, col_scale, w)


def _t5_bucket(rel):
    nb = N_BUCKETS // 2
    exact = nb // 2
    n = np.abs(rel)
    sign = np.where(rel > 0, nb, 0)
    large = exact + (np.log(np.maximum(n, 1) / exact) / math.log(MAX_DISTANCE / exact) * (nb - exact)).astype(np.int64)
    large = np.minimum(large, nb - 1)
    return (sign + np.where(n < exact, n, large)).astype(np.int32)


def _tile_offsets():
    i = np.arange(RADIUS)[:, None]
    j = np.arange(3 * RADIUS)[None, :]
    band = j - RADIUS - i
    qa, qi = i // 16, i % 16
    ka, kj = j // 48, j % 48
    perm = 4 * (kj - 16 - qi) + (ka - qa)
    return (band, perm, band)


def _dilated_bias(t5_table):
    table = t5_table.astype(F32).T
    tiles = []
    for dil, rel in zip(DILATIONS, _tile_offsets()):
        inside = np.abs(rel) <= RADIUS
        bucket = _t5_bucket(rel * dil)
        b = jnp.full((table.shape[0],) + rel.shape, NEG, F32)
        for o in range(N_BUCKETS):
            b = jnp.where((inside & (bucket == o))[None], table[:, o, None, None], b)
        tiles.append(b)
    return jnp.stack(tiles, axis=1)


def _view_rows(ref_p, ref_c, ref_n, lo, hi, lanes):
    parts = []
    if lo < 0:
        parts.append(ref_p[VIEW_ROWS + lo:VIEW_ROWS, lanes])
    parts.append(ref_c[max(lo, 0):min(hi, VIEW_ROWS), lanes])
    if hi > VIEW_ROWS:
        parts.append(ref_n[0:hi - VIEW_ROWS, lanes])
    return parts


def _dil_kernel(q1_blk, kv1p_blk, kv1c_blk, kv1n_blk, q16_blk, kv16p_blk, kv16c_blk, kv16n_blk,
                bias_ref, o_ref, s_s, e_s, pos_s, num_s, m_s, l_s, *, first_blocks, last_blocks):
    n = pl.program_id(1)
    first = functools.reduce(jnp.logical_or, [n == s for s in first_blocks])
    last = functools.reduce(jnp.logical_or, [n == s for s in last_blocks])
    fpen = jnp.where(first, NEG, 0.0).astype(F32)
    lpen = jnp.where(last, NEG, 0.0).astype(F32)
    col = lax.broadcasted_iota(jnp.int32, (1, 3 * RADIUS), 1)
    pen_first = jnp.where(col < RADIUS, fpen, 0.0)
    pen_last = jnp.where(col >= 2 * RADIUS, lpen, 0.0)
    pen_perm_first = jnp.where(lax.rem(col, 48) < 16, fpen, 0.0)
    pen_perm_last = jnp.where(lax.rem(col, 48) >= 32, lpen, 0.0)
    ones = jnp.ones((3 * RADIUS, HEAD_DIM), BF16)

    q1_ref, q16_ref = q1_blk.at[0, 0], q16_blk.at[0, 0]
    k1 = tuple(blk.at[0, 0] for blk in (kv1p_blk, kv1c_blk, kv1n_blk))
    v1 = tuple(blk.at[0, 1] for blk in (kv1p_blk, kv1c_blk, kv1n_blk))
    k16 = tuple(blk.at[0, 0] for blk in (kv16p_blk, kv16c_blk, kv16n_blk))
    v16 = tuple(blk.at[0, 1] for blk in (kv16p_blk, kv16c_blk, kv16n_blk))

    def keys_d1(refs, b):
        ref_p, ref_c, ref_n = refs
        if b == 0:
            return jnp.concatenate([ref_p[...], ref_c[0:2 * RADIUS]], axis=0)
        if b == TILES - 1:
            return jnp.concatenate([ref_c[DIL_BLOCK - 2 * RADIUS:DIL_BLOCK], ref_n[...]], axis=0)
        return ref_c[RADIUS * (b - 1):RADIUS * (b + 2)]

    def keys_d4(refs, r4, b):
        parts = []
        for a in range(4):
            parts += _view_rows(*refs, 16 * b - 16, 16 * b + 32, _lanes(4 * a + r4))
        return jnp.concatenate(parts, axis=0)

    def keys_d16(refs, r):
        return jnp.concatenate([ref[:, _lanes(r)] for ref in refs], axis=0)

    def tile_operands(p, t):
        if p == 0:
            pen = pen_first if t == 0 else pen_last if t == TILES - 1 else None
            return q1_ref[RADIUS * t:RADIUS * (t + 1)], keys_d1(k1, t), keys_d1(v1, t), pen
        if p == 1:
            r4, b = divmod(t, 4)
            q = jnp.concatenate([q16_ref[16 * b:16 * b + 16, _lanes(4 * a + r4)] for a in range(4)], axis=0)
            pen = pen_perm_first if b == 0 else pen_perm_last if b == 3 else None
            return q, keys_d4(k16, r4, b), keys_d4(v16, r4, b), pen
        return q16_ref[:, _lanes(t)], keys_d16(k16, t), keys_d16(v16, t), pen_first + pen_last

    def store_rows(p, t, tile, pos_idx, view_ref):
        if p == 0:
            pos_s[pos_idx, RADIUS * t:RADIUS * (t + 1), :] = tile
        elif p == 1:
            r4, b = divmod(t, 4)
            for a in range(4):
                view_ref[0, 16 * b:16 * b + 16, _lanes(4 * a + r4)] = tile[16 * a:16 * a + 16]
        else:
            view_ref[1, :, _lanes(t)] = tile

    def logits_pass(p):
        for t in range(TILES):
            q, k, _, pen = tile_operands(p, t)
            s = lax.dot_general(q, k, (((1,), (1,)), ((), ())), preferred_element_type=F32)
            s = s + bias_ref[0, p]
            s_s[p, t] = s if pen is None else s + pen

    def softmax_pass(p):
        for t in range(TILES):
            s = s_s[p, t]
            m = jnp.max(s, axis=-1, keepdims=True)
            e_s[p, t] = jnp.exp(s - m).astype(BF16)
            store_rows(p, t, jnp.broadcast_to(m, (RADIUS, HEAD_DIM)), 1, m_s)

    def values_pass(p):
        for t in range(TILES):
            _, _, v, _ = tile_operands(p, t)
            nv = jnp.dot(e_s[p, t], jnp.concatenate([v, ones], axis=1), preferred_element_type=F32)
            store_rows(p, t, nv[:, :HEAD_DIM], 0, num_s)
            store_rows(p, t, nv[:, HEAD_DIM:], 2, l_s)

    logits_pass(2)
    logits_pass(1)
    softmax_pass(2)
    values_pass(2)
    logits_pass(0)
    softmax_pass(1)
    values_pass(1)
    softmax_pass(0)
    values_pass(0)

    for r in range(DIL_VIEW):
        strided = pl.ds(r, VIEW_ROWS, stride=DIL_VIEW)
        nums = (pos_s[0, strided, :], num_s[0, :, _lanes(r)], num_s[1, :, _lanes(r)])
        ms = (pos_s[1, strided, :], m_s[0, :, _lanes(r)], m_s[1, :, _lanes(r)])
        ls = (pos_s[2, strided, :], l_s[0, :, _lanes(r)], l_s[1, :, _lanes(r)])
        m_all = jnp.maximum(jnp.maximum(ms[0], ms[1]), ms[2])
        ws = [jnp.exp(m - m_all) for m in ms]
        num = ws[0] * nums[0] + ws[1] * nums[1] + ws[2] * nums[2]
        den = ws[0] * ls[0] + ws[1] * ls[1] + ws[2] * ls[2]
        o_ref[0, :, _lanes(r)] = (num / den).astype(o_ref.dtype)


def _dilated_attention(qkv, bias, seq_lens):
    nt = qkv.shape[2]
    nblk = nt // DIL_BLOCK
    qkv16 = qkv.reshape(qkv.shape[0], qkv.shape[1], nt // DIL_VIEW, DIL_VIEW * HEAD_DIM)
    bounds = np.cumsum((0,) + seq_lens) // DIL_BLOCK
    first_blocks = tuple(int(b) for b in bounds[:-1])
    last_blocks = tuple(int(b) - 1 for b in bounds[1:])
    per = DIL_BLOCK // RADIUS
    prev1 = lambda n: jnp.maximum(per * n - 1, 0)
    next1 = lambda n: jnp.minimum(per * (n + 1), nt // RADIUS - 1)
    prev16 = lambda n: jnp.maximum(n - 1, 0)
    next16 = lambda n: jnp.minimum(n + 1, nblk - 1)
    kv_blk = SLOT_KA // 2
    blk16 = (1, 1, VIEW_ROWS, DIL_VIEW * HEAD_DIM)
    kv16 = (1, 2, VIEW_ROWS, DIL_VIEW * HEAD_DIM)
    npat = len(DILATIONS)
    kernel = functools.partial(_dil_kernel, first_blocks=first_blocks, last_blocks=last_blocks)
    out = pl.pallas_call(
        kernel,
        grid=(N_HEADS_A, nblk),
        in_specs=[pl.BlockSpec((1, 1, DIL_BLOCK, HEAD_DIM), lambda h, n: (h, SLOT_QA, n, 0)),
                  pl.BlockSpec((1, 2, RADIUS, HEAD_DIM), lambda h, n: (h, kv_blk, prev1(n), 0)),
                  pl.BlockSpec((1, 2, DIL_BLOCK, HEAD_DIM), lambda h, n: (h, kv_blk, n, 0)),
                  pl.BlockSpec((1, 2, RADIUS, HEAD_DIM), lambda h, n: (h, kv_blk, next1(n), 0)),
                  pl.BlockSpec(blk16, lambda h, n: (h, SLOT_QA, n, 0)),
                  pl.BlockSpec(kv16, lambda h, n: (h, kv_blk, prev16(n), 0)),
                  pl.BlockSpec(kv16, lambda h, n: (h, kv_blk, n, 0)),
                  pl.BlockSpec(kv16, lambda h, n: (h, kv_blk, next16(n), 0)),
                  pl.BlockSpec((1, npat, RADIUS, 3 * RADIUS), lambda h, n: (h, 0, 0, 0))],
        out_specs=pl.BlockSpec((1, VIEW_ROWS, DIL_VIEW * HEAD_DIM), lambda h, n: (h, n, 0)),
        out_shape=jax.ShapeDtypeStruct((N_HEADS_A, nt // DIL_VIEW, DIL_VIEW * HEAD_DIM), BF16),
        scratch_shapes=[pltpu.VMEM((npat, TILES, RADIUS, 3 * RADIUS), F32),
                        pltpu.VMEM((npat, TILES, RADIUS, 3 * RADIUS), BF16),
                        pltpu.VMEM((3, DIL_BLOCK, HEAD_DIM), F32),
                        pltpu.VMEM((2, VIEW_ROWS, DIL_VIEW * HEAD_DIM), F32),
                        pltpu.VMEM((2, VIEW_ROWS, DIL_VIEW * HEAD_DIM), F32),
                        pltpu.VMEM((2, VIEW_ROWS, DIL_VIEW * HEAD_DIM), F32)],
        compiler_params=_cparams(("parallel", "parallel"), 40),
        name="dilated_attention",
    )(qkv, qkv, qkv, qkv, qkv16, qkv16, qkv16, qkv16, bias)
    return out.reshape(N_HEADS_A, nt, HEAD_DIM)


def _na_bias(rpb):
    c = np.arange(GRID_W)
    cstart = np.clip(c - NA_COLS // 2, 0, GRID_W - NA_COLS)
    cmask = (c[None, :] >= cstart[:, None]) & (c[None, :] < cstart[:, None] + NA_COLS)
    coff = np.clip(c[None, :] - c[:, None], -(NA_COLS - 1), NA_COLS - 1) + NA_COLS - 1
    cols = jnp.full(rpb.shape[:2] + (GRID_W, GRID_W), NEG, F32)
    for o in range(2 * NA_COLS - 1):
        cols = jnp.where((cmask & (coff == o))[None, None], rpb[:, :, o, None, None].astype(F32), cols)
    tiles = [cols[:, NA_ROWS - 1 - dd:2 * NA_ROWS - 1 - dd] for dd in range(NA_ROWS)]
    b = jnp.stack(tiles, axis=1).transpose(0, 1, 3, 2, 4)
    return b.reshape(rpb.shape[0], NA_ROWS, GRID_W, NA_ROWS * GRID_W)


def _na_kernel(q_blk, kvp_blk, kvc_blk, kvn_blk, bias_ref, o_ref, kw, vw,
               s_s, e_s, *, block_starts, seq_rows):
    n = pl.program_id(1)
    nstart = jnp.int32(block_starts[0])
    rows = jnp.int32(seq_rows[0])
    for s, r in zip(block_starts[1:], seq_rows[1:]):
        nstart = jnp.where(n >= s, s, nstart)
        rows = jnp.where(n >= s, r, rows)

    q_ref = q_blk.at[0, 0]
    for which, win in enumerate((kw, vw)):
        win[0:NA_HALO, :] = kvp_blk[0, which]
        win[NA_HALO:NA_HALO + NA_BLOCK, :] = kvc_blk[0, which]
        win[NA_HALO + NA_BLOCK:, :] = kvn_blk[0, which]

    nkeys = NA_ROWS * GRID_W
    shifts, starts = [], []
    for i in range(NA_BLOCK_ROWS):
        r = (n - nstart) * NA_BLOCK_ROWS + i
        dd = r - jnp.clip(r - NA_ROWS // 2, 0, rows - NA_ROWS)
        shifts.append(dd)
        starts.append(pl.multiple_of((NA_HALO_ROWS + i - dd) * GRID_W, GRID_W))

    for i in range(NA_BLOCK_ROWS):
        k = kw[pl.ds(starts[i], nkeys), :]
        q = q_ref[i * GRID_W:(i + 1) * GRID_W, :]
        s = lax.dot_general(q, k, (((1,), (1,)), ((), ())), preferred_element_type=F32)
        s_s[i] = s + bias_ref[0, shifts[i]]
    for i in range(NA_BLOCK_ROWS):
        s = s_s[i]
        e_s[i] = jnp.exp(s - jnp.max(s, axis=-1, keepdims=True)).astype(BF16)
    ones = jnp.ones((nkeys, HEAD_DIM), BF16)
    for i in range(NA_BLOCK_ROWS):
        v = vw[pl.ds(starts[i], nkeys), :]
        nv = jnp.dot(e_s[i], jnp.concatenate([v, ones], axis=1), preferred_element_type=F32)
        o_ref[i * GRID_W:(i + 1) * GRID_W, :] = (nv[:, :HEAD_DIM] / nv[:, HEAD_DIM:]).astype(o_ref.dtype)


def _neighborhood_attention(qkv, bias, seq_lens):
    nt = qkv.shape[2]
    block_starts = tuple(int(s) // NA_BLOCK for s in np.cumsum((0,) + seq_lens[:-1]))
    seq_rows = tuple(int(s) // GRID_W for s in seq_lens)
    assert all(r >= NA_ROWS and r % NA_BLOCK_ROWS == 0 for r in seq_rows)
    per = NA_BLOCK // NA_HALO
    prev = lambda n: jnp.maximum(per * n - 1, 0)
    nxt = lambda n: jnp.minimum(per * (n + 1), nt // NA_HALO - 1)
    kv_blk = SLOT_KB // 2
    kernel = functools.partial(_na_kernel, block_starts=block_starts, seq_rows=seq_rows)
    nkeys = NA_ROWS * GRID_W
    return pl.pallas_call(
        kernel,
        grid=(N_HEADS_B, nt // NA_BLOCK),
        in_specs=[pl.BlockSpec((1, 1, NA_BLOCK, HEAD_DIM), lambda h, n: (h, SLOT_QB, n, 0)),
                  pl.BlockSpec((1, 2, NA_HALO, HEAD_DIM), lambda h, n: (h, kv_blk, prev(n), 0)),
                  pl.BlockSpec((1, 2, NA_BLOCK, HEAD_DIM), lambda h, n: (h, kv_blk, n, 0)),
                  pl.BlockSpec((1, 2, NA_HALO, HEAD_DIM), lambda h, n: (h, kv_blk, nxt(n), 0)),
                  pl.BlockSpec((1, NA_ROWS, GRID_W, nkeys), lambda h, n: (h, 0, 0, 0))],
        out_specs=pl.BlockSpec((NA_BLOCK, HEAD_DIM), lambda h, n: (n, h)),
        out_shape=jax.ShapeDtypeStruct((nt, D_B), BF16),
        scratch_shapes=[pltpu.VMEM((NA_BLOCK + 2 * NA_HALO, HEAD_DIM), BF16),
                        pltpu.VMEM((NA_BLOCK + 2 * NA_HALO, HEAD_DIM), BF16),
                        pltpu.VMEM((NA_BLOCK_ROWS, GRID_W, nkeys), F32),
                        pltpu.VMEM((NA_BLOCK_ROWS, GRID_W, nkeys), BF16)],
        compiler_params=_cparams(("parallel", "parallel"), 32),
        name="neighborhood_attention",
    )(qkv, qkv, qkv, qkv, bias)


def _plain_norm(y, g):
    y = y.astype(F32)
    return (y * lax.rsqrt(jnp.mean(y * y, axis=-1, keepdims=True) + EPS) * g).astype(BF16)


OUT_ROW_CHUNK = 256


def _out_kernel(ya_ref, yb_ref, ga_ref, gb_ref, w_ref, x_ref, mod_ref, g2_ref, xo_ref, h_ref):
    for r0 in range(0, x_ref.shape[0], OUT_ROW_CHUNK):
        rows = slice(r0, r0 + OUT_ROW_CHUNK)
        ya = jnp.concatenate([ya_ref[h, rows, :] for h in range(N_HEADS_A)], axis=1)
        y = jnp.dot(_plain_norm(ya, ga_ref[...]), w_ref[:D_A, :], preferred_element_type=F32)
        y = y + jnp.dot(_plain_norm(yb_ref[rows, :], gb_ref[...]), w_ref[D_A:, :],
                        preferred_element_type=F32)
        x = x_ref[rows, :] + mod_ref[0, 2:3, :] * y
        xo_ref[rows, :] = x
        h = _modulated_norm(x, g2_ref[...], mod_ref[0, 3:4, :], mod_ref[0, 4:5, :])
        h_ref[rows, :] = h.astype(BF16)


def _out_proj(ya, yb, ga, gb, w, x, mod_l, g2, seq_lens):
    nt, d = x.shape
    tm = 512
    starts = tuple(int(s) // tm for s in np.cumsum((0,) + seq_lens[:-1]))
    const = lambda i: (0, 0)
    return pl.pallas_call(
        _out_kernel,
        grid=(nt // tm,),
        in_specs=[pl.BlockSpec((N_HEADS_A, tm, HEAD_DIM), lambda i: (0, i, 0)),
                  pl.BlockSpec((tm, D_B), lambda i: (i, 0)),
                  pl.BlockSpec((1, D_A), const),
                  pl.BlockSpec((1, D_B), const),
                  pl.BlockSpec((D_A + D_B, d), const),
                  pl.BlockSpec((tm, d), lambda i: (i, 0)),
                  pl.BlockSpec((1, 6, d), lambda i: (_seq_index(i, starts), 0, 0)),
                  pl.BlockSpec((1, d), const)],
        out_specs=[pl.BlockSpec((tm, d), lambda i: (i, 0)),
                   pl.BlockSpec((tm, d), lambda i: (i, 0))],
        out_shape=[jax.ShapeDtypeStruct((nt, d), F32), jax.ShapeDtypeStruct((nt, d), BF16)],
        compiler_params=_cparams(("parallel",), 48),
        name="out_proj",
    )(ya, yb, ga, gb, w, x, mod_l, g2)


def _ffn_kernel(h_ref, wg_ref, wu_ref, wd_ref, x_ref, mod_ref, ng_ref, *refs, final):
    acc_ref = refs[-1]
    f = pl.program_id(1)

    @pl.when(f == 0)
    def _():
        acc_ref[...] = jnp.zeros_like(acc_ref)

    h = h_ref[...]
    a = jnp.dot(h, wg_ref[...], preferred_element_type=F32)
    b = jnp.dot(h, wu_ref[...], preferred_element_type=F32)
    act = (a / (1.0 + jnp.exp(-a)) * b).astype(BF16)
    acc_ref[...] += jnp.dot(act, wd_ref[...], preferred_element_type=F32)

    @pl.when(f == pl.num_programs(1) - 1)
    def _():
        x = x_ref[...] + mod_ref[0, 5:6, :] * acc_ref[...]
        if final:
            y_ref, = refs[:-1]
            y_ref[...] = x * lax.rsqrt(jnp.mean(x * x, axis=-1, keepdims=True) + EPS) * ng_ref[...]
        else:
            nmod_ref, xo_ref, hn_ref = refs[:-1]
            xo_ref[...] = x
            hn = _modulated_norm(x, ng_ref[...], nmod_ref[0, 0:1, :], nmod_ref[0, 1:2, :])
            hn_ref[...] = hn.astype(BF16)


def _ffn(h, wg, wu, wd, x, mod_l, next_g, next_mod, seq_lens):
    nt, d = x.shape
    dff = wg.shape[1]
    tm, tf = 512, 512
    final = next_mod is None
    starts = tuple(int(s) // tm for s in np.cumsum((0,) + seq_lens[:-1]))
    row_blk = pl.BlockSpec((tm, d), lambda i, f: (i, 0))
    mod_blk = pl.BlockSpec((1, 6, d), lambda i, f: (_seq_index(i, starts), 0, 0))
    in_specs = [row_blk,
                pl.BlockSpec((d, tf), lambda i, f: (0, f)),
                pl.BlockSpec((d, tf), lambda i, f: (0, f)),
                pl.BlockSpec((tf, d), lambda i, f: (f, 0)),
                row_blk,
                mod_blk,
                pl.BlockSpec((1, d), lambda i, f: (0, 0))]
    operands = [h, wg, wu, wd, x, mod_l, next_g]
    if final:
        out_specs, out_shape = row_blk, jax.ShapeDtypeStruct((nt, d), F32)
    else:
        in_specs.append(mod_blk)
        operands.append(next_mod)
        out_specs = [row_blk, row_blk]
        out_shape = [jax.ShapeDtypeStruct((nt, d), F32), jax.ShapeDtypeStruct((nt, d), BF16)]
    return pl.pallas_call(
        functools.partial(_ffn_kernel, final=final),
        grid=(nt // tm, dff // tf),
        in_specs=in_specs,
        out_specs=out_specs,
        out_shape=out_shape,
        scratch_shapes=[pltpu.VMEM((tm, d), F32)],
        compiler_params=_cparams(("parallel", "arbitrary"), 48),
        name="ffn_final" if final else "ffn",
    )(*operands)


def _trunk(x, c, seq_lens, t5_table, norm1_g, norm2_g, w_ada, b_ada, w_in, out_norm_a, out_norm_b,
           w_out, na_rpb, w_gate, w_up, w_down, final_g):
    depth = w_in.shape[0]
    nseq = len(seq_lens)
    d = x.shape[1]
    assert all(s % DIL_BLOCK == 0 for s in seq_lens)

    c_pad = jnp.zeros((8, d), F32).at[:nseq].set(c)
    mod = _ada_modulation(c_pad, w_ada, b_ada)[:, :nseq].reshape(depth, nseq, 6, d)

    col_scale = jnp.ones((6, D_A), F32).at[0].set(QK_SCALE).at[3].set(QK_SCALE).reshape(1, 6 * D_A)
    dil_bias = _dilated_bias(t5_table)
    h = _prenorm(x, mod[0], norm1_g[0][None], seq_lens)
    for l in range(depth):
        final = l == depth - 1
        qkv = _qkv_proj(h, col_scale, w_in[l].astype(BF16))
        ya = _dilated_attention(qkv, dil_bias, seq_lens)
        yb = _neighborhood_attention(qkv, _na_bias(na_rpb[l]), seq_lens)
        x, h2 = _out_proj(ya, yb, out_norm_a[l][None], out_norm_b[l][None], w_out[l].astype(BF16),
                          x, mod[l], norm2_g[l][None], seq_lens)
        next_g = final_g[None] if final else norm1_g[l + 1][None]
        next_mod = None if final else mod[l + 1]
        out = _ffn(h2, w_gate[l].astype(BF16), w_up[l].astype(BF16), w_down[l].astype(BF16), x, mod[l],
                   next_g, next_mod, seq_lens)
        if not final:
            x, h = out
    return out


def kernel(x_prompt, x_sample, c_prompt, c_sample, t5_table, norm1_g, norm2_g, w_ada, b_ada, w_in,
           out_norm_a, out_norm_b, w_out, na_rpb, w_gate, w_up, w_down, final_g):
    bp, tp, d = x_prompt.shape
    bs, ts, _ = x_sample.shape
    seq_lens = (tp,) * bp + (ts,) * bs
    x = jnp.concatenate([x_prompt.reshape(bp * tp, d), x_sample.reshape(bs * ts, d)], axis=0)
    c = jnp.concatenate([c_prompt, c_sample], axis=0)
    y = _trunk(x, c, seq_lens, t5_table, norm1_g, norm2_g, w_ada, b_ada, w_in, out_norm_a,
               out_norm_b, w_out, na_rpb, w_gate, w_up, w_down, final_g)
    return (y[:bp * tp].reshape(bp, tp, d), y[bp * tp:].reshape(bs, ts, d))
```

```python
import functools
import math

import numpy as np
import jax
import jax.numpy as jnp
from jax import lax
from jax.experimental import pallas as pl
from jax.experimental.pallas import tpu as pltpu

D_MODEL = 2048
HEAD_DIM = 128
N_HEADS_A = 8
N_HEADS_B = 8
D_A = N_HEADS_A * HEAD_DIM
D_B = N_HEADS_B * HEAD_DIM
QK_SCALE = HEAD_DIM ** -0.5
DILATIONS = (1, 4, 16)
RADIUS = 64
N_BUCKETS = 32
MAX_DISTANCE = 1024
GRID_W = 64
NA_ROWS = 8
NA_COLS = 16
EPS = 1e-6
NEG = -1e30

DIL_BLOCK = 1024
DIL_VIEW = 16
VIEW_ROWS = DIL_BLOCK // DIL_VIEW
TILES = DIL_BLOCK // RADIUS
NA_BLOCK_ROWS = 16
NA_HALO_ROWS = 8
NA_BLOCK = NA_BLOCK_ROWS * GRID_W
NA_HALO = NA_HALO_ROWS * GRID_W

F32 = jnp.float32
BF16 = jnp.bfloat16
MIB = 1024 * 1024


def _cparams(sem, vmem_mib):
    return pltpu.CompilerParams(dimension_semantics=sem, vmem_limit_bytes=vmem_mib * MIB)


def _seq_index(i, starts):
    idx = 0
    for s in starts[1:]:
        idx = idx + (i >= s).astype(jnp.int32)
    return idx


def _lanes(r):
    return slice(r * HEAD_DIM, (r + 1) * HEAD_DIM)


def _ada_kernel(c_ref, w_ref, b_ref, o_ref):
    c = c_ref[...]
    s = c / (1.0 + jnp.exp(-c))
    o_ref[0] = jnp.dot(s, w_ref[0], precision=lax.Precision.HIGHEST,
                       preferred_element_type=F32) + b_ref[0]


def _ada_modulation(c_pad, w_ada, b_ada):
    depth, d, n = w_ada.shape
    tn = 1024
    return pl.pallas_call(
        _ada_kernel,
        grid=(depth, n // tn),
        in_specs=[pl.BlockSpec((8, d), lambda l, j: (0, 0)),
                  pl.BlockSpec((1, d, tn), lambda l, j: (l, 0, j)),
                  pl.BlockSpec((1, 1, tn), lambda l, j: (l, 0, j))],
        out_specs=pl.BlockSpec((1, 8, tn), lambda l, j: (l, 0, j)),
        out_shape=jax.ShapeDtypeStruct((depth, 8, n), F32),
        compiler_params=_cparams(("parallel", "parallel"), 40),
        name="ada_modulation",
    )(c_pad, w_ada, b_ada.reshape(depth, 1, n))


def _modulated_norm(x, g, shift, scale):
    y = x * lax.rsqrt(jnp.mean(x * x, axis=-1, keepdims=True) + EPS) * g
    return y * (1.0 + scale) + shift


def _prenorm_kernel(x_ref, mod_ref, g_ref, h_ref):
    h = _modulated_norm(x_ref[...], g_ref[...], mod_ref[0, 0:1, :], mod_ref[0, 1:2, :])
    h_ref[...] = h.astype(BF16)


def _prenorm(x, mod_l, g, seq_lens):
    nt, d = x.shape
    tm = 512
    starts = tuple(int(s) // tm for s in np.cumsum((0,) + seq_lens[:-1]))
    return pl.pallas_call(
        _prenorm_kernel,
        grid=(nt // tm,),
        in_specs=[pl.BlockSpec((tm, d), lambda i: (i, 0)),
                  pl.BlockSpec((1, 6, d), lambda i: (_seq_index(i, starts), 0, 0)),
                  pl.BlockSpec((1, d), lambda i: (0, 0))],
        out_specs=pl.BlockSpec((tm, d), lambda i: (i, 0)),
        out_shape=jax.ShapeDtypeStruct((nt, d), BF16),
        compiler_params=_cparams(("parallel",), 32),
        name="prenorm",
    )(x, mod_l, g)


SLOT_K, SLOT_V, SLOT_Q = range(3)
KV_BLOCK = SLOT_K // 2


def _qkv_kernel(h_ref, cs_ref, w_ref, o_ref, *view, tm):
    r = jnp.dot(h_ref[...], w_ref[...], preferred_element_type=F32) * cs_ref[...]
    for c in range(o_ref.shape[0]):
        o_ref[c, 0] = r[:, _lanes(c)].astype(BF16)
    if view:
        ov_ref, r_s = view
        for c in range(o_ref.shape[0]):
            r_s[c] = r[:, _lanes(c)]
            for g in range(DIL_VIEW):
                rows = r_s[c, pl.ds(g, tm // DIL_VIEW, stride=DIL_VIEW), :]
                ov_ref[c, 0, :, _lanes(g)] = rows.astype(BF16)


def _qkv_proj(h, col_scale, w, group, with_view):
    nt, d = h.shape
    tm = 1024
    width = N_HEADS_A * HEAD_DIM
    assert w.shape[1] == 6 * width and N_HEADS_A == N_HEADS_B
    slot = lambda j: lax.rem(j + 2, 3)
    out_specs = [pl.BlockSpec((N_HEADS_A, 1, tm, HEAD_DIM), lambda i, j: (0, slot(j), i, 0))]
    out_shape = [jax.ShapeDtypeStruct((N_HEADS_A, 3, nt, HEAD_DIM), BF16)]
    scratch = []
    if with_view:
        out_specs.append(pl.BlockSpec((N_HEADS_A, 1, tm // DIL_VIEW, DIL_VIEW * HEAD_DIM),
                                      lambda i, j: (0, slot(j), i, 0)))
        out_shape.append(jax.ShapeDtypeStruct((N_HEADS_A, 3, nt // DIL_VIEW, DIL_VIEW * HEAD_DIM), BF16))
        scratch.append(pltpu.VMEM((N_HEADS_A, tm, HEAD_DIM), F32))
    return pl.pallas_call(
        functools.partial(_qkv_kernel, tm=tm),
        grid=(nt // tm, 3),
        in_specs=[pl.BlockSpec((tm, d), lambda i, j: (i, 0)),
                  pl.BlockSpec((1, width), lambda i, j: (0, 3 * group + j)),
                  pl.BlockSpec((d, width), lambda i, j: (0, 3 * group + j))],
        out_specs=out_specs,
        out_shape=out_shape,
        scratch_shapes=scratch,
        compiler_params=_cparams(("parallel", "parallel"), 48),
        name="qkv_proj_view" if with_view else "qkv_proj",
    )(h, col_scale, w)


def _t5_bucket(rel):
    nb = N_BUCKETS // 2
    exact = nb // 2
    n = np.abs(rel)
    sign = np.where(rel > 0, nb, 0)
    large = exact + (np.log(np.maximum(n, 1) / exact) / math.log(MAX_DISTANCE / exact) * (nb - exact)).astype(np.int64)
    large = np.minimum(large, nb - 1)
    return (sign + np.where(n < exact, n, large)).astype(np.int32)


def _tile_offsets():
    i = np.arange(RADIUS)[:, None]
    j = np.arange(3 * RADIUS)[None, :]
    band = j - RADIUS - i
    qa, qi = i // 16, i % 16
    ka, kj = j // 48, j % 48
    perm = 4 * (kj - 16 - qi) + (ka - qa)
    return (band, perm, band)


def _dilated_bias(t5_table):
    table = t5_table.astype(F32).T
    tiles = []
    for dil, rel in zip(DILATIONS, _tile_offsets()):
        inside = np.abs(rel) <= RADIUS
        bucket = _t5_bucket(rel * dil)
        b = jnp.full((table.shape[0],) + rel.shape, NEG, F32)
        for o in range(N_BUCKETS):
            b = jnp.where((inside & (bucket == o))[None], table[:, o, None, None], b)
        tiles.append(b)
    return jnp.stack(tiles, axis=1)


def _view_rows(ref_p, ref_c, ref_n, lo, hi, lanes):
    parts = []
    if lo < 0:
        parts.append(ref_p[VIEW_ROWS + lo:VIEW_ROWS, lanes])
    parts.append(ref_c[max(lo, 0):min(hi, VIEW_ROWS), lanes])
    if hi > VIEW_ROWS:
        parts.append(ref_n[0:hi - VIEW_ROWS, lanes])
    return parts


def _dil_kernel(q1_blk, kv1p_blk, kv1c_blk, kv1n_blk, q16_blk, kv16p_blk, kv16c_blk, kv16n_blk,
                bias_ref, o_ref, s_s, e_s, pos_s, num_s, m_s, l_s, *, first_blocks, last_blocks):
    n = pl.program_id(1)
    first = functools.reduce(jnp.logical_or, [n == s for s in first_blocks])
    last = functools.reduce(jnp.logical_or, [n == s for s in last_blocks])
    fpen = jnp.where(first, NEG, 0.0).astype(F32)
    lpen = jnp.where(last, NEG, 0.0).astype(F32)
    col = lax.broadcasted_iota(jnp.int32, (1, 3 * RADIUS), 1)
    pen_first = jnp.where(col < RADIUS, fpen, 0.0)
    pen_last = jnp.where(col >= 2 * RADIUS, lpen, 0.0)
    pen_perm_first = jnp.where(lax.rem(col, 48) < 16, fpen, 0.0)
    pen_perm_last = jnp.where(lax.rem(col, 48) >= 32, lpen, 0.0)
    ones = jnp.ones((3 * RADIUS, HEAD_DIM), BF16)

    q1_ref, q16_ref = q1_blk.at[0, 0], q16_blk.at[0, 0]
    k1 = tuple(blk.at[0, 0] for blk in (kv1p_blk, kv1c_blk, kv1n_blk))
    v1 = tuple(blk.at[0, 1] for blk in (kv1p_blk, kv1c_blk, kv1n_blk))
    k16 = tuple(blk.at[0, 0] for blk in (kv16p_blk, kv16c_blk, kv16n_blk))
    v16 = tuple(blk.at[0, 1] for blk in (kv16p_blk, kv16c_blk, kv16n_blk))

    def keys_d1(refs, b):
        ref_p, ref_c, ref_n = refs
        if b == 0:
            return jnp.concatenate([ref_p[...], ref_c[0:2 * RADIUS]], axis=0)
        if b == TILES - 1:
            return jnp.concatenate([ref_c[DIL_BLOCK - 2 * RADIUS:DIL_BLOCK], ref_n[...]], axis=0)
        return ref_c[RADIUS * (b - 1):RADIUS * (b + 2)]

    def keys_d4(refs, r4, b):
        parts = []
        for a in range(4):
            parts += _view_rows(*refs, 16 * b - 16, 16 * b + 32, _lanes(4 * a + r4))
        return jnp.concatenate(parts, axis=0)

    def keys_d16(refs, r):
        return jnp.concatenate([ref[:, _lanes(r)] for ref in refs], axis=0)

    def tile_operands(p, t):
        if p == 0:
            pen = pen_first if t == 0 else pen_last if t == TILES - 1 else None
            return q1_ref[RADIUS * t:RADIUS * (t + 1)], keys_d1(k1, t), keys_d1(v1, t), pen
        if p == 1:
            r4, b = divmod(t, 4)
            q = jnp.concatenate([q16_ref[16 * b:16 * b + 16, _lanes(4 * a + r4)] for a in range(4)], axis=0)
            pen = pen_perm_first if b == 0 else pen_perm_last if b == 3 else None
            return q, keys_d4(k16, r4, b), keys_d4(v16, r4, b), pen
        return q16_ref[:, _lanes(t)], keys_d16(k16, t), keys_d16(v16, t), pen_first + pen_last

    def store_rows(p, t, tile, pos_idx, view_ref):
        if p == 0:
            pos_s[pos_idx, RADIUS * t:RADIUS * (t + 1), :] = tile
        elif p == 1:
            r4, b = divmod(t, 4)
            for a in range(4):
                view_ref[0, 16 * b:16 * b + 16, _lanes(4 * a + r4)] = tile[16 * a:16 * a + 16]
        else:
            view_ref[1, :, _lanes(t)] = tile

    def logits_pass(p):
        for t in range(TILES):
            q, k, _, pen = tile_operands(p, t)
            s = lax.dot_general(q, k, (((1,), (1,)), ((), ())), preferred_element_type=F32)
            s = s + bias_ref[0, p]
            s_s[p, t] = s if pen is None else s + pen

    def softmax_pass(p):
        for t in range(TILES):
            s = s_s[p, t]
            m = jnp.max(s, axis=-1, keepdims=True)
            e_s[p, t] = jnp.exp(s - m).astype(BF16)
            store_rows(p, t, jnp.broadcast_to(m, (RADIUS, HEAD_DIM)), 1, m_s)

    def values_pass(p):
        for t in range(TILES):
            _, _, v, _ = tile_operands(p, t)
            nv = jnp.dot(e_s[p, t], jnp.concatenate([v, ones], axis=1), preferred_element_type=F32)
            store_rows(p, t, nv[:, :HEAD_DIM], 0, num_s)
            store_rows(p, t, nv[:, HEAD_DIM:], 2, l_s)

    logits_pass(2)
    logits_pass(1)
    softmax_pass(2)
    values_pass(2)
    logits_pass(0)
    softmax_pass(1)
    values_pass(1)
    softmax_pass(0)
    values_pass(0)

    for r in range(DIL_VIEW):
        strided = pl.ds(r, VIEW_ROWS, stride=DIL_VIEW)
        nums = (pos_s[0, strided, :], num_s[0, :, _lanes(r)], num_s[1, :, _lanes(r)])
        ms = (pos_s[1, strided, :], m_s[0, :, _lanes(r)], m_s[1, :, _lanes(r)])
        ls = (pos_s[2, strided, :], l_s[0, :, _lanes(r)], l_s[1, :, _lanes(r)])
        m_all = jnp.maximum(jnp.maximum(ms[0], ms[1]), ms[2])
        ws = [jnp.exp(m - m_all) for m in ms]
        num = ws[0] * nums[0] + ws[1] * nums[1] + ws[2] * nums[2]
        den = ws[0] * ls[0] + ws[1] * ls[1] + ws[2] * ls[2]
        o_ref[0, :, _lanes(r)] = (num / den).astype(o_ref.dtype)


def _dilated_attention(qkv, qkv16, bias, seq_lens):
    nt = qkv.shape[2]
    nblk = nt // DIL_BLOCK
    bounds = np.cumsum((0,) + seq_lens) // DIL_BLOCK
    first_blocks = tuple(int(b) for b in bounds[:-1])
    last_blocks = tuple(int(b) - 1 for b in bounds[1:])
    per = DIL_BLOCK // RADIUS
    prev1 = lambda n: jnp.maximum(per * n - 1, 0)
    next1 = lambda n: jnp.minimum(per * (n + 1), nt // RADIUS - 1)
    prev16 = lambda n: jnp.maximum(n - 1, 0)
    next16 = lambda n: jnp.minimum(n + 1, nblk - 1)
    kv_blk = KV_BLOCK
    blk16 = (1, 1, VIEW_ROWS, DIL_VIEW * HEAD_DIM)
    kv16 = (1, 2, VIEW_ROWS, DIL_VIEW * HEAD_DIM)
    npat = len(DILATIONS)
    kernel = functools.partial(_dil_kernel, first_blocks=first_blocks, last_blocks=last_blocks)
    out = pl.pallas_call(
        kernel,
        grid=(N_HEADS_A, nblk),
        in_specs=[pl.BlockSpec((1, 1, DIL_BLOCK, HEAD_DIM), lambda h, n: (h, SLOT_Q, n, 0)),
                  pl.BlockSpec((1, 2, RADIUS, HEAD_DIM), lambda h, n: (h, kv_blk, prev1(n), 0)),
                  pl.BlockSpec((1, 2, DIL_BLOCK, HEAD_DIM), lambda h, n: (h, kv_blk, n, 0)),
                  pl.BlockSpec((1, 2, RADIUS, HEAD_DIM), lambda h, n: (h, kv_blk, next1(n), 0)),
                  pl.BlockSpec(blk16, lambda h, n: (h, SLOT_Q, n, 0)),
                  pl.BlockSpec(kv16, lambda h, n: (h, kv_blk, prev16(n), 0)),
                  pl.BlockSpec(kv16, lambda h, n: (h, kv_blk, n, 0)),
                  pl.BlockSpec(kv16, lambda h, n: (h, kv_blk, next16(n), 0)),
                  pl.BlockSpec((1, npat, RADIUS, 3 * RADIUS), lambda h, n: (h, 0, 0, 0))],
        out_specs=pl.BlockSpec((1, VIEW_ROWS, DIL_VIEW * HEAD_DIM), lambda h, n: (h, n, 0)),
        out_shape=jax.ShapeDtypeStruct((N_HEADS_A, nt // DIL_VIEW, DIL_VIEW * HEAD_DIM), BF16),
        scratch_shapes=[pltpu.VMEM((npat, TILES, RADIUS, 3 * RADIUS), F32),
                        pltpu.VMEM((npat, TILES, RADIUS, 3 * RADIUS), BF16),
                        pltpu.VMEM((3, DIL_BLOCK, HEAD_DIM), F32),
                        pltpu.VMEM((2, VIEW_ROWS, DIL_VIEW * HEAD_DIM), F32),
                        pltpu.VMEM((2, VIEW_ROWS, DIL_VIEW * HEAD_DIM), F32),
                        pltpu.VMEM((2, VIEW_ROWS, DIL_VIEW * HEAD_DIM), F32)],
        compiler_params=_cparams(("parallel", "parallel"), 40),
        name="dilated_attention",
    )(qkv, qkv, qkv, qkv, qkv16, qkv16, qkv16, qkv16, bias)
    return out.reshape(N_HEADS_A, nt, HEAD_DIM)


def _na_bias(rpb):
    c = np.arange(GRID_W)
    cstart = np.clip(c - NA_COLS // 2, 0, GRID_W - NA_COLS)
    cmask = (c[None, :] >= cstart[:, None]) & (c[None, :] < cstart[:, None] + NA_COLS)
    coff = np.clip(c[None, :] - c[:, None], -(NA_COLS - 1), NA_COLS - 1) + NA_COLS - 1
    cols = jnp.full(rpb.shape[:2] + (GRID_W, GRID_W), NEG, F32)
    for o in range(2 * NA_COLS - 1):
        cols = jnp.where((cmask & (coff == o))[None, None], rpb[:, :, o, None, None].astype(F32), cols)
    tiles = [cols[:, NA_ROWS - 1 - dd:2 * NA_ROWS - 1 - dd] for dd in range(NA_ROWS)]
    b = jnp.stack(tiles, axis=1).transpose(0, 1, 3, 2, 4)
    return b.reshape(rpb.shape[0], NA_ROWS, GRID_W, NA_ROWS * GRID_W)


def _na_kernel(q_blk, kvp_blk, kvc_blk, kvn_blk, bias_ref, o_ref, kw, vw,
               s_s, e_s, *, block_starts, seq_rows):
    n = pl.program_id(1)
    nstart = jnp.int32(block_starts[0])
    rows = jnp.int32(seq_rows[0])
    for s, r in zip(block_starts[1:], seq_rows[1:]):
        nstart = jnp.where(n >= s, s, nstart)
        rows = jnp.where(n >= s, r, rows)

    q_ref = q_blk.at[0, 0]
    for which, win in enumerate((kw, vw)):
        win[0:NA_HALO, :] = kvp_blk[0, which]
        win[NA_HALO:NA_HALO + NA_BLOCK, :] = kvc_blk[0, which]
        win[NA_HALO + NA_BLOCK:, :] = kvn_blk[0, which]

    nkeys = NA_ROWS * GRID_W
    shifts, starts = [], []
    for i in range(NA_BLOCK_ROWS):
        r = (n - nstart) * NA_BLOCK_ROWS + i
        dd = r - jnp.clip(r - NA_ROWS // 2, 0, rows - NA_ROWS)
        shifts.append(dd)
        starts.append(pl.multiple_of((NA_HALO_ROWS + i - dd) * GRID_W, GRID_W))

    for i in range(NA_BLOCK_ROWS):
        k = kw[pl.ds(starts[i], nkeys), :]
        q = q_ref[i * GRID_W:(i + 1) * GRID_W, :]
        s = lax.dot_general(q, k, (((1,), (1,)), ((), ())), preferred_element_type=F32)
        s_s[i] = s + bias_ref[0, shifts[i]]
    for i in range(NA_BLOCK_ROWS):
        s = s_s[i]
        e_s[i] = jnp.exp(s - jnp.max(s, axis=-1, keepdims=True)).astype(BF16)
    ones = jnp.ones((nkeys, HEAD_DIM), BF16)
    for i in range(NA_BLOCK_ROWS):
        v = vw[pl.ds(starts[i], nkeys), :]
        nv = jnp.dot(e_s[i], jnp.concatenate([v, ones], axis=1), preferred_element_type=F32)
        o_ref[i * GRID_W:(i + 1) * GRID_W, :] = (nv[:, :HEAD_DIM] / nv[:, HEAD_DIM:]).astype(o_ref.dtype)


def _neighborhood_attention(qkv, bias, seq_lens):
    nt = qkv.shape[2]
    block_starts = tuple(int(s) // NA_BLOCK for s in np.cumsum((0,) + seq_lens[:-1]))
    seq_rows = tuple(int(s) // GRID_W for s in seq_lens)
    assert all(r >= NA_ROWS and r % NA_BLOCK_ROWS == 0 for r in seq_rows)
    per = NA_BLOCK // NA_HALO
    prev = lambda n: jnp.maximum(per * n - 1, 0)
    nxt = lambda n: jnp.minimum(per * (n + 1), nt // NA_HALO - 1)
    kv_blk = KV_BLOCK
    kernel = functools.partial(_na_kernel, block_starts=block_starts, seq_rows=seq_rows)
    nkeys = NA_ROWS * GRID_W
    return pl.pallas_call(
        kernel,
        grid=(N_HEADS_B, nt // NA_BLOCK),
        in_specs=[pl.BlockSpec((1, 1, NA_BLOCK, HEAD_DIM), lambda h, n: (h, SLOT_Q, n, 0)),
                  pl.BlockSpec((1, 2, NA_HALO, HEAD_DIM), lambda h, n: (h, kv_blk, prev(n), 0)),
                  pl.BlockSpec((1, 2, NA_BLOCK, HEAD_DIM), lambda h, n: (h, kv_blk, n, 0)),
                  pl.BlockSpec((1, 2, NA_HALO, HEAD_DIM), lambda h, n: (h, kv_blk, nxt(n), 0)),
                  pl.BlockSpec((1, NA_ROWS, GRID_W, nkeys), lambda h, n: (h, 0, 0, 0))],
        out_specs=pl.BlockSpec((NA_BLOCK, HEAD_DIM), lambda h, n: (n, h)),
        out_shape=jax.ShapeDtypeStruct((nt, D_B), BF16),
        scratch_shapes=[pltpu.VMEM((NA_BLOCK + 2 * NA_HALO, HEAD_DIM), BF16),
                        pltpu.VMEM((NA_BLOCK + 2 * NA_HALO, HEAD_DIM), BF16),
                        pltpu.VMEM((NA_BLOCK_ROWS, GRID_W, nkeys), F32),
                        pltpu.VMEM((NA_BLOCK_ROWS, GRID_W, nkeys), BF16)],
        compiler_params=_cparams(("parallel", "parallel"), 32),
        name="neighborhood_attention",
    )(qkv, qkv, qkv, qkv, bias)


def _plain_norm(y, g):
    y = y.astype(F32)
    return (y * lax.rsqrt(jnp.mean(y * y, axis=-1, keepdims=True) + EPS) * g).astype(BF16)


OUT_ROW_CHUNK = 256


def _out_kernel(ya_ref, yb_ref, ga_ref, gb_ref, w_ref, x_ref, mod_ref, g2_ref, xo_ref, h_ref):
    for r0 in range(0, x_ref.shape[0], OUT_ROW_CHUNK):
        rows = slice(r0, r0 + OUT_ROW_CHUNK)
        ya = jnp.concatenate([ya_ref[h, rows, :] for h in range(N_HEADS_A)], axis=1)
        y = jnp.dot(_plain_norm(ya, ga_ref[...]), w_ref[:D_A, :], preferred_element_type=F32)
        y = y + jnp.dot(_plain_norm(yb_ref[rows, :], gb_ref[...]), w_ref[D_A:, :],
                        preferred_element_type=F32)
        x = x_ref[rows, :] + mod_ref[0, 2:3, :] * y
        xo_ref[rows, :] = x
        h = _modulated_norm(x, g2_ref[...], mod_ref[0, 3:4, :], mod_ref[0, 4:5, :])
        h_ref[rows, :] = h.astype(BF16)


def _out_proj(ya, yb, ga, gb, w, x, mod_l, g2, seq_lens):
    nt, d = x.shape
    tm = 512
    starts = tuple(int(s) // tm for s in np.cumsum((0,) + seq_lens[:-1]))
    const = lambda i: (0, 0)
    return pl.pallas_call(
        _out_kernel,
        grid=(nt // tm,),
        in_specs=[pl.BlockSpec((N_HEADS_A, tm, HEAD_DIM), lambda i: (0, i, 0)),
                  pl.BlockSpec((tm, D_B), lambda i: (i, 0)),
                  pl.BlockSpec((1, D_A), const),
                  pl.BlockSpec((1, D_B), const),
                  pl.BlockSpec((D_A + D_B, d), const),
                  pl.BlockSpec((tm, d), lambda i: (i, 0)),
                  pl.BlockSpec((1, 6, d), lambda i: (_seq_index(i, starts), 0, 0)),
                  pl.BlockSpec((1, d), const)],
        out_specs=[pl.BlockSpec((tm, d), lambda i: (i, 0)),
                   pl.BlockSpec((tm, d), lambda i: (i, 0))],
        out_shape=[jax.ShapeDtypeStruct((nt, d), F32), jax.ShapeDtypeStruct((nt, d), BF16)],
        compiler_params=_cparams(("parallel",), 48),
        name="out_proj",
    )(ya, yb, ga, gb, w, x, mod_l, g2)


def _ffn_kernel(h_ref, wg_ref, wu_ref, wd_ref, x_ref, mod_ref, ng_ref, *refs, split_block):
    acc_ref = refs[-1]
    f = pl.program_id(1)

    @pl.when(f == 0)
    def _():
        acc_ref[...] = jnp.zeros_like(acc_ref)

    h = h_ref[...]
    a = jnp.dot(h, wg_ref[...], preferred_element_type=F32)
    b = jnp.dot(h, wu_ref[...], preferred_element_type=F32)
    act = (a / (1.0 + jnp.exp(-a)) * b).astype(BF16)
    acc_ref[...] += jnp.dot(act, wd_ref[...], preferred_element_type=F32)

    @pl.when(f == pl.num_programs(1) - 1)
    def _():
        x = x_ref[...] + mod_ref[0, 5:6, :] * acc_ref[...]
        if split_block is not None:
            y0_ref, y1_ref = refs[:-1]
            y = x * lax.rsqrt(jnp.mean(x * x, axis=-1, keepdims=True) + EPS) * ng_ref[...]
            i = pl.program_id(0)

            @pl.when(i < split_block)
            def _():
                y0_ref[...] = y

            @pl.when(i >= split_block)
            def _():
                y1_ref[...] = y
        else:
            nmod_ref, xo_ref, hn_ref = refs[:-1]
            xo_ref[...] = x
            hn = _modulated_norm(x, ng_ref[...], nmod_ref[0, 0:1, :], nmod_ref[0, 1:2, :])
            hn_ref[...] = hn.astype(BF16)


def _ffn(h, wg, wu, wd, x, mod_l, next_g, next_mod, seq_lens, out_split=None):
    nt, d = x.shape
    dff = wg.shape[1]
    tm, tf = 512, 512
    final = next_mod is None
    split_block = out_split // tm if final else None
    starts = tuple(int(s) // tm for s in np.cumsum((0,) + seq_lens[:-1]))
    row_blk = pl.BlockSpec((tm, d), lambda i, f: (i, 0))
    mod_blk = pl.BlockSpec((1, 6, d), lambda i, f: (_seq_index(i, starts), 0, 0))
    in_specs = [row_blk,
                pl.BlockSpec((d, tf), lambda i, f: (0, f)),
                pl.BlockSpec((d, tf), lambda i, f: (0, f)),
                pl.BlockSpec((tf, d), lambda i, f: (f, 0)),
                row_blk,
                mod_blk,
                pl.BlockSpec((1, d), lambda i, f: (0, 0))]
    operands = [h, wg, wu, wd, x, mod_l, next_g]
    if final:
        out_specs = [pl.BlockSpec((tm, d), lambda i, f: (jnp.minimum(i, split_block - 1), 0)),
                     pl.BlockSpec((tm, d), lambda i, f: (jnp.maximum(i - split_block, 0), 0))]
        out_shape = [jax.ShapeDtypeStruct((out_split, d), F32),
                     jax.ShapeDtypeStruct((nt - out_split, d), F32)]
    else:
        in_specs.append(mod_blk)
        operands.append(next_mod)
        out_specs = [row_blk, row_blk]
        out_shape = [jax.ShapeDtypeStruct((nt, d), F32), jax.ShapeDtypeStruct((nt, d), BF16)]
    return pl.pallas_call(
        functools.partial(_ffn_kernel, split_block=split_block),
        grid=(nt // tm, dff // tf),
        in_specs=in_specs,
        out_specs=out_specs,
        out_shape=out_shape,
        scratch_shapes=[pltpu.VMEM((tm, d), F32)],
        compiler_params=_cparams(("arbitrary" if final else "parallel", "arbitrary"), 56),
        name="ffn_final" if final else "ffn",
    )(*operands)


def _trunk(x, c, seq_lens, out_split, t5_table, norm1_g, norm2_g, w_ada, b_ada, w_in, out_norm_a,
           out_norm_b, w_out, na_rpb, w_gate, w_up, w_down, final_g):
    depth = w_in.shape[0]
    nseq = len(seq_lens)
    d = x.shape[1]
    assert all(s % DIL_BLOCK == 0 for s in seq_lens)

    c_pad = jnp.zeros((8, d), F32).at[:nseq].set(c)
    mod = _ada_modulation(c_pad, w_ada, b_ada)[:, :nseq].reshape(depth, nseq, 6, d)

    col_scale = jnp.ones((6, D_A), F32).at[0].set(QK_SCALE).at[3].set(QK_SCALE).reshape(1, 6 * D_A)
    dil_bias = _dilated_bias(t5_table)
    h = _prenorm(x, mod[0], norm1_g[0][None], seq_lens)
    for l in range(depth):
        final = l == depth - 1
        w_qkv = w_in[l].astype(BF16)
        qkv_a, qkv_a16 = _qkv_proj(h, col_scale, w_qkv, 0, True)
        qkv_b, = _qkv_proj(h, col_scale, w_qkv, 1, False)
        ya = _dilated_attention(qkv_a, qkv_a16, dil_bias, seq_lens)
        yb = _neighborhood_attention(qkv_b, _na_bias(na_rpb[l]), seq_lens)
        x, h2 = _out_proj(ya, yb, out_norm_a[l][None], out_norm_b[l][None], w_out[l].astype(BF16),
                          x, mod[l], norm2_g[l][None], seq_lens)
        next_g = final_g[None] if final else norm1_g[l + 1][None]
        next_mod = None if final else mod[l + 1]
        out = _ffn(h2, w_gate[l].astype(BF16), w_up[l].astype(BF16), w_down[l].astype(BF16), x, mod[l],
                   next_g, next_mod, seq_lens, out_split)
        if not final:
            x, h = out
    return out


def kernel(x_prompt, x_sample, c_prompt, c_sample, t5_table, norm1_g, norm2_g, w_ada, b_ada, w_in,
           out_norm_a, out_norm_b, w_out, na_rpb, w_gate, w_up, w_down, final_g):
    bp, tp, d = x_prompt.shape
    bs, ts, _ = x_sample.shape
    seq_lens = (tp,) * bp + (ts,) * bs
    x = jnp.concatenate([x_prompt.reshape(bp * tp, d), x_sample.reshape(bs * ts, d)], axis=0)
    c = jnp.concatenate([c_prompt, c_sample], axis=0)
    y_prompt, y_sample = _trunk(x, c, seq_lens, bp * tp, t5_table, norm1_g, norm2_g, w_ada, b_ada, w_in,
                                out_norm_a, out_norm_b, w_out, na_rpb, w_gate, w_up, w_down, final_g)
    return (y_prompt.reshape(bp, tp, d), y_sample.reshape(bs, ts, d))
```

```python
import functools
import math

import numpy as np
import jax
import jax.numpy as jnp
from jax import lax
from jax.experimental import pallas as pl
from jax.experimental.pallas import tpu as pltpu

D_MODEL = 2048
HEAD_DIM = 128
N_HEADS_A = 8
N_HEADS_B = 8
D_A = N_HEADS_A * HEAD_DIM
D_B = N_HEADS_B * HEAD_DIM
QK_SCALE = HEAD_DIM ** -0.5
DILATIONS = (1, 4, 16)
RADIUS = 64
N_BUCKETS = 32
MAX_DISTANCE = 1024
GRID_W = 64
NA_ROWS = 8
NA_COLS = 16
EPS = 1e-6
NEG = -1e30

DIL_BLOCK = 1024
DIL_VIEW = 16
VIEW_ROWS = DIL_BLOCK // DIL_VIEW
TILES = DIL_BLOCK // RADIUS
DIL_HEADS_PER_STEP = 2
NA_BLOCK_ROWS = 16
NA_HALO_ROWS = 8
NA_BLOCK = NA_BLOCK_ROWS * GRID_W
NA_HEADS_PER_STEP = 2
NA_HALO = NA_HALO_ROWS * GRID_W

F32 = jnp.float32
BF16 = jnp.bfloat16
MIB = 1024 * 1024


def _cparams(sem, vmem_mib):
    return pltpu.CompilerParams(dimension_semantics=sem, vmem_limit_bytes=vmem_mib * MIB)


def _seq_index(i, starts):
    idx = 0
    for s in starts[1:]:
        idx = idx + (i >= s).astype(jnp.int32)
    return idx


def _lanes(r):
    return slice(r * HEAD_DIM, (r + 1) * HEAD_DIM)


def _ada_kernel(c_ref, w_ref, b_ref, o_ref):
    c = c_ref[...]
    s = c / (1.0 + jnp.exp(-c))
    o_ref[0] = jnp.dot(s, w_ref[0], precision=lax.Precision.HIGHEST,
                       preferred_element_type=F32) + b_ref[0]


def _ada_modulation(c_pad, w_ada, b_ada):
    depth, d, n = w_ada.shape
    tn = 1024
    return pl.pallas_call(
        _ada_kernel,
        grid=(depth, n // tn),
        in_specs=[pl.BlockSpec((8, d), lambda l, j: (0, 0)),
                  pl.BlockSpec((1, d, tn), lambda l, j: (l, 0, j)),
                  pl.BlockSpec((1, 1, tn), lambda l, j: (l, 0, j))],
        out_specs=pl.BlockSpec((1, 8, tn), lambda l, j: (l, 0, j)),
        out_shape=jax.ShapeDtypeStruct((depth, 8, n), F32),
        compiler_params=_cparams(("parallel", "parallel"), 40),
        name="ada_modulation",
    )(c_pad, w_ada, b_ada.reshape(depth, 1, n))


def _modulated_norm(x, g, shift, scale):
    y = x * lax.rsqrt(jnp.mean(x * x, axis=-1, keepdims=True) + EPS) * g
    return y * (1.0 + scale) + shift


def _prenorm_kernel(x_ref, mod_ref, g_ref, h_ref):
    h = _modulated_norm(x_ref[...], g_ref[...], mod_ref[0, 0:1, :], mod_ref[0, 1:2, :])
    h_ref[...] = h.astype(BF16)


def _prenorm(x, mod_l, g, seq_lens):
    nt, d = x.shape
    tm = 512
    starts = tuple(int(s) // tm for s in np.cumsum((0,) + seq_lens[:-1]))
    return pl.pallas_call(
        _prenorm_kernel,
        grid=(nt // tm,),
        in_specs=[pl.BlockSpec((tm, d), lambda i: (i, 0)),
                  pl.BlockSpec((1, 6, d), lambda i: (_seq_index(i, starts), 0, 0)),
                  pl.BlockSpec((1, d), lambda i: (0, 0))],
        out_specs=pl.BlockSpec((tm, d), lambda i: (i, 0)),
        out_shape=jax.ShapeDtypeStruct((nt, d), BF16),
        compiler_params=_cparams(("parallel",), 32),
        name="prenorm",
    )(x, mod_l, g)


SLOT_K, SLOT_V, SLOT_Q = range(3)
KV_BLOCK = SLOT_K // 2


def _qkv_kernel(h_ref, cs_ref, w_ref, o_ref, *view, tm):
    r = jnp.dot(h_ref[...], w_ref[...], preferred_element_type=F32) * cs_ref[...]
    for c in range(o_ref.shape[0]):
        o_ref[c, 0] = r[:, _lanes(c)].astype(BF16)
    if view:
        ov_ref, r_s = view
        for c in range(o_ref.shape[0]):
            r_s[c] = r[:, _lanes(c)]
            for g in range(DIL_VIEW):
                rows = r_s[c, pl.ds(g, tm // DIL_VIEW, stride=DIL_VIEW), :]
                ov_ref[c, 0, :, _lanes(g)] = rows.astype(BF16)


def _qkv_proj(h, col_scale, w, layer, group, with_view):
    nt, d = h.shape
    tm = 1024
    width = N_HEADS_A * HEAD_DIM
    assert w.shape[2] == 6 * width and N_HEADS_A == N_HEADS_B
    slot = lambda j: lax.rem(j + 2, 3)
    out_specs = [pl.BlockSpec((N_HEADS_A, 1, tm, HEAD_DIM), lambda i, j: (0, slot(j), i, 0))]
    out_shape = [jax.ShapeDtypeStruct((N_HEADS_A, 3, nt, HEAD_DIM), BF16)]
    scratch = []
    if with_view:
        out_specs.append(pl.BlockSpec((N_HEADS_A, 1, tm // DIL_VIEW, DIL_VIEW * HEAD_DIM),
                                      lambda i, j: (0, slot(j), i, 0)))
        out_shape.append(jax.ShapeDtypeStruct((N_HEADS_A, 3, nt // DIL_VIEW, DIL_VIEW * HEAD_DIM), BF16))
        scratch.append(pltpu.VMEM((N_HEADS_A, tm, HEAD_DIM), F32))
    return pl.pallas_call(
        functools.partial(_qkv_kernel, tm=tm),
        grid=(nt // tm, 3),
        in_specs=[pl.BlockSpec((tm, d), lambda i, j: (i, 0)),
                  pl.BlockSpec((1, width), lambda i, j: (0, 3 * group + j)),
                  pl.BlockSpec((None, d, width), lambda i, j: (layer, 0, 3 * group + j))],
        out_specs=out_specs,
        out_shape=out_shape,
        scratch_shapes=scratch,
        compiler_params=_cparams(("parallel", "parallel"), 48),
        name="qkv_proj_view" if with_view else "qkv_proj",
    )(h, col_scale, w)


def _t5_bucket(rel):
    nb = N_BUCKETS // 2
    exact = nb // 2
    n = np.abs(rel)
    sign = np.where(rel > 0, nb, 0)
    large = exact + (np.log(np.maximum(n, 1) / exact) / math.log(MAX_DISTANCE / exact) * (nb - exact)).astype(np.int64)
    large = np.minimum(large, nb - 1)
    return (sign + np.where(n < exact, n, large)).astype(np.int32)


def _tile_offsets():
    i = np.arange(RADIUS)[:, None]
    j = np.arange(3 * RADIUS)[None, :]
    band = j - RADIUS - i
    qa, qi = i // 16, i % 16
    ka, kj = j // 48, j % 48
    perm = 4 * (kj - 16 - qi) + (ka - qa)
    return (band, perm, band)


def _dilated_bias(t5_table):
    table = t5_table.astype(F32).T
    tiles = []
    for dil, rel in zip(DILATIONS, _tile_offsets()):
        inside = np.abs(rel) <= RADIUS
        bucket = _t5_bucket(rel * dil)
        b = jnp.full((table.shape[0],) + rel.shape, NEG, F32)
        for o in range(N_BUCKETS):
            b = jnp.where((inside & (bucket == o))[None], table[:, o, None, None], b)
        tiles.append(b)
    return jnp.stack(tiles, axis=1)


def _view_rows(ref_p, ref_c, ref_n, lo, hi, lanes):
    parts = []
    if lo < 0:
        parts.append(ref_p[VIEW_ROWS + lo:VIEW_ROWS, lanes])
    parts.append(ref_c[max(lo, 0):min(hi, VIEW_ROWS), lanes])
    if hi > VIEW_ROWS:
        parts.append(ref_n[0:hi - VIEW_ROWS, lanes])
    return parts


def _dil_kernel(q1_blk, kv1p_blk, kv1c_blk, kv1n_blk, q16_blk, kv16p_blk, kv16c_blk, kv16n_blk,
                bias_ref, o_ref, s_all, e_all, pos_all, num_all, m_all_s, l_all,
                *, first_blocks, last_blocks):
    n = pl.program_id(1)
    first = functools.reduce(jnp.logical_or, [n == s for s in first_blocks])
    last = functools.reduce(jnp.logical_or, [n == s for s in last_blocks])
    fpen = jnp.where(first, NEG, 0.0).astype(F32)
    lpen = jnp.where(last, NEG, 0.0).astype(F32)
    col = lax.broadcasted_iota(jnp.int32, (1, 3 * RADIUS), 1)
    pen_first = jnp.where(col < RADIUS, fpen, 0.0)
    pen_last = jnp.where(col >= 2 * RADIUS, lpen, 0.0)
    pen_perm_first = jnp.where(lax.rem(col, 48) < 16, fpen, 0.0)
    pen_perm_last = jnp.where(lax.rem(col, 48) >= 32, lpen, 0.0)
    ones = jnp.ones((3 * RADIUS, HEAD_DIM), BF16)

    def keys_d1(refs, b):
        ref_p, ref_c, ref_n = refs
        if b == 0:
            return jnp.concatenate([ref_p[...], ref_c[0:2 * RADIUS]], axis=0)
        if b == TILES - 1:
            return jnp.concatenate([ref_c[DIL_BLOCK - 2 * RADIUS:DIL_BLOCK], ref_n[...]], axis=0)
        return ref_c[RADIUS * (b - 1):RADIUS * (b + 2)]

    def keys_d4(refs, r4, b):
        parts = []
        for a in range(4):
            parts += _view_rows(*refs, 16 * b - 16, 16 * b + 32, _lanes(4 * a + r4))
        return jnp.concatenate(parts, axis=0)

    def keys_d16(refs, r):
        return jnp.concatenate([ref[:, _lanes(r)] for ref in refs], axis=0)

    def head_passes(hh):
        q1_ref, q16_ref = q1_blk.at[hh, 0], q16_blk.at[hh, 0]
        k1 = tuple(blk.at[hh, 0] for blk in (kv1p_blk, kv1c_blk, kv1n_blk))
        v1 = tuple(blk.at[hh, 1] for blk in (kv1p_blk, kv1c_blk, kv1n_blk))
        k16 = tuple(blk.at[hh, 0] for blk in (kv16p_blk, kv16c_blk, kv16n_blk))
        v16 = tuple(blk.at[hh, 1] for blk in (kv16p_blk, kv16c_blk, kv16n_blk))
        s_s, e_s, pos_s = s_all.at[hh], e_all.at[hh], pos_all.at[hh]
        num_s, m_s, l_s = num_all.at[hh], m_all_s.at[hh], l_all.at[hh]

        def tile_operands(p, t):
            if p == 0:
                pen = pen_first if t == 0 else pen_last if t == TILES - 1 else None
                return q1_ref[RADIUS * t:RADIUS * (t + 1)], keys_d1(k1, t), keys_d1(v1, t), pen
            if p == 1:
                r4, b = divmod(t, 4)
                q = jnp.concatenate([q16_ref[16 * b:16 * b + 16, _lanes(4 * a + r4)] for a in range(4)],
                                    axis=0)
                pen = pen_perm_first if b == 0 else pen_perm_last if b == 3 else None
                return q, keys_d4(k16, r4, b), keys_d4(v16, r4, b), pen
            return q16_ref[:, _lanes(t)], keys_d16(k16, t), keys_d16(v16, t), pen_first + pen_last

        def store_rows(p, t, tile, pos_idx, view_ref):
            if p == 0:
                pos_s[pos_idx, RADIUS * t:RADIUS * (t + 1), :] = tile
            elif p == 1:
                r4, b = divmod(t, 4)
                for a in range(4):
                    view_ref[0, 16 * b:16 * b + 16, _lanes(4 * a + r4)] = tile[16 * a:16 * a + 16]
            else:
                view_ref[1, :, _lanes(t)] = tile

        def logits(p, t):
            q, k, _, pen = tile_operands(p, t)
            s = lax.dot_general(q, k, (((1,), (1,)), ((), ())), preferred_element_type=F32)
            s = s + bias_ref[hh, p]
            s_s[p, t] = s if pen is None else s + pen

        def softmax(p, t):
            s = s_s[p, t]
            m = jnp.max(s, axis=-1, keepdims=True)
            e_s[p, t] = jnp.exp(s - m).astype(BF16)
            store_rows(p, t, jnp.broadcast_to(m, (RADIUS, HEAD_DIM)), 1, m_s)

        def values(p, t):
            _, _, v, _ = tile_operands(p, t)
            nv = jnp.dot(e_s[p, t], jnp.concatenate([v, ones], axis=1), preferred_element_type=F32)
            store_rows(p, t, nv[:, :HEAD_DIM], 0, num_s)
            store_rows(p, t, nv[:, HEAD_DIM:], 2, l_s)

        def merge():
            for r in range(DIL_VIEW):
                strided = pl.ds(r, VIEW_ROWS, stride=DIL_VIEW)
                nums = (pos_s[0, strided, :], num_s[0, :, _lanes(r)], num_s[1, :, _lanes(r)])
                ms = (pos_s[1, strided, :], m_s[0, :, _lanes(r)], m_s[1, :, _lanes(r)])
                ls = (pos_s[2, strided, :], l_s[0, :, _lanes(r)], l_s[1, :, _lanes(r)])
                m_max = jnp.maximum(jnp.maximum(ms[0], ms[1]), ms[2])
                ws = [jnp.exp(m - m_max) for m in ms]
                num = ws[0] * nums[0] + ws[1] * nums[1] + ws[2] * nums[2]
                den = ws[0] * ls[0] + ws[1] * ls[1] + ws[2] * ls[2]
                o_ref[hh, :, _lanes(r)] = (num / den).astype(o_ref.dtype)

        def all_tiles(stage, p):
            return lambda: [stage(p, t) for t in range(TILES)]

        return [all_tiles(logits, 2), all_tiles(logits, 1), all_tiles(softmax, 2), all_tiles(values, 2),
                all_tiles(logits, 0), all_tiles(softmax, 1), all_tiles(values, 1), all_tiles(softmax, 0),
                all_tiles(values, 0), merge]

    first_head, second_head = head_passes(0), head_passes(1)
    for run in first_head[:7] + second_head[:2] + first_head[7:9] + second_head[2:5] + first_head[9:] + second_head[5:]:
        run()


def _dilated_attention(qkv, qkv16, bias, seq_lens):
    nt = qkv.shape[2]
    nblk = nt // DIL_BLOCK
    bounds = np.cumsum((0,) + seq_lens) // DIL_BLOCK
    first_blocks = tuple(int(b) for b in bounds[:-1])
    last_blocks = tuple(int(b) - 1 for b in bounds[1:])
    per = DIL_BLOCK // RADIUS
    prev1 = lambda n: jnp.maximum(per * n - 1, 0)
    next1 = lambda n: jnp.minimum(per * (n + 1), nt // RADIUS - 1)
    prev16 = lambda n: jnp.maximum(n - 1, 0)
    next16 = lambda n: jnp.minimum(n + 1, nblk - 1)
    kv_blk = KV_BLOCK
    hps = DIL_HEADS_PER_STEP
    blk16 = (hps, 1, VIEW_ROWS, DIL_VIEW * HEAD_DIM)
    kv16 = (hps, 2, VIEW_ROWS, DIL_VIEW * HEAD_DIM)
    npat = len(DILATIONS)
    kernel = functools.partial(_dil_kernel, first_blocks=first_blocks, last_blocks=last_blocks)
    out = pl.pallas_call(
        kernel,
        grid=(N_HEADS_A // hps, nblk),
        in_specs=[pl.BlockSpec((hps, 1, DIL_BLOCK, HEAD_DIM), lambda h, n: (h, SLOT_Q, n, 0)),
                  pl.BlockSpec((hps, 2, RADIUS, HEAD_DIM), lambda h, n: (h, kv_blk, prev1(n), 0)),
                  pl.BlockSpec((hps, 2, DIL_BLOCK, HEAD_DIM), lambda h, n: (h, kv_blk, n, 0)),
                  pl.BlockSpec((hps, 2, RADIUS, HEAD_DIM), lambda h, n: (h, kv_blk, next1(n), 0)),
                  pl.BlockSpec(blk16, lambda h, n: (h, SLOT_Q, n, 0)),
                  pl.BlockSpec(kv16, lambda h, n: (h, kv_blk, prev16(n), 0)),
                  pl.BlockSpec(kv16, lambda h, n: (h, kv_blk, n, 0)),
                  pl.BlockSpec(kv16, lambda h, n: (h, kv_blk, next16(n), 0)),
                  pl.BlockSpec((hps, npat, RADIUS, 3 * RADIUS), lambda h, n: (h, 0, 0, 0))],
        out_specs=pl.BlockSpec((hps, VIEW_ROWS, DIL_VIEW * HEAD_DIM), lambda h, n: (h, n, 0)),
        out_shape=jax.ShapeDtypeStruct((N_HEADS_A, nt // DIL_VIEW, DIL_VIEW * HEAD_DIM), BF16),
        scratch_shapes=[pltpu.VMEM((hps, npat, TILES, RADIUS, 3 * RADIUS), F32),
                        pltpu.VMEM((hps, npat, TILES, RADIUS, 3 * RADIUS), BF16),
                        pltpu.VMEM((hps, 3, DIL_BLOCK, HEAD_DIM), F32),
                        pltpu.VMEM((hps, 2, VIEW_ROWS, DIL_VIEW * HEAD_DIM), F32),
                        pltpu.VMEM((hps, 2, VIEW_ROWS, DIL_VIEW * HEAD_DIM), F32),
                        pltpu.VMEM((hps, 2, VIEW_ROWS, DIL_VIEW * HEAD_DIM), F32)],
        compiler_params=_cparams(("parallel", "parallel"), 48),
        name="dilated_attention",
    )(qkv, qkv, qkv, qkv, qkv16, qkv16, qkv16, qkv16, bias)
    return out.reshape(N_HEADS_A, nt, HEAD_DIM)


def _na_bias(rpb):
    c = np.arange(GRID_W)
    cstart = np.clip(c - NA_COLS // 2, 0, GRID_W - NA_COLS)
    cmask = (c[None, :] >= cstart[:, None]) & (c[None, :] < cstart[:, None] + NA_COLS)
    coff = np.clip(c[None, :] - c[:, None], -(NA_COLS - 1), NA_COLS - 1) + NA_COLS - 1
    cols = jnp.full(rpb.shape[:2] + (GRID_W, GRID_W), NEG, F32)
    for o in range(2 * NA_COLS - 1):
        cols = jnp.where((cmask & (coff == o))[None, None], rpb[:, :, o, None, None].astype(F32), cols)
    tiles = [cols[:, NA_ROWS - 1 - dd:2 * NA_ROWS - 1 - dd] for dd in range(NA_ROWS)]
    b = jnp.stack(tiles, axis=1).transpose(0, 1, 3, 2, 4)
    return b.reshape(rpb.shape[0], NA_ROWS, GRID_W, NA_ROWS * GRID_W)


def _na_kernel(q_blk, kvp_blk, kvc_blk, kvn_blk, bias_ref, o_ref, kw, vw,
               s_s, e_s, *, block_starts, seq_rows):
    n = pl.program_id(1)
    nstart = jnp.int32(block_starts[0])
    rows = jnp.int32(seq_rows[0])
    for s, r in zip(block_starts[1:], seq_rows[1:]):
        nstart = jnp.where(n >= s, s, nstart)
        rows = jnp.where(n >= s, r, rows)

    nkeys = NA_ROWS * GRID_W
    shifts, starts = [], []
    for i in range(NA_BLOCK_ROWS):
        r = (n - nstart) * NA_BLOCK_ROWS + i
        dd = r - jnp.clip(r - NA_ROWS // 2, 0, rows - NA_ROWS)
        shifts.append(dd)
        starts.append(pl.multiple_of((NA_HALO_ROWS + i - dd) * GRID_W, GRID_W))
    ones = jnp.ones((nkeys, HEAD_DIM), BF16)

    def head_passes(hh):
        q_ref = q_blk.at[hh, 0]

        def logits_pass():
            for which, win in enumerate((kw, vw)):
                win[hh, 0:NA_HALO, :] = kvp_blk[hh, which]
                win[hh, NA_HALO:NA_HALO + NA_BLOCK, :] = kvc_blk[hh, which]
                win[hh, NA_HALO + NA_BLOCK:, :] = kvn_blk[hh, which]
            for i in range(NA_BLOCK_ROWS):
                k = kw[hh, pl.ds(starts[i], nkeys), :]
                q = q_ref[i * GRID_W:(i + 1) * GRID_W, :]
                s = lax.dot_general(q, k, (((1,), (1,)), ((), ())), preferred_element_type=F32)
                s_s[hh, i] = s + bias_ref[hh, shifts[i]]

        def softmax_pass():
            for i in range(NA_BLOCK_ROWS):
                s = s_s[hh, i]
                e_s[hh, i] = jnp.exp(s - jnp.max(s, axis=-1, keepdims=True)).astype(BF16)

        def values_pass():
            for i in range(NA_BLOCK_ROWS):
                v = vw[hh, pl.ds(starts[i], nkeys), :]
                nv = jnp.dot(e_s[hh, i], jnp.concatenate([v, ones], axis=1), preferred_element_type=F32)
                out = (nv[:, :HEAD_DIM] / nv[:, HEAD_DIM:]).astype(o_ref.dtype)
                o_ref[i * GRID_W:(i + 1) * GRID_W, _lanes(hh)] = out

        return [logits_pass, softmax_pass, values_pass]

    first_head, second_head = head_passes(0), head_passes(1)
    for run in (first_head[0], second_head[0], first_head[1], first_head[2], second_head[1], second_head[2]):
        run()


def _neighborhood_attention(qkv, bias, seq_lens):
    nt = qkv.shape[2]
    block_starts = tuple(int(s) // NA_BLOCK for s in np.cumsum((0,) + seq_lens[:-1]))
    seq_rows = tuple(int(s) // GRID_W for s in seq_lens)
    assert all(r >= NA_ROWS and r % NA_BLOCK_ROWS == 0 for r in seq_rows)
    per = NA_BLOCK // NA_HALO
    prev = lambda n: jnp.maximum(per * n - 1, 0)
    nxt = lambda n: jnp.minimum(per * (n + 1), nt // NA_HALO - 1)
    kv_blk = KV_BLOCK
    kernel = functools.partial(_na_kernel, block_starts=block_starts, seq_rows=seq_rows)
    nkeys = NA_ROWS * GRID_W
    hps = NA_HEADS_PER_STEP
    return pl.pallas_call(
        kernel,
        grid=(N_HEADS_B // hps, nt // NA_BLOCK),
        in_specs=[pl.BlockSpec((hps, 1, NA_BLOCK, HEAD_DIM), lambda h, n: (h, SLOT_Q, n, 0)),
                  pl.BlockSpec((hps, 2, NA_HALO, HEAD_DIM), lambda h, n: (h, kv_blk, prev(n), 0)),
                  pl.BlockSpec((hps, 2, NA_BLOCK, HEAD_DIM), lambda h, n: (h, kv_blk, n, 0)),
                  pl.BlockSpec((hps, 2, NA_HALO, HEAD_DIM), lambda h, n: (h, kv_blk, nxt(n), 0)),
                  pl.BlockSpec((hps, NA_ROWS, GRID_W, nkeys), lambda h, n: (h, 0, 0, 0))],
        out_specs=pl.BlockSpec((NA_BLOCK, hps * HEAD_DIM), lambda h, n: (n, h)),
        out_shape=jax.ShapeDtypeStruct((nt, D_B), BF16),
        scratch_shapes=[pltpu.VMEM((hps, NA_BLOCK + 2 * NA_HALO, HEAD_DIM), BF16),
                        pltpu.VMEM((hps, NA_BLOCK + 2 * NA_HALO, HEAD_DIM), BF16),
                        pltpu.VMEM((hps, NA_BLOCK_ROWS, GRID_W, nkeys), F32),
                        pltpu.VMEM((hps, NA_BLOCK_ROWS, GRID_W, nkeys), BF16)],
        compiler_params=_cparams(("parallel", "parallel"), 40),
        name="neighborhood_attention",
    )(qkv, qkv, qkv, qkv, bias)


def _plain_norm(y, g):
    y = y.astype(F32)
    return (y * lax.rsqrt(jnp.mean(y * y, axis=-1, keepdims=True) + EPS) * g).astype(BF16)


OUT_ROW_CHUNK = 256


def _out_kernel(ya_ref, yb_ref, ga_ref, gb_ref, w_ref, x_ref, mod_ref, g2_ref, xo_ref, h_ref):
    for r0 in range(0, x_ref.shape[0], OUT_ROW_CHUNK):
        rows = slice(r0, r0 + OUT_ROW_CHUNK)
        ya = jnp.concatenate([ya_ref[h, rows, :] for h in range(N_HEADS_A)], axis=1)
        y = jnp.dot(_plain_norm(ya, ga_ref[...]), w_ref[:D_A, :], preferred_element_type=F32)
        y = y + jnp.dot(_plain_norm(yb_ref[rows, :], gb_ref[...]), w_ref[D_A:, :],
                        preferred_element_type=F32)
        x = x_ref[rows, :] + mod_ref[0, 2:3, :] * y
        xo_ref[rows, :] = x
        h = _modulated_norm(x, g2_ref[...], mod_ref[0, 3:4, :], mod_ref[0, 4:5, :])
        h_ref[rows, :] = h.astype(BF16)


def _out_proj(ya, yb, ga, gb, w, layer, x, mod_l, g2, seq_lens):
    nt, d = x.shape
    tm = 512
    starts = tuple(int(s) // tm for s in np.cumsum((0,) + seq_lens[:-1]))
    const = lambda i: (0, 0)
    return pl.pallas_call(
        _out_kernel,
        grid=(nt // tm,),
        in_specs=[pl.BlockSpec((N_HEADS_A, tm, HEAD_DIM), lambda i: (0, i, 0)),
                  pl.BlockSpec((tm, D_B), lambda i: (i, 0)),
                  pl.BlockSpec((1, D_A), const),
                  pl.BlockSpec((1, D_B), const),
                  pl.BlockSpec((None, D_A + D_B, d), lambda i: (layer, 0, 0), pipeline_mode=pl.Buffered(1)),
                  pl.BlockSpec((tm, d), lambda i: (i, 0)),
                  pl.BlockSpec((1, 6, d), lambda i: (_seq_index(i, starts), 0, 0)),
                  pl.BlockSpec((1, d), const)],
        out_specs=[pl.BlockSpec((tm, d), lambda i: (i, 0)),
                   pl.BlockSpec((tm, d), lambda i: (i, 0))],
        out_shape=[jax.ShapeDtypeStruct((nt, d), F32), jax.ShapeDtypeStruct((nt, d), BF16)],
        compiler_params=_cparams(("parallel",), 48),
        name="out_proj",
    )(ya, yb, ga, gb, w, x, mod_l, g2)


def _ffn_kernel(h_ref, wg_ref, wu_ref, wd_ref, x_ref, mod_ref, ng_ref, *refs, split_block):
    acc_ref = refs[-1]
    f = pl.program_id(1)

    @pl.when(f == 0)
    def _():
        acc_ref[...] = jnp.zeros_like(acc_ref)

    h = h_ref[...]
    a = jnp.dot(h, wg_ref[...], preferred_element_type=F32)
    b = jnp.dot(h, wu_ref[...], preferred_element_type=F32)
    act = (a / (1.0 + jnp.exp(-a)) * b).astype(BF16)
    acc_ref[...] += jnp.dot(act, wd_ref[...], preferred_element_type=F32)

    @pl.when(f == pl.num_programs(1) - 1)
    def _():
        x = x_ref[...] + mod_ref[0, 5:6, :] * acc_ref[...]
        if split_block is not None:
            y0_ref, y1_ref = refs[:-1]
            y = x * lax.rsqrt(jnp.mean(x * x, axis=-1, keepdims=True) + EPS) * ng_ref[...]
            i = pl.program_id(0)

            @pl.when(i < split_block)
            def _():
                y0_ref[...] = y

            @pl.when(i >= split_block)
            def _():
                y1_ref[...] = y
        else:
            nmod_ref, xo_ref, hn_ref = refs[:-1]
            xo_ref[...] = x
            hn = _modulated_norm(x, ng_ref[...], nmod_ref[0, 0:1, :], nmod_ref[0, 1:2, :])
            hn_ref[...] = hn.astype(BF16)


def _ffn(h, wg, wu, wd, layer, x, mod_l, next_g, next_mod, seq_lens, out_split=None):
    nt, d = x.shape
    dff = wg.shape[2]
    tm, tf = 512, 512
    final = next_mod is None
    split_block = out_split // tm if final else None
    starts = tuple(int(s) // tm for s in np.cumsum((0,) + seq_lens[:-1]))
    row_blk = pl.BlockSpec((tm, d), lambda i, f: (i, 0))
    mod_blk = pl.BlockSpec((1, 6, d), lambda i, f: (_seq_index(i, starts), 0, 0))
    in_specs = [row_blk,
                pl.BlockSpec((None, d, tf), lambda i, f: (layer, 0, f)),
                pl.BlockSpec((None, d, tf), lambda i, f: (layer, 0, f)),
                pl.BlockSpec((None, tf, d), lambda i, f: (layer, f, 0)),
                row_blk,
                mod_blk,
                pl.BlockSpec((1, d), lambda i, f: (0, 0))]
    operands = [h, wg, wu, wd, x, mod_l, next_g]
    if final:
        out_specs = [pl.BlockSpec((tm, d), lambda i, f: (jnp.minimum(i, split_block - 1), 0)),
                     pl.BlockSpec((tm, d), lambda i, f: (jnp.maximum(i - split_block, 0), 0))]
        out_shape = [jax.ShapeDtypeStruct((out_split, d), F32),
                     jax.ShapeDtypeStruct((nt - out_split, d), F32)]
    else:
        in_specs.append(mod_blk)
        operands.append(next_mod)
        out_specs = [row_blk, row_blk]
        out_shape = [jax.ShapeDtypeStruct((nt, d), F32), jax.ShapeDtypeStruct((nt, d), BF16)]
    return pl.pallas_call(
        functools.partial(_ffn_kernel, split_block=split_block),
        grid=(nt // tm, dff // tf),
        in_specs=in_specs,
        out_specs=out_specs,
        out_shape=out_shape,
        scratch_shapes=[pltpu.VMEM((tm, d), F32)],
        compiler_params=_cparams(("arbitrary" if final else "parallel", "arbitrary"), 56),
        name="ffn_final" if final else "ffn",
    )(*operands)


def _trunk(x, c, seq_lens, out_split, t5_table, norm1_g, norm2_g, w_ada, b_ada, w_in, out_norm_a,
           out_norm_b, w_out, na_rpb, w_gate, w_up, w_down, final_g):
    depth = w_in.shape[0]
    nseq = len(seq_lens)
    d = x.shape[1]
    assert all(s % DIL_BLOCK == 0 for s in seq_lens)

    c_pad = jnp.zeros((8, d), F32).at[:nseq].set(c)
    mod = _ada_modulation(c_pad, w_ada, b_ada)[:, :nseq].reshape(depth, nseq, 6, d)

    col_scale = jnp.ones((6, D_A), F32).at[0].set(QK_SCALE).at[3].set(QK_SCALE).reshape(1, 6 * D_A)
    dil_bias = _dilated_bias(t5_table)
    w_in, w_out, w_gate, w_up, w_down = (w.astype(BF16) for w in (w_in, w_out, w_gate, w_up, w_down))
    h = _prenorm(x, mod[0], norm1_g[0][None], seq_lens)
    for l in range(depth):
        final = l == depth - 1
        qkv_a, qkv_a16 = _qkv_proj(h, col_scale, w_in, l, 0, True)
        qkv_b, = _qkv_proj(h, col_scale, w_in, l, 1, False)
        ya = _dilated_attention(qkv_a, qkv_a16, dil_bias, seq_lens)
        yb = _neighborhood_attention(qkv_b, _na_bias(na_rpb[l]), seq_lens)
        x, h2 = _out_proj(ya, yb, out_norm_a[l][None], out_norm_b[l][None], w_out, l,
                          x, mod[l], norm2_g[l][None], seq_lens)
        next_g = final_g[None] if final else norm1_g[l + 1][None]
        next_mod = None if final else mod[l + 1]
        out = _ffn(h2, w_gate, w_up, w_down, l, x, mod[l], next_g, next_mod, seq_lens, out_split)
        if not final:
            x, h = out
    return out


def kernel(x_prompt, x_sample, c_prompt, c_sample, t5_table, norm1_g, norm2_g, w_ada, b_ada, w_in,
           out_norm_a, out_norm_b, w_out, na_rpb, w_gate, w_up, w_down, final_g):
    bp, tp, d = x_prompt.shape
    bs, ts, _ = x_sample.shape
    seq_lens = (tp,) * bp + (ts,) * bs
    x = jnp.concatenate([x_prompt.reshape(bp * tp, d), x_sample.reshape(bs * ts, d)], axis=0)
    c = jnp.concatenate([c_prompt, c_sample], axis=0)
    y_prompt, y_sample = _trunk(x, c, seq_lens, bp * tp, t5_table, norm1_g, norm2_g, w_ada, b_ada, w_in,
                                out_norm_a, out_norm_b, w_out, na_rpb, w_gate, w_up, w_down, final_g)
    return (y_prompt.reshape(bp, tp, d), y_sample.reshape(bs, ts, d))
```

```python
import functools
import math

import numpy as np
import jax
import jax.numpy as jnp
from jax import lax
from jax.experimental import pallas as pl
from jax.experimental.pallas import tpu as pltpu

D_MODEL = 2048
HEAD_DIM = 128
N_HEADS_A = 8
N_HEADS_B = 8
D_A = N_HEADS_A * HEAD_DIM
D_B = N_HEADS_B * HEAD_DIM
QK_SCALE = HEAD_DIM ** -0.5
LOG2E = math.log2(math.e)
DILATIONS = (1, 4, 16)
RADIUS = 64
N_BUCKETS = 32
MAX_DISTANCE = 1024
GRID_W = 64
NA_ROWS = 8
NA_COLS = 16
EPS = 1e-6
NEG = -1e30

DIL_BLOCK = 1024
DIL_VIEW = 16
VIEW_ROWS = DIL_BLOCK // DIL_VIEW
TILES = DIL_BLOCK // RADIUS
DIL_HEADS_PER_STEP = 2
NA_BLOCK_ROWS = 16
NA_HALO_ROWS = 8
NA_BLOCK = NA_BLOCK_ROWS * GRID_W
NA_HEADS_PER_STEP = 2
NA_HALO = NA_HALO_ROWS * GRID_W

F32 = jnp.float32
BF16 = jnp.bfloat16
MIB = 1024 * 1024


def _cparams(sem, vmem_mib):
    return pltpu.CompilerParams(dimension_semantics=sem, vmem_limit_bytes=vmem_mib * MIB)


def _seq_index(i, starts):
    idx = 0
    for s in starts[1:]:
        idx = idx + (i >= s).astype(jnp.int32)
    return idx


def _lanes(r):
    return slice(r * HEAD_DIM, (r + 1) * HEAD_DIM)


def _ada_kernel(c_ref, w_ref, b_ref, o_ref):
    c = c_ref[...]
    s = c / (1.0 + jnp.exp(-c))
    o_ref[0] = jnp.dot(s, w_ref[0], precision=lax.Precision.HIGHEST,
                       preferred_element_type=F32) + b_ref[0]


def _ada_modulation(c_pad, w_ada, b_ada):
    depth, d, n = w_ada.shape
    tn = 1024
    return pl.pallas_call(
        _ada_kernel,
        grid=(depth, n // tn),
        in_specs=[pl.BlockSpec((8, d), lambda l, j: (0, 0)),
                  pl.BlockSpec((1, d, tn), lambda l, j: (l, 0, j)),
                  pl.BlockSpec((1, 1, tn), lambda l, j: (l, 0, j))],
        out_specs=pl.BlockSpec((1, 8, tn), lambda l, j: (l, 0, j)),
        out_shape=jax.ShapeDtypeStruct((depth, 8, n), F32),
        compiler_params=_cparams(("parallel", "parallel"), 40),
        name="ada_modulation",
    )(c_pad, w_ada, b_ada.reshape(depth, 1, n))


def _modulated_norm(x, g, shift, scale):
    y = x * lax.rsqrt(jnp.mean(x * x, axis=-1, keepdims=True) + EPS) * g
    return y * (1.0 + scale) + shift


def _prenorm_kernel(x_ref, mod_ref, g_ref, h_ref):
    h = _modulated_norm(x_ref[...], g_ref[...], mod_ref[0, 0:1, :], mod_ref[0, 1:2, :])
    h_ref[...] = h.astype(BF16)


def _prenorm(x, mod_l, g, seq_lens):
    nt, d = x.shape
    tm = 512
    starts = tuple(int(s) // tm for s in np.cumsum((0,) + seq_lens[:-1]))
    return pl.pallas_call(
        _prenorm_kernel,
        grid=(nt // tm,),
        in_specs=[pl.BlockSpec((tm, d), lambda i: (i, 0)),
                  pl.BlockSpec((1, 6, d), lambda i: (_seq_index(i, starts), 0, 0)),
                  pl.BlockSpec((1, d), lambda i: (0, 0))],
        out_specs=pl.BlockSpec((tm, d), lambda i: (i, 0)),
        out_shape=jax.ShapeDtypeStruct((nt, d), BF16),
        compiler_params=_cparams(("parallel",), 32),
        name="prenorm",
    )(x, mod_l, g)


SLOT_K, SLOT_V, SLOT_Q = range(3)
KV_BLOCK = SLOT_K // 2


def _qkv_kernel(h_ref, cs_ref, w_ref, o_ref, *view, tm):
    r = jnp.dot(h_ref[...], w_ref[...], preferred_element_type=F32) * cs_ref[...]
    for c in range(o_ref.shape[0]):
        o_ref[c, 0] = r[:, _lanes(c)].astype(BF16)
    if view:
        ov_ref, r_s = view
        for c in range(o_ref.shape[0]):
            r_s[c] = r[:, _lanes(c)]
            for g in range(DIL_VIEW):
                rows = r_s[c, pl.ds(g, tm // DIL_VIEW, stride=DIL_VIEW), :]
                ov_ref[c, 0, :, _lanes(g)] = rows.astype(BF16)


def _qkv_proj(h, col_scale, w, layer, group, with_view):
    nt, d = h.shape
    tm = 1024
    width = N_HEADS_A * HEAD_DIM
    assert w.shape[2] == 6 * width and N_HEADS_A == N_HEADS_B
    slot = lambda j: lax.rem(j + 2, 3)
    out_specs = [pl.BlockSpec((N_HEADS_A, 1, tm, HEAD_DIM), lambda i, j: (0, slot(j), i, 0))]
    out_shape = [jax.ShapeDtypeStruct((N_HEADS_A, 3, nt, HEAD_DIM), BF16)]
    scratch = []
    if with_view:
        out_specs.append(pl.BlockSpec((N_HEADS_A, 1, tm // DIL_VIEW, DIL_VIEW * HEAD_DIM),
                                      lambda i, j: (0, slot(j), i, 0)))
        out_shape.append(jax.ShapeDtypeStruct((N_HEADS_A, 3, nt // DIL_VIEW, DIL_VIEW * HEAD_DIM), BF16))
        scratch.append(pltpu.VMEM((N_HEADS_A, tm, HEAD_DIM), F32))
    return pl.pallas_call(
        functools.partial(_qkv_kernel, tm=tm),
        grid=(nt // tm, 3),
        in_specs=[pl.BlockSpec((tm, d), lambda i, j: (i, 0)),
                  pl.BlockSpec((1, width), lambda i, j: (0, 3 * group + j)),
                  pl.BlockSpec((None, d, width), lambda i, j: (layer, 0, 3 * group + j))],
        out_specs=out_specs,
        out_shape=out_shape,
        scratch_shapes=scratch,
        compiler_params=_cparams(("parallel", "parallel"), 48),
        name="qkv_proj_view" if with_view else "qkv_proj",
    )(h, col_scale, w)


def _t5_bucket(rel):
    nb = N_BUCKETS // 2
    exact = nb // 2
    n = np.abs(rel)
    sign = np.where(rel > 0, nb, 0)
    large = exact + (np.log(np.maximum(n, 1) / exact) / math.log(MAX_DISTANCE / exact) * (nb - exact)).astype(np.int64)
    large = np.minimum(large, nb - 1)
    return (sign + np.where(n < exact, n, large)).astype(np.int32)


def _tile_offsets():
    i = np.arange(RADIUS)[:, None]
    j = np.arange(3 * RADIUS)[None, :]
    band = j - RADIUS - i
    qa, qi = i // 16, i % 16
    ka, kj = j // 48, j % 48
    perm = 4 * (kj - 16 - qi) + (ka - qa)
    return (band, perm, band)


def _dilated_bias(t5_table):
    table = t5_table.astype(F32).T * LOG2E
    tiles = []
    for dil, rel in zip(DILATIONS, _tile_offsets()):
        inside = np.abs(rel) <= RADIUS
        bucket = _t5_bucket(rel * dil)
        b = jnp.full((table.shape[0],) + rel.shape, NEG, F32)
        for o in range(N_BUCKETS):
            b = jnp.where((inside & (bucket == o))[None], table[:, o, None, None], b)
        tiles.append(b)
    return jnp.stack(tiles, axis=1)


def _view_rows(ref_p, ref_c, ref_n, lo, hi, lanes):
    parts = []
    if lo < 0:
        parts.append(ref_p[VIEW_ROWS + lo:VIEW_ROWS, lanes])
    parts.append(ref_c[max(lo, 0):min(hi, VIEW_ROWS), lanes])
    if hi > VIEW_ROWS:
        parts.append(ref_n[0:hi - VIEW_ROWS, lanes])
    return parts


def _dil_kernel(q1_blk, kv1p_blk, kv1c_blk, kv1n_blk, q16_blk, kv16p_blk, kv16c_blk, kv16n_blk,
                bias_ref, o_ref, s_all, e_all, pos_all, num_all, m_all_s, l_all,
                *, first_blocks, last_blocks):
    n = pl.program_id(1)
    first = functools.reduce(jnp.logical_or, [n == s for s in first_blocks])
    last = functools.reduce(jnp.logical_or, [n == s for s in last_blocks])
    fpen = jnp.where(first, NEG, 0.0).astype(F32)
    lpen = jnp.where(last, NEG, 0.0).astype(F32)
    col = lax.broadcasted_iota(jnp.int32, (1, 3 * RADIUS), 1)
    pen_first = jnp.where(col < RADIUS, fpen, 0.0)
    pen_last = jnp.where(col >= 2 * RADIUS, lpen, 0.0)
    pen_perm_first = jnp.where(lax.rem(col, 48) < 16, fpen, 0.0)
    pen_perm_last = jnp.where(lax.rem(col, 48) >= 32, lpen, 0.0)
    ones = jnp.ones((3 * RADIUS, HEAD_DIM), BF16)

    def keys_d1(refs, b):
        ref_p, ref_c, ref_n = refs
        if b == 0:
            return jnp.concatenate([ref_p[...], ref_c[0:2 * RADIUS]], axis=0)
        if b == TILES - 1:
            return jnp.concatenate([ref_c[DIL_BLOCK - 2 * RADIUS:DIL_BLOCK], ref_n[...]], axis=0)
        return ref_c[RADIUS * (b - 1):RADIUS * (b + 2)]

    def keys_d4(refs, r4, b):
        parts = []
        for a in range(4):
            parts += _view_rows(*refs, 16 * b - 16, 16 * b + 32, _lanes(4 * a + r4))
        return jnp.concatenate(parts, axis=0)

    def keys_d16(refs, r):
        return jnp.concatenate([ref[:, _lanes(r)] for ref in refs], axis=0)

    def head_passes(hh):
        q1_ref, q16_ref = q1_blk.at[hh, 0], q16_blk.at[hh, 0]
        k1 = tuple(blk.at[hh, 0] for blk in (kv1p_blk, kv1c_blk, kv1n_blk))
        v1 = tuple(blk.at[hh, 1] for blk in (kv1p_blk, kv1c_blk, kv1n_blk))
        k16 = tuple(blk.at[hh, 0] for blk in (kv16p_blk, kv16c_blk, kv16n_blk))
        v16 = tuple(blk.at[hh, 1] for blk in (kv16p_blk, kv16c_blk, kv16n_blk))
        s_s, e_s, pos_s = s_all.at[hh], e_all.at[hh], pos_all.at[hh]
        num_s, m_s, l_s = num_all.at[hh], m_all_s.at[hh], l_all.at[hh]

        def tile_operands(p, t):
            if p == 0:
                pen = pen_first if t == 0 else pen_last if t == TILES - 1 else None
                return q1_ref[RADIUS * t:RADIUS * (t + 1)], keys_d1(k1, t), keys_d1(v1, t), pen
            if p == 1:
                r4, b = divmod(t, 4)
                q = jnp.concatenate([q16_ref[16 * b:16 * b + 16, _lanes(4 * a + r4)] for a in range(4)],
                                    axis=0)
                pen = pen_perm_first if b == 0 else pen_perm_last if b == 3 else None
                return q, keys_d4(k16, r4, b), keys_d4(v16, r4, b), pen
            return q16_ref[:, _lanes(t)], keys_d16(k16, t), keys_d16(v16, t), pen_first + pen_last

        def store_rows(p, t, tile, pos_idx, view_ref):
            if p == 0:
                pos_s[pos_idx, RADIUS * t:RADIUS * (t + 1), :] = tile
            elif p == 1:
                r4, b = divmod(t, 4)
                for a in range(4):
                    view_ref[0, 16 * b:16 * b + 16, _lanes(4 * a + r4)] = tile[16 * a:16 * a + 16]
            else:
                view_ref[1, :, _lanes(t)] = tile

        def logits(p, t):
            q, k, _, pen = tile_operands(p, t)
            s = lax.dot_general(q, k, (((1,), (1,)), ((), ())), preferred_element_type=F32)
            s = s + bias_ref[hh, p]
            s_s[p, t] = s if pen is None else s + pen

        def softmax(p, t):
            s = s_s[p, t]
            m = jnp.max(s, axis=-1, keepdims=True)
            e_s[p, t] = jnp.exp2((s - m).astype(BF16))
            store_rows(p, t, jnp.broadcast_to(m, (RADIUS, HEAD_DIM)), 1, m_s)

        def values(p, t):
            _, _, v, _ = tile_operands(p, t)
            nv = jnp.dot(e_s[p, t], jnp.concatenate([v, ones], axis=1), preferred_element_type=F32)
            store_rows(p, t, nv[:, :HEAD_DIM], 0, num_s)
            store_rows(p, t, nv[:, HEAD_DIM:], 2, l_s)

        def merge():
            for r in range(DIL_VIEW):
                strided = pl.ds(r, VIEW_ROWS, stride=DIL_VIEW)
                nums = (pos_s[0, strided, :], num_s[0, :, _lanes(r)], num_s[1, :, _lanes(r)])
                ms = (pos_s[1, strided, :], m_s[0, :, _lanes(r)], m_s[1, :, _lanes(r)])
                ls = (pos_s[2, strided, :], l_s[0, :, _lanes(r)], l_s[1, :, _lanes(r)])
                m_max = jnp.maximum(jnp.maximum(ms[0], ms[1]), ms[2])
                ws = [jnp.exp2(m - m_max) for m in ms]
                num = ws[0] * nums[0] + ws[1] * nums[1] + ws[2] * nums[2]
                den = ws[0] * ls[0] + ws[1] * ls[1] + ws[2] * ls[2]
                o_ref[hh, :, _lanes(r)] = (num / den).astype(o_ref.dtype)

        def all_tiles(stage, p):
            return lambda: [stage(p, t) for t in range(TILES)]

        return [all_tiles(logits, 2), all_tiles(logits, 1), all_tiles(softmax, 2), all_tiles(values, 2),
                all_tiles(logits, 0), all_tiles(softmax, 1), all_tiles(values, 1), all_tiles(softmax, 0),
                all_tiles(values, 0), merge]

    first_head, second_head = head_passes(0), head_passes(1)
    for run in first_head[:7] + second_head[:2] + first_head[7:9] + second_head[2:5] + first_head[9:] + second_head[5:]:
        run()


def _dilated_attention(qkv, qkv16, bias, seq_lens):
    nt = qkv.shape[2]
    nblk = nt // DIL_BLOCK
    bounds = np.cumsum((0,) + seq_lens) // DIL_BLOCK
    first_blocks = tuple(int(b) for b in bounds[:-1])
    last_blocks = tuple(int(b) - 1 for b in bounds[1:])
    per = DIL_BLOCK // RADIUS
    prev1 = lambda n: jnp.maximum(per * n - 1, 0)
    next1 = lambda n: jnp.minimum(per * (n + 1), nt // RADIUS - 1)
    prev16 = lambda n: jnp.maximum(n - 1, 0)
    next16 = lambda n: jnp.minimum(n + 1, nblk - 1)
    kv_blk = KV_BLOCK
    hps = DIL_HEADS_PER_STEP
    blk16 = (hps, 1, VIEW_ROWS, DIL_VIEW * HEAD_DIM)
    kv16 = (hps, 2, VIEW_ROWS, DIL_VIEW * HEAD_DIM)
    npat = len(DILATIONS)
    kernel = functools.partial(_dil_kernel, first_blocks=first_blocks, last_blocks=last_blocks)
    out = pl.pallas_call(
        kernel,
        grid=(N_HEADS_A // hps, nblk),
        in_specs=[pl.BlockSpec((hps, 1, DIL_BLOCK, HEAD_DIM), lambda h, n: (h, SLOT_Q, n, 0)),
                  pl.BlockSpec((hps, 2, RADIUS, HEAD_DIM), lambda h, n: (h, kv_blk, prev1(n), 0)),
                  pl.BlockSpec((hps, 2, DIL_BLOCK, HEAD_DIM), lambda h, n: (h, kv_blk, n, 0)),
                  pl.BlockSpec((hps, 2, RADIUS, HEAD_DIM), lambda h, n: (h, kv_blk, next1(n), 0)),
                  pl.BlockSpec(blk16, lambda h, n: (h, SLOT_Q, n, 0)),
                  pl.BlockSpec(kv16, lambda h, n: (h, kv_blk, prev16(n), 0)),
                  pl.BlockSpec(kv16, lambda h, n: (h, kv_blk, n, 0)),
                  pl.BlockSpec(kv16, lambda h, n: (h, kv_blk, next16(n), 0)),
                  pl.BlockSpec((hps, npat, RADIUS, 3 * RADIUS), lambda h, n: (h, 0, 0, 0))],
        out_specs=pl.BlockSpec((hps, VIEW_ROWS, DIL_VIEW * HEAD_DIM), lambda h, n: (h, n, 0)),
        out_shape=jax.ShapeDtypeStruct((N_HEADS_A, nt // DIL_VIEW, DIL_VIEW * HEAD_DIM), BF16),
        scratch_shapes=[pltpu.VMEM((hps, npat, TILES, RADIUS, 3 * RADIUS), F32),
                        pltpu.VMEM((hps, npat, TILES, RADIUS, 3 * RADIUS), BF16),
                        pltpu.VMEM((hps, 3, DIL_BLOCK, HEAD_DIM), F32),
                        pltpu.VMEM((hps, 2, VIEW_ROWS, DIL_VIEW * HEAD_DIM), F32),
                        pltpu.VMEM((hps, 2, VIEW_ROWS, DIL_VIEW * HEAD_DIM), F32),
                        pltpu.VMEM((hps, 2, VIEW_ROWS, DIL_VIEW * HEAD_DIM), F32)],
        compiler_params=_cparams(("parallel", "parallel"), 48),
        name="dilated_attention",
    )(qkv, qkv, qkv, qkv, qkv16, qkv16, qkv16, qkv16, bias)
    return out.reshape(N_HEADS_A, nt, HEAD_DIM)


def _na_bias(rpb):
    c = np.arange(GRID_W)
    cstart = np.clip(c - NA_COLS // 2, 0, GRID_W - NA_COLS)
    cmask = (c[None, :] >= cstart[:, None]) & (c[None, :] < cstart[:, None] + NA_COLS)
    coff = np.clip(c[None, :] - c[:, None], -(NA_COLS - 1), NA_COLS - 1) + NA_COLS - 1
    cols = jnp.full(rpb.shape[:2] + (GRID_W, GRID_W), NEG, F32)
    for o in range(2 * NA_COLS - 1):
        value = rpb[:, :, o, None, None].astype(F32) * LOG2E
        cols = jnp.where((cmask & (coff == o))[None, None], value, cols)
    tiles = [cols[:, NA_ROWS - 1 - dd:2 * NA_ROWS - 1 - dd] for dd in range(NA_ROWS)]
    b = jnp.stack(tiles, axis=1).transpose(0, 1, 3, 2, 4)
    return b.reshape(rpb.shape[0], NA_ROWS, GRID_W, NA_ROWS * GRID_W)


def _na_kernel(q_blk, kvp_blk, kvc_blk, kvn_blk, bias_ref, o_ref, kw, vw,
               s_s, e_s, *, block_starts, seq_rows):
    n = pl.program_id(1)
    nstart = jnp.int32(block_starts[0])
    rows = jnp.int32(seq_rows[0])
    for s, r in zip(block_starts[1:], seq_rows[1:]):
        nstart = jnp.where(n >= s, s, nstart)
        rows = jnp.where(n >= s, r, rows)

    nkeys = NA_ROWS * GRID_W
    shifts, starts = [], []
    for i in range(NA_BLOCK_ROWS):
        r = (n - nstart) * NA_BLOCK_ROWS + i
        dd = r - jnp.clip(r - NA_ROWS // 2, 0, rows - NA_ROWS)
        shifts.append(dd)
        starts.append(pl.multiple_of((NA_HALO_ROWS + i - dd) * GRID_W, GRID_W))
    ones = jnp.ones((nkeys, HEAD_DIM), BF16)

    def head_passes(hh):
        q_ref = q_blk.at[hh, 0]

        def logits_pass():
            for which, win in enumerate((kw, vw)):
                win[hh, 0:NA_HALO, :] = kvp_blk[hh, which]
                win[hh, NA_HALO:NA_HALO + NA_BLOCK, :] = kvc_blk[hh, which]
                win[hh, NA_HALO + NA_BLOCK:, :] = kvn_blk[hh, which]
            for i in range(NA_BLOCK_ROWS):
                k = kw[hh, pl.ds(starts[i], nkeys), :]
                q = q_ref[i * GRID_W:(i + 1) * GRID_W, :]
                s = lax.dot_general(q, k, (((1,), (1,)), ((), ())), preferred_element_type=F32)
                s_s[hh, i] = s + bias_ref[hh, shifts[i]]

        def softmax_pass():
            for i in range(NA_BLOCK_ROWS):
                s = s_s[hh, i]
                e_s[hh, i] = jnp.exp2((s - jnp.max(s, axis=-1, keepdims=True)).astype(BF16))

        def values_pass():
            for i in range(NA_BLOCK_ROWS):
                v = vw[hh, pl.ds(starts[i], nkeys), :]
                nv = jnp.dot(e_s[hh, i], jnp.concatenate([v, ones], axis=1), preferred_element_type=F32)
                out = (nv[:, :HEAD_DIM] / nv[:, HEAD_DIM:]).astype(o_ref.dtype)
                o_ref[i * GRID_W:(i + 1) * GRID_W, _lanes(hh)] = out

        return [logits_pass, softmax_pass, values_pass]

    first_head, second_head = head_passes(0), head_passes(1)
    for run in (first_head[0], second_head[0], first_head[1], first_head[2], second_head[1], second_head[2]):
        run()


def _neighborhood_attention(qkv, bias, seq_lens):
    nt = qkv.shape[2]
    block_starts = tuple(int(s) // NA_BLOCK for s in np.cumsum((0,) + seq_lens[:-1]))
    seq_rows = tuple(int(s) // GRID_W for s in seq_lens)
    assert all(r >= NA_ROWS and r % NA_BLOCK_ROWS == 0 for r in seq_rows)
    per = NA_BLOCK // NA_HALO
    prev = lambda n: jnp.maximum(per * n - 1, 0)
    nxt = lambda n: jnp.minimum(per * (n + 1), nt // NA_HALO - 1)
    kv_blk = KV_BLOCK
    kernel = functools.partial(_na_kernel, block_starts=block_starts, seq_rows=seq_rows)
    nkeys = NA_ROWS * GRID_W
    hps = NA_HEADS_PER_STEP
    return pl.pallas_call(
        kernel,
        grid=(N_HEADS_B // hps, nt // NA_BLOCK),
        in_specs=[pl.BlockSpec((hps, 1, NA_BLOCK, HEAD_DIM), lambda h, n: (h, SLOT_Q, n, 0)),
                  pl.BlockSpec((hps, 2, NA_HALO, HEAD_DIM), lambda h, n: (h, kv_blk, prev(n), 0)),
                  pl.BlockSpec((hps, 2, NA_BLOCK, HEAD_DIM), lambda h, n: (h, kv_blk, n, 0)),
                  pl.BlockSpec((hps, 2, NA_HALO, HEAD_DIM), lambda h, n: (h, kv_blk, nxt(n), 0)),
                  pl.BlockSpec((hps, NA_ROWS, GRID_W, nkeys), lambda h, n: (h, 0, 0, 0))],
        out_specs=pl.BlockSpec((NA_BLOCK, hps * HEAD_DIM), lambda h, n: (n, h)),
        out_shape=jax.ShapeDtypeStruct((nt, D_B), BF16),
        scratch_shapes=[pltpu.VMEM((hps, NA_BLOCK + 2 * NA_HALO, HEAD_DIM), BF16),
                        pltpu.VMEM((hps, NA_BLOCK + 2 * NA_HALO, HEAD_DIM), BF16),
                        pltpu.VMEM((hps, NA_BLOCK_ROWS, GRID_W, nkeys), F32),
                        pltpu.VMEM((hps, NA_BLOCK_ROWS, GRID_W, nkeys), BF16)],
        compiler_params=_cparams(("parallel", "parallel"), 40),
        name="neighborhood_attention",
    )(qkv, qkv, qkv, qkv, bias)


def _plain_norm(y, g):
    y = y.astype(F32)
    return (y * lax.rsqrt(jnp.mean(y * y, axis=-1, keepdims=True) + EPS) * g).astype(BF16)


OUT_ROW_CHUNK = 256


def _out_kernel(ya_ref, yb_ref, ga_ref, gb_ref, w_ref, x_ref, mod_ref, g2_ref, xo_ref, h_ref):
    for r0 in range(0, x_ref.shape[0], OUT_ROW_CHUNK):
        rows = slice(r0, r0 + OUT_ROW_CHUNK)
        ya = jnp.concatenate([ya_ref[h, rows, :] for h in range(N_HEADS_A)], axis=1)
        y = jnp.dot(_plain_norm(ya, ga_ref[...]), w_ref[:D_A, :], preferred_element_type=F32)
        y = y + jnp.dot(_plain_norm(yb_ref[rows, :], gb_ref[...]), w_ref[D_A:, :],
                        preferred_element_type=F32)
        x = x_ref[rows, :] + mod_ref[0, 2:3, :] * y
        xo_ref[rows, :] = x
        h = _modulated_norm(x, g2_ref[...], mod_ref[0, 3:4, :], mod_ref[0, 4:5, :])
        h_ref[rows, :] = h.astype(BF16)


def _out_proj(ya, yb, ga, gb, w, layer, x, mod_l, g2, seq_lens):
    nt, d = x.shape
    tm = 512
    starts = tuple(int(s) // tm for s in np.cumsum((0,) + seq_lens[:-1]))
    const = lambda i: (0, 0)
    return pl.pallas_call(
        _out_kernel,
        grid=(nt // tm,),
        in_specs=[pl.BlockSpec((N_HEADS_A, tm, HEAD_DIM), lambda i: (0, i, 0)),
                  pl.BlockSpec((tm, D_B), lambda i: (i, 0)),
                  pl.BlockSpec((1, D_A), const),
                  pl.BlockSpec((1, D_B), const),
                  pl.BlockSpec((None, D_A + D_B, d), lambda i: (layer, 0, 0), pipeline_mode=pl.Buffered(1)),
                  pl.BlockSpec((tm, d), lambda i: (i, 0)),
                  pl.BlockSpec((1, 6, d), lambda i: (_seq_index(i, starts), 0, 0)),
                  pl.BlockSpec((1, d), const)],
        out_specs=[pl.BlockSpec((tm, d), lambda i: (i, 0)),
                   pl.BlockSpec((tm, d), lambda i: (i, 0))],
        out_shape=[jax.ShapeDtypeStruct((nt, d), F32), jax.ShapeDtypeStruct((nt, d), BF16)],
        compiler_params=_cparams(("parallel",), 48),
        name="out_proj",
    )(ya, yb, ga, gb, w, x, mod_l, g2)


def _ffn_kernel(h_ref, wg_ref, wu_ref, wd_ref, x_ref, mod_ref, ng_ref, *refs, split_block):
    acc_ref = refs[-1]
    f = pl.program_id(1)

    @pl.when(f == 0)
    def _():
        acc_ref[...] = jnp.zeros_like(acc_ref)

    h = h_ref[...]
    a = jnp.dot(h, wg_ref[...], preferred_element_type=F32)
    b = jnp.dot(h, wu_ref[...], preferred_element_type=F32)
    act = (a / (1.0 + jnp.exp(-a)) * b).astype(BF16)
    acc_ref[...] += jnp.dot(act, wd_ref[...], preferred_element_type=F32)

    @pl.when(f == pl.num_programs(1) - 1)
    def _():
        x = x_ref[...] + mod_ref[0, 5:6, :] * acc_ref[...]
        if split_block is not None:
            y0_ref, y1_ref = refs[:-1]
            y = x * lax.rsqrt(jnp.mean(x * x, axis=-1, keepdims=True) + EPS) * ng_ref[...]
            i = pl.program_id(0)

            @pl.when(i < split_block)
            def _():
                y0_ref[...] = y

            @pl.when(i >= split_block)
            def _():
                y1_ref[...] = y
        else:
            nmod_ref, xo_ref, hn_ref = refs[:-1]
            xo_ref[...] = x
            hn = _modulated_norm(x, ng_ref[...], nmod_ref[0, 0:1, :], nmod_ref[0, 1:2, :])
            hn_ref[...] = hn.astype(BF16)


def _ffn(h, wg, wu, wd, layer, x, mod_l, next_g, next_mod, seq_lens, out_split=None):
    nt, d = x.shape
    dff = wg.shape[2]
    tm, tf = 512, 512
    final = next_mod is None
    split_block = out_split // tm if final else None
    starts = tuple(int(s) // tm for s in np.cumsum((0,) + seq_lens[:-1]))
    row_blk = pl.BlockSpec((tm, d), lambda i, f: (i, 0))
    mod_blk = pl.BlockSpec((1, 6, d), lambda i, f: (_seq_index(i, starts), 0, 0))
    in_specs = [row_blk,
                pl.BlockSpec((None, d, tf), lambda i, f: (layer, 0, f)),
                pl.BlockSpec((None, d, tf), lambda i, f: (layer, 0, f)),
                pl.BlockSpec((None, tf, d), lambda i, f: (layer, f, 0)),
                row_blk,
                mod_blk,
                pl.BlockSpec((1, d), lambda i, f: (0, 0))]
    operands = [h, wg, wu, wd, x, mod_l, next_g]
    if final:
        out_specs = [pl.BlockSpec((tm, d), lambda i, f: (jnp.minimum(i, split_block - 1), 0)),
                     pl.BlockSpec((tm, d), lambda i, f: (jnp.maximum(i - split_block, 0), 0))]
        out_shape = [jax.ShapeDtypeStruct((out_split, d), F32),
                     jax.ShapeDtypeStruct((nt - out_split, d), F32)]
    else:
        in_specs.append(mod_blk)
        operands.append(next_mod)
        out_specs = [row_blk, row_blk]
        out_shape = [jax.ShapeDtypeStruct((nt, d), F32), jax.ShapeDtypeStruct((nt, d), BF16)]
    return pl.pallas_call(
        functools.partial(_ffn_kernel, split_block=split_block),
        grid=(nt // tm, dff // tf),
        in_specs=in_specs,
        out_specs=out_specs,
        out_shape=out_shape,
        scratch_shapes=[pltpu.VMEM((tm, d), F32)],
        compiler_params=_cparams(("arbitrary" if final else "parallel", "arbitrary"), 56),
        name="ffn_final" if final else "ffn",
    )(*operands)


def _trunk(x, c, seq_lens, out_split, t5_table, norm1_g, norm2_g, w_ada, b_ada, w_in, out_norm_a,
           out_norm_b, w_out, na_rpb, w_gate, w_up, w_down, final_g):
    depth = w_in.shape[0]
    nseq = len(seq_lens)
    d = x.shape[1]
    assert all(s % DIL_BLOCK == 0 for s in seq_lens)

    c_pad = jnp.zeros((8, d), F32).at[:nseq].set(c)
    mod = _ada_modulation(c_pad, w_ada, b_ada)[:, :nseq].reshape(depth, nseq, 6, d)

    q_scale = QK_SCALE * LOG2E
    col_scale = jnp.ones((6, D_A), F32).at[0].set(q_scale).at[3].set(q_scale).reshape(1, 6 * D_A)
    dil_bias = _dilated_bias(t5_table)
    w_in, w_out, w_gate, w_up, w_down = (w.astype(BF16) for w in (w_in, w_out, w_gate, w_up, w_down))
    h = _prenorm(x, mod[0], norm1_g[0][None], seq_lens)
    for l in range(depth):
        final = l == depth - 1
        qkv_a, qkv_a16 = _qkv_proj(h, col_scale, w_in, l, 0, True)
        qkv_b, = _qkv_proj(h, col_scale, w_in, l, 1, False)
        ya = _dilated_attention(qkv_a, qkv_a16, dil_bias, seq_lens)
        yb = _neighborhood_attention(qkv_b, _na_bias(na_rpb[l]), seq_lens)
        x, h2 = _out_proj(ya, yb, out_norm_a[l][None], out_norm_b[l][None], w_out, l,
                          x, mod[l], norm2_g[l][None], seq_lens)
        next_g = final_g[None] if final else norm1_g[l + 1][None]
        next_mod = None if final else mod[l + 1]
        out = _ffn(h2, w_gate, w_up, w_down, l, x, mod[l], next_g, next_mod, seq_lens, out_split)
        if not final:
            x, h = out
    return out


def kernel(x_prompt, x_sample, c_prompt, c_sample, t5_table, norm1_g, norm2_g, w_ada, b_ada, w_in,
           out_norm_a, out_norm_b, w_out, na_rpb, w_gate, w_up, w_down, final_g):
    bp, tp, d = x_prompt.shape
    bs, ts, _ = x_sample.shape
    seq_lens = (tp,) * bp + (ts,) * bs
    x = jnp.concatenate([x_prompt.reshape(bp * tp, d), x_sample.reshape(bs * ts, d)], axis=0)
    c = jnp.concatenate([c_prompt, c_sample], axis=0)
    y_prompt, y_sample = _trunk(x, c, seq_lens, bp * tp, t5_table, norm1_g, norm2_g, w_ada, b_ada, w_in,
                                out_norm_a, out_norm_b, w_out, na_rpb, w_gate, w_up, w_down, final_g)
    return (y_prompt.reshape(bp, tp, d), y_sample.reshape(bs, ts, d))
```

```python
import functools
import math

import numpy as np
import jax
import jax.numpy as jnp
from jax import lax
from jax.experimental import pallas as pl
from jax.experimental.pallas import tpu as pltpu

D_MODEL = 2048
HEAD_DIM = 128
N_HEADS_A = 8
N_HEADS_B = 8
D_A = N_HEADS_A * HEAD_DIM
D_B = N_HEADS_B * HEAD_DIM
QK_SCALE = HEAD_DIM ** -0.5
DILATIONS = (1, 4, 16)
RADIUS = 64
N_BUCKETS = 32
MAX_DISTANCE = 1024
GRID_W = 64
NA_ROWS = 8
NA_COLS = 16
EPS = 1e-6
NEG = -1e30

DIL_BLOCK = 1024
DIL_VIEW = 16
VIEW_ROWS = DIL_BLOCK // DIL_VIEW
TILES = DIL_BLOCK // RADIUS
DIL_HEADS_PER_STEP = 2
NA_BLOCK_ROWS = 16
NA_HALO_ROWS = 8
NA_BLOCK = NA_BLOCK_ROWS * GRID_W
NA_HEADS_PER_STEP = 2
NA_HALO = NA_HALO_ROWS * GRID_W

F32 = jnp.float32
BF16 = jnp.bfloat16
MIB = 1024 * 1024


def _cparams(sem, vmem_mib):
    return pltpu.CompilerParams(dimension_semantics=sem, vmem_limit_bytes=vmem_mib * MIB)


def _seq_index(i, starts):
    idx = 0
    for s in starts[1:]:
        idx = idx + (i >= s).astype(jnp.int32)
    return idx


def _lanes(r):
    return slice(r * HEAD_DIM, (r + 1) * HEAD_DIM)


def _ada_kernel(c_ref, w_ref, b_ref, o_ref):
    c = c_ref[...]
    s = c / (1.0 + jnp.exp(-c))
    o_ref[0] = jnp.dot(s, w_ref[0], precision=lax.Precision.HIGHEST,
                       preferred_element_type=F32) + b_ref[0]


def _ada_modulation(c_pad, w_ada, b_ada):
    depth, d, n = w_ada.shape
    tn = 1024
    return pl.pallas_call(
        _ada_kernel,
        grid=(depth, n // tn),
        in_specs=[pl.BlockSpec((8, d), lambda l, j: (0, 0)),
                  pl.BlockSpec((1, d, tn), lambda l, j: (l, 0, j)),
                  pl.BlockSpec((1, 1, tn), lambda l, j: (l, 0, j))],
        out_specs=pl.BlockSpec((1, 8, tn), lambda l, j: (l, 0, j)),
        out_shape=jax.ShapeDtypeStruct((depth, 8, n), F32),
        compiler_params=_cparams(("parallel", "parallel"), 40),
        name="ada_modulation",
    )(c_pad, w_ada, b_ada.reshape(depth, 1, n))


def _modulated_norm(x, g, shift, scale):
    y = x * lax.rsqrt(jnp.mean(x * x, axis=-1, keepdims=True) + EPS) * g
    return y * (1.0 + scale) + shift


def _prenorm_kernel(x_ref, mod_ref, g_ref, h_ref):
    h = _modulated_norm(x_ref[...], g_ref[...], mod_ref[0, 0:1, :], mod_ref[0, 1:2, :])
    h_ref[...] = h.astype(BF16)


def _prenorm(x, mod_l, g, seq_lens):
    nt, d = x.shape
    tm = 512
    starts = tuple(int(s) // tm for s in np.cumsum((0,) + seq_lens[:-1]))
    return pl.pallas_call(
        _prenorm_kernel,
        grid=(nt // tm,),
        in_specs=[pl.BlockSpec((tm, d), lambda i: (i, 0)),
                  pl.BlockSpec((1, 6, d), lambda i: (_seq_index(i, starts), 0, 0)),
                  pl.BlockSpec((1, d), lambda i: (0, 0))],
        out_specs=pl.BlockSpec((tm, d), lambda i: (i, 0)),
        out_shape=jax.ShapeDtypeStruct((nt, d), BF16),
        compiler_params=_cparams(("parallel",), 32),
        name="prenorm",
    )(x, mod_l, g)


SLOT_K, SLOT_V, SLOT_Q = range(3)
KV_BLOCK = SLOT_K // 2


def _qkv_kernel(h_ref, cs_ref, w_ref, o_ref, *view, tm):
    r = jnp.dot(h_ref[...], w_ref[...], preferred_element_type=F32) * cs_ref[...]
    for c in range(o_ref.shape[0]):
        o_ref[c, 0] = r[:, _lanes(c)].astype(BF16)
    if view:
        ov_ref, r_s = view
        for c in range(o_ref.shape[0]):
            r_s[c] = r[:, _lanes(c)]
            for g in range(DIL_VIEW):
                rows = r_s[c, pl.ds(g, tm // DIL_VIEW, stride=DIL_VIEW), :]
                ov_ref[c, 0, :, _lanes(g)] = rows.astype(BF16)


def _qkv_proj(h, col_scale, w, layer, group, with_view):
    nt, d = h.shape
    tm = 1024
    width = N_HEADS_A * HEAD_DIM
    assert w.shape[2] == 6 * width and N_HEADS_A == N_HEADS_B
    slot = lambda j: lax.rem(j + 2, 3)
    out_specs = [pl.BlockSpec((N_HEADS_A, 1, tm, HEAD_DIM), lambda i, j: (0, slot(j), i, 0))]
    out_shape = [jax.ShapeDtypeStruct((N_HEADS_A, 3, nt, HEAD_DIM), BF16)]
    scratch = []
    if with_view:
        out_specs.append(pl.BlockSpec((N_HEADS_A, 1, tm // DIL_VIEW, DIL_VIEW * HEAD_DIM),
                                      lambda i, j: (0, slot(j), i, 0)))
        out_shape.append(jax.ShapeDtypeStruct((N_HEADS_A, 3, nt // DIL_VIEW, DIL_VIEW * HEAD_DIM), BF16))
        scratch.append(pltpu.VMEM((N_HEADS_A, tm, HEAD_DIM), F32))
    return pl.pallas_call(
        functools.partial(_qkv_kernel, tm=tm),
        grid=(nt // tm, 3),
        in_specs=[pl.BlockSpec((tm, d), lambda i, j: (i, 0)),
                  pl.BlockSpec((1, width), lambda i, j: (0, 3 * group + j)),
                  pl.BlockSpec((None, d, width), lambda i, j: (layer, 0, 3 * group + j))],
        out_specs=out_specs,
        out_shape=out_shape,
        scratch_shapes=scratch,
        compiler_params=_cparams(("parallel", "parallel"), 48),
        name="qkv_proj_view" if with_view else "qkv_proj",
    )(h, col_scale, w)


def _t5_bucket(rel):
    nb = N_BUCKETS // 2
    exact = nb // 2
    n = np.abs(rel)
    sign = np.where(rel > 0, nb, 0)
    large = exact + (np.log(np.maximum(n, 1) / exact) / math.log(MAX_DISTANCE / exact) * (nb - exact)).astype(np.int64)
    large = np.minimum(large, nb - 1)
    return (sign + np.where(n < exact, n, large)).astype(np.int32)


def _tile_offsets():
    i = np.arange(RADIUS)[:, None]
    j = np.arange(3 * RADIUS)[None, :]
    band = j - RADIUS - i
    qa, qi = i // 16, i % 16
    ka, kj = j // 48, j % 48
    perm = 4 * (kj - 16 - qi) + (ka - qa)
    return (band, perm, band)


def _dilated_bias(t5_table):
    table = t5_table.astype(F32).T
    tiles = []
    for dil, rel in zip(DILATIONS, _tile_offsets()):
        inside = np.abs(rel) <= RADIUS
        bucket = _t5_bucket(rel * dil)
        b = jnp.full((table.shape[0],) + rel.shape, NEG, F32)
        for o in range(N_BUCKETS):
            b = jnp.where((inside & (bucket == o))[None], table[:, o, None, None], b)
        tiles.append(b)
    return jnp.stack(tiles, axis=1)


def _view_rows(ref_p, ref_c, ref_n, lo, hi, lanes):
    parts = []
    if lo < 0:
        parts.append(ref_p[VIEW_ROWS + lo:VIEW_ROWS, lanes])
    parts.append(ref_c[max(lo, 0):min(hi, VIEW_ROWS), lanes])
    if hi > VIEW_ROWS:
        parts.append(ref_n[0:hi - VIEW_ROWS, lanes])
    return parts


def _dil_kernel(q1_blk, kv1p_blk, kv1c_blk, kv1n_blk, q16_blk, kv16p_blk, kv16c_blk, kv16n_blk,
                bias_ref, o_ref, s_all, e_all, pos_all, num_all, m_all_s, l_all,
                *, first_blocks, last_blocks):
    n = pl.program_id(1)
    first = functools.reduce(jnp.logical_or, [n == s for s in first_blocks])
    last = functools.reduce(jnp.logical_or, [n == s for s in last_blocks])
    fpen = jnp.where(first, NEG, 0.0).astype(F32)
    lpen = jnp.where(last, NEG, 0.0).astype(F32)
    col = lax.broadcasted_iota(jnp.int32, (1, 3 * RADIUS), 1)
    pen_first = jnp.where(col < RADIUS, fpen, 0.0)
    pen_last = jnp.where(col >= 2 * RADIUS, lpen, 0.0)
    pen_perm_first = jnp.where(lax.rem(col, 48) < 16, fpen, 0.0)
    pen_perm_last = jnp.where(lax.rem(col, 48) >= 32, lpen, 0.0)
    ones = jnp.ones((3 * RADIUS, HEAD_DIM), BF16)

    def keys_d1(refs, b):
        ref_p, ref_c, ref_n = refs
        if b == 0:
            return jnp.concatenate([ref_p[...], ref_c[0:2 * RADIUS]], axis=0)
        if b == TILES - 1:
            return jnp.concatenate([ref_c[DIL_BLOCK - 2 * RADIUS:DIL_BLOCK], ref_n[...]], axis=0)
        return ref_c[RADIUS * (b - 1):RADIUS * (b + 2)]

    def keys_d4(refs, r4, b):
        parts = []
        for a in range(4):
            parts += _view_rows(*refs, 16 * b - 16, 16 * b + 32, _lanes(4 * a + r4))
        return jnp.concatenate(parts, axis=0)

    def keys_d16(refs, r):
        return jnp.concatenate([ref[:, _lanes(r)] for ref in refs], axis=0)

    def head_passes(hh):
        q1_ref, q16_ref = q1_blk.at[hh, 0], q16_blk.at[hh, 0]
        k1 = tuple(blk.at[hh, 0] for blk in (kv1p_blk, kv1c_blk, kv1n_blk))
        v1 = tuple(blk.at[hh, 1] for blk in (kv1p_blk, kv1c_blk, kv1n_blk))
        k16 = tuple(blk.at[hh, 0] for blk in (kv16p_blk, kv16c_blk, kv16n_blk))
        v16 = tuple(blk.at[hh, 1] for blk in (kv16p_blk, kv16c_blk, kv16n_blk))
        s_s, e_s, pos_s = s_all.at[hh], e_all.at[hh], pos_all.at[hh]
        num_s, m_s, l_s = num_all.at[hh], m_all_s.at[hh], l_all.at[hh]

        def tile_operands(p, t):
            if p == 0:
                pen = pen_first if t == 0 else pen_last if t == TILES - 1 else None
                return q1_ref[RADIUS * t:RADIUS * (t + 1)], keys_d1(k1, t), keys_d1(v1, t), pen
            if p == 1:
                r4, b = divmod(t, 4)
                q = jnp.concatenate([q16_ref[16 * b:16 * b + 16, _lanes(4 * a + r4)] for a in range(4)],
                                    axis=0)
                pen = pen_perm_first if b == 0 else pen_perm_last if b == 3 else None
                return q, keys_d4(k16, r4, b), keys_d4(v16, r4, b), pen
            return q16_ref[:, _lanes(t)], keys_d16(k16, t), keys_d16(v16, t), pen_first + pen_last

        def store_rows(p, t, tile, pos_idx, view_ref):
            if p == 0:
                pos_s[pos_idx, RADIUS * t:RADIUS * (t + 1), :] = tile
            elif p == 1:
                r4, b = divmod(t, 4)
                for a in range(4):
                    view_ref[0, 16 * b:16 * b + 16, _lanes(4 * a + r4)] = tile[16 * a:16 * a + 16]
            else:
                view_ref[1, :, _lanes(t)] = tile

        def logits(p, t):
            q, k, _, pen = tile_operands(p, t)
            s = lax.dot_general(q, k, (((1,), (1,)), ((), ())), preferred_element_type=F32)
            s = s + bias_ref[hh, p]
            s_s[p, t] = s if pen is None else s + pen

        def softmax(p, t):
            s = s_s[p, t]
            m = jnp.max(s, axis=-1, keepdims=True)
            e_s[p, t] = jnp.exp(s - m).astype(BF16)
            store_rows(p, t, jnp.broadcast_to(m, (RADIUS, HEAD_DIM)), 1, m_s)

        def values(p, t):
            _, _, v, _ = tile_operands(p, t)
            nv = jnp.dot(e_s[p, t], jnp.concatenate([v, ones], axis=1), preferred_element_type=F32)
            store_rows(p, t, nv[:, :HEAD_DIM], 0, num_s)
            store_rows(p, t, nv[:, HEAD_DIM:], 2, l_s)

        def merge():
            for r in range(DIL_VIEW):
                strided = pl.ds(r, VIEW_ROWS, stride=DIL_VIEW)
                nums = (pos_s[0, strided, :], num_s[0, :, _lanes(r)], num_s[1, :, _lanes(r)])
                ms = (pos_s[1, strided, :], m_s[0, :, _lanes(r)], m_s[1, :, _lanes(r)])
                ls = (pos_s[2, strided, :], l_s[0, :, _lanes(r)], l_s[1, :, _lanes(r)])
                m_max = jnp.maximum(jnp.maximum(ms[0], ms[1]), ms[2])
                ws = [jnp.exp(m - m_max) for m in ms]
                num = ws[0] * nums[0] + ws[1] * nums[1] + ws[2] * nums[2]
                den = ws[0] * ls[0] + ws[1] * ls[1] + ws[2] * ls[2]
                o_ref[hh, strided, :] = num / den

        def all_tiles(stage, p):
            return lambda: [stage(p, t) for t in range(TILES)]

        return [all_tiles(logits, 2), all_tiles(logits, 1), all_tiles(softmax, 2), all_tiles(values, 2),
                all_tiles(logits, 0), all_tiles(softmax, 1), all_tiles(values, 1), all_tiles(softmax, 0),
                all_tiles(values, 0), merge]

    first_head, second_head = head_passes(0), head_passes(1)
    for run in first_head[:7] + second_head[:2] + first_head[7:9] + second_head[2:5] + first_head[9:] + second_head[5:]:
        run()


def _dilated_attention(qkv, qkv16, bias, seq_lens):
    nt = qkv.shape[2]
    nblk = nt // DIL_BLOCK
    bounds = np.cumsum((0,) + seq_lens) // DIL_BLOCK
    first_blocks = tuple(int(b) for b in bounds[:-1])
    last_blocks = tuple(int(b) - 1 for b in bounds[1:])
    per = DIL_BLOCK // RADIUS
    prev1 = lambda n: jnp.maximum(per * n - 1, 0)
    next1 = lambda n: jnp.minimum(per * (n + 1), nt // RADIUS - 1)
    prev16 = lambda n: jnp.maximum(n - 1, 0)
    next16 = lambda n: jnp.minimum(n + 1, nblk - 1)
    kv_blk = KV_BLOCK
    hps = DIL_HEADS_PER_STEP
    blk16 = (hps, 1, VIEW_ROWS, DIL_VIEW * HEAD_DIM)
    kv16 = (hps, 2, VIEW_ROWS, DIL_VIEW * HEAD_DIM)
    npat = len(DILATIONS)
    kernel = functools.partial(_dil_kernel, first_blocks=first_blocks, last_blocks=last_blocks)
    out = pl.pallas_call(
        kernel,
        grid=(N_HEADS_A // hps, nblk),
        in_specs=[pl.BlockSpec((hps, 1, DIL_BLOCK, HEAD_DIM), lambda h, n: (h, SLOT_Q, n, 0)),
                  pl.BlockSpec((hps, 2, RADIUS, HEAD_DIM), lambda h, n: (h, kv_blk, prev1(n), 0)),
                  pl.BlockSpec((hps, 2, DIL_BLOCK, HEAD_DIM), lambda h, n: (h, kv_blk, n, 0)),
                  pl.BlockSpec((hps, 2, RADIUS, HEAD_DIM), lambda h, n: (h, kv_blk, next1(n), 0)),
                  pl.BlockSpec(blk16, lambda h, n: (h, SLOT_Q, n, 0)),
                  pl.BlockSpec(kv16, lambda h, n: (h, kv_blk, prev16(n), 0)),
                  pl.BlockSpec(kv16, lambda h, n: (h, kv_blk, n, 0)),
                  pl.BlockSpec(kv16, lambda h, n: (h, kv_blk, next16(n), 0)),
                  pl.BlockSpec((hps, npat, RADIUS, 3 * RADIUS), lambda h, n: (h, 0, 0, 0))],
        out_specs=pl.BlockSpec((hps, DIL_BLOCK, HEAD_DIM), lambda h, n: (h, n, 0)),
        out_shape=jax.ShapeDtypeStruct((N_HEADS_A, nt, HEAD_DIM), F32),
        scratch_shapes=[pltpu.VMEM((hps, npat, TILES, RADIUS, 3 * RADIUS), F32),
                        pltpu.VMEM((hps, npat, TILES, RADIUS, 3 * RADIUS), BF16),
                        pltpu.VMEM((hps, 3, DIL_BLOCK, HEAD_DIM), F32),
                        pltpu.VMEM((hps, 2, VIEW_ROWS, DIL_VIEW * HEAD_DIM), F32),
                        pltpu.VMEM((hps, 2, VIEW_ROWS, DIL_VIEW * HEAD_DIM), F32),
                        pltpu.VMEM((hps, 2, VIEW_ROWS, DIL_VIEW * HEAD_DIM), F32)],
        compiler_params=_cparams(("parallel", "parallel"), 48),
        name="dilated_attention",
    )(qkv, qkv, qkv, qkv, qkv16, qkv16, qkv16, qkv16, bias)
    return out


def _na_bias(rpb):
    c = np.arange(GRID_W)
    cstart = np.clip(c - NA_COLS // 2, 0, GRID_W - NA_COLS)
    cmask = (c[None, :] >= cstart[:, None]) & (c[None, :] < cstart[:, None] + NA_COLS)
    coff = np.clip(c[None, :] - c[:, None], -(NA_COLS - 1), NA_COLS - 1) + NA_COLS - 1
    cols = jnp.full(rpb.shape[:2] + (GRID_W, GRID_W), NEG, F32)
    for o in range(2 * NA_COLS - 1):
        cols = jnp.where((cmask & (coff == o))[None, None], rpb[:, :, o, None, None].astype(F32), cols)
    tiles = [cols[:, NA_ROWS - 1 - dd:2 * NA_ROWS - 1 - dd] for dd in range(NA_ROWS)]
    b = jnp.stack(tiles, axis=1).transpose(0, 1, 3, 2, 4)
    return b.reshape(rpb.shape[0], NA_ROWS, GRID_W, NA_ROWS * GRID_W)


def _na_kernel(q_blk, kvp_blk, kvc_blk, kvn_blk, bias_ref, o_ref, kw, vw,
               s_s, e_s, *, block_starts, seq_rows):
    n = pl.program_id(1)
    nstart = jnp.int32(block_starts[0])
    rows = jnp.int32(seq_rows[0])
    for s, r in zip(block_starts[1:], seq_rows[1:]):
        nstart = jnp.where(n >= s, s, nstart)
        rows = jnp.where(n >= s, r, rows)

    nkeys = NA_ROWS * GRID_W
    shifts, starts = [], []
    for i in range(NA_BLOCK_ROWS):
        r = (n - nstart) * NA_BLOCK_ROWS + i
        dd = r - jnp.clip(r - NA_ROWS // 2, 0, rows - NA_ROWS)
        shifts.append(dd)
        starts.append(pl.multiple_of((NA_HALO_ROWS + i - dd) * GRID_W, GRID_W))
    ones = jnp.ones((nkeys, HEAD_DIM), BF16)

    def head_passes(hh):
        q_ref = q_blk.at[hh, 0]

        def logits_pass():
            for which, win in enumerate((kw, vw)):
                win[hh, 0:NA_HALO, :] = kvp_blk[hh, which]
                win[hh, NA_HALO:NA_HALO + NA_BLOCK, :] = kvc_blk[hh, which]
                win[hh, NA_HALO + NA_BLOCK:, :] = kvn_blk[hh, which]
            for i in range(NA_BLOCK_ROWS):
                k = kw[hh, pl.ds(starts[i], nkeys), :]
                q = q_ref[i * GRID_W:(i + 1) * GRID_W, :]
                s = lax.dot_general(q, k, (((1,), (1,)), ((), ())), preferred_element_type=F32)
                s_s[hh, i] = s + bias_ref[hh, shifts[i]]

        def softmax_pass():
            for i in range(NA_BLOCK_ROWS):
                s = s_s[hh, i]
                e_s[hh, i] = jnp.exp(s - jnp.max(s, axis=-1, keepdims=True)).astype(BF16)

        def values_pass():
            for i in range(NA_BLOCK_ROWS):
                v = vw[hh, pl.ds(starts[i], nkeys), :]
                nv = jnp.dot(e_s[hh, i], jnp.concatenate([v, ones], axis=1), preferred_element_type=F32)
                out = (nv[:, :HEAD_DIM] / nv[:, HEAD_DIM:]).astype(o_ref.dtype)
                o_ref[i * GRID_W:(i + 1) * GRID_W, _lanes(hh)] = out

        return [logits_pass, softmax_pass, values_pass]

    first_head, second_head = head_passes(0), head_passes(1)
    for run in (first_head[0], second_head[0], first_head[1], first_head[2], second_head[1], second_head[2]):
        run()


def _neighborhood_attention(qkv, bias, seq_lens):
    nt = qkv.shape[2]
    block_starts = tuple(int(s) // NA_BLOCK for s in np.cumsum((0,) + seq_lens[:-1]))
    seq_rows = tuple(int(s) // GRID_W for s in seq_lens)
    assert all(r >= NA_ROWS and r % NA_BLOCK_ROWS == 0 for r in seq_rows)
    per = NA_BLOCK // NA_HALO
    prev = lambda n: jnp.maximum(per * n - 1, 0)
    nxt = lambda n: jnp.minimum(per * (n + 1), nt // NA_HALO - 1)
    kv_blk = KV_BLOCK
    kernel = functools.partial(_na_kernel, block_starts=block_starts, seq_rows=seq_rows)
    nkeys = NA_ROWS * GRID_W
    hps = NA_HEADS_PER_STEP
    return pl.pallas_call(
        kernel,
        grid=(N_HEADS_B // hps, nt // NA_BLOCK),
        in_specs=[pl.BlockSpec((hps, 1, NA_BLOCK, HEAD_DIM), lambda h, n: (h, SLOT_Q, n, 0)),
                  pl.BlockSpec((hps, 2, NA_HALO, HEAD_DIM), lambda h, n: (h, kv_blk, prev(n), 0)),
                  pl.BlockSpec((hps, 2, NA_BLOCK, HEAD_DIM), lambda h, n: (h, kv_blk, n, 0)),
                  pl.BlockSpec((hps, 2, NA_HALO, HEAD_DIM), lambda h, n: (h, kv_blk, nxt(n), 0)),
                  pl.BlockSpec((hps, NA_ROWS, GRID_W, nkeys), lambda h, n: (h, 0, 0, 0))],
        out_specs=pl.BlockSpec((NA_BLOCK, hps * HEAD_DIM), lambda h, n: (n, h)),
        out_shape=jax.ShapeDtypeStruct((nt, D_B), BF16),
        scratch_shapes=[pltpu.VMEM((hps, NA_BLOCK + 2 * NA_HALO, HEAD_DIM), BF16),
                        pltpu.VMEM((hps, NA_BLOCK + 2 * NA_HALO, HEAD_DIM), BF16),
                        pltpu.VMEM((hps, NA_BLOCK_ROWS, GRID_W, nkeys), F32),
                        pltpu.VMEM((hps, NA_BLOCK_ROWS, GRID_W, nkeys), BF16)],
        compiler_params=_cparams(("parallel", "parallel"), 40),
        name="neighborhood_attention",
    )(qkv, qkv, qkv, qkv, bias)


def _plain_norm(y, g):
    y = y.astype(F32)
    return (y * lax.rsqrt(jnp.mean(y * y, axis=-1, keepdims=True) + EPS) * g).astype(BF16)


OUT_ROW_CHUNK = 256


def _out_kernel(ya_ref, yb_ref, ga_ref, gb_ref, w_ref, x_ref, mod_ref, g2_ref, xo_ref, h_ref):
    for r0 in range(0, x_ref.shape[0], OUT_ROW_CHUNK):
        rows = slice(r0, r0 + OUT_ROW_CHUNK)
        ya = jnp.concatenate([ya_ref[h, rows, :] for h in range(N_HEADS_A)], axis=1)
        y = jnp.dot(_plain_norm(ya, ga_ref[...]), w_ref[:D_A, :], preferred_element_type=F32)
        y = y + jnp.dot(_plain_norm(yb_ref[rows, :], gb_ref[...]), w_ref[D_A:, :],
                        preferred_element_type=F32)
        x = x_ref[rows, :] + mod_ref[0, 2:3, :] * y
        xo_ref[rows, :] = x
        h = _modulated_norm(x, g2_ref[...], mod_ref[0, 3:4, :], mod_ref[0, 4:5, :])
        h_ref[rows, :] = h.astype(BF16)


def _out_proj(ya, yb, ga, gb, w, layer, x, mod_l, g2, seq_lens):
    nt, d = x.shape
    tm = 512
    starts = tuple(int(s) // tm for s in np.cumsum((0,) + seq_lens[:-1]))
    const = lambda i: (0, 0)
    return pl.pallas_call(
        _out_kernel,
        grid=(nt // tm,),
        in_specs=[pl.BlockSpec((N_HEADS_A, tm, HEAD_DIM), lambda i: (0, i, 0)),
                  pl.BlockSpec((tm, D_B), lambda i: (i, 0)),
                  pl.BlockSpec((1, D_A), const),
                  pl.BlockSpec((1, D_B), const),
                  pl.BlockSpec((None, D_A + D_B, d), lambda i: (layer, 0, 0), pipeline_mode=pl.Buffered(1)),
                  pl.BlockSpec((tm, d), lambda i: (i, 0)),
                  pl.BlockSpec((1, 6, d), lambda i: (_seq_index(i, starts), 0, 0)),
                  pl.BlockSpec((1, d), const)],
        out_specs=[pl.BlockSpec((tm, d), lambda i: (i, 0)),
                   pl.BlockSpec((tm, d), lambda i: (i, 0))],
        out_shape=[jax.ShapeDtypeStruct((nt, d), F32), jax.ShapeDtypeStruct((nt, d), BF16)],
        compiler_params=_cparams(("parallel",), 48),
        name="out_proj",
    )(ya, yb, ga, gb, w, x, mod_l, g2)


def _ffn_kernel(h_ref, wg_ref, wu_ref, wd_ref, x_ref, mod_ref, ng_ref, *refs, split_block):
    acc_ref = refs[-1]
    f = pl.program_id(1)

    @pl.when(f == 0)
    def _():
        acc_ref[...] = jnp.zeros_like(acc_ref)

    h = h_ref[...]
    a = jnp.dot(h, wg_ref[...], preferred_element_type=F32)
    b = jnp.dot(h, wu_ref[...], preferred_element_type=F32)
    act = (a / (1.0 + jnp.exp(-a)) * b).astype(BF16)
    acc_ref[...] += jnp.dot(act, wd_ref[...], preferred_element_type=F32)

    @pl.when(f == pl.num_programs(1) - 1)
    def _():
        x = x_ref[...] + mod_ref[0, 5:6, :] * acc_ref[...]
        if split_block is not None:
            y0_ref, y1_ref = refs[:-1]
            y = x * lax.rsqrt(jnp.mean(x * x, axis=-1, keepdims=True) + EPS) * ng_ref[...]
            i = pl.program_id(0)

            @pl.when(i < split_block)
            def _():
                y0_ref[...] = y

            @pl.when(i >= split_block)
            def _():
                y1_ref[...] = y
        else:
            nmod_ref, xo_ref, hn_ref = refs[:-1]
            xo_ref[...] = x
            hn = _modulated_norm(x, ng_ref[...], nmod_ref[0, 0:1, :], nmod_ref[0, 1:2, :])
            hn_ref[...] = hn.astype(BF16)


def _ffn(h, wg, wu, wd, layer, x, mod_l, next_g, next_mod, seq_lens, out_split=None):
    nt, d = x.shape
    dff = wg.shape[2]
    tm, tf = 512, 512
    final = next_mod is None
    split_block = out_split // tm if final else None
    starts = tuple(int(s) // tm for s in np.cumsum((0,) + seq_lens[:-1]))
    row_blk = pl.BlockSpec((tm, d), lambda i, f: (i, 0))
    mod_blk = pl.BlockSpec((1, 6, d), lambda i, f: (_seq_index(i, starts), 0, 0))
    in_specs = [row_blk,
                pl.BlockSpec((None, d, tf), lambda i, f: (layer, 0, f)),
                pl.BlockSpec((None, d, tf), lambda i, f: (layer, 0, f)),
                pl.BlockSpec((None, tf, d), lambda i, f: (layer, f, 0)),
                row_blk,
                mod_blk,
                pl.BlockSpec((1, d), lambda i, f: (0, 0))]
    operands = [h, wg, wu, wd, x, mod_l, next_g]
    if final:
        out_specs = [pl.BlockSpec((tm, d), lambda i, f: (jnp.minimum(i, split_block - 1), 0)),
                     pl.BlockSpec((tm, d), lambda i, f: (jnp.maximum(i - split_block, 0), 0))]
        out_shape = [jax.ShapeDtypeStruct((out_split, d), F32),
                     jax.ShapeDtypeStruct((nt - out_split, d), F32)]
    else:
        in_specs.append(mod_blk)
        operands.append(next_mod)
        out_specs = [row_blk, row_blk]
        out_shape = [jax.ShapeDtypeStruct((nt, d), F32), jax.ShapeDtypeStruct((nt, d), BF16)]
    return pl.pallas_call(
        functools.partial(_ffn_kernel, split_block=split_block),
        grid=(nt // tm, dff // tf),
        in_specs=in_specs,
        out_specs=out_specs,
        out_shape=out_shape,
        scratch_shapes=[pltpu.VMEM((tm, d), F32)],
        compiler_params=_cparams(("arbitrary" if final else "parallel", "arbitrary"), 56),
        name="ffn_final" if final else "ffn",
    )(*operands)


def _trunk(x, c, seq_lens, out_split, t5_table, norm1_g, norm2_g, w_ada, b_ada, w_in, out_norm_a,
           out_norm_b, w_out, na_rpb, w_gate, w_up, w_down, final_g):
    depth = w_in.shape[0]
    nseq = len(seq_lens)
    d = x.shape[1]
    assert all(s % DIL_BLOCK == 0 for s in seq_lens)

    c_pad = jnp.zeros((8, d), F32).at[:nseq].set(c)
    mod = _ada_modulation(c_pad, w_ada, b_ada)[:, :nseq].reshape(depth, nseq, 6, d)

    col_scale = jnp.ones((6, D_A), F32).at[0].set(QK_SCALE).at[3].set(QK_SCALE).reshape(1, 6 * D_A)
    dil_bias = _dilated_bias(t5_table)
    w_in, w_out, w_gate, w_up, w_down = (w.astype(BF16) for w in (w_in, w_out, w_gate, w_up, w_down))
    h = _prenorm(x, mod[0], norm1_g[0][None], seq_lens)
    for l in range(depth):
        final = l == depth - 1
        qkv_a, qkv_a16 = _qkv_proj(h, col_scale, w_in, l, 0, True)
        qkv_b, = _qkv_proj(h, col_scale, w_in, l, 1, False)
        ya = _dilated_attention(qkv_a, qkv_a16, dil_bias, seq_lens)
        yb = _neighborhood_attention(qkv_b, _na_bias(na_rpb[l]), seq_lens)
        x, h2 = _out_proj(ya, yb, out_norm_a[l][None], out_norm_b[l][None], w_out, l,
                          x, mod[l], norm2_g[l][None], seq_lens)
        next_g = final_g[None] if final else norm1_g[l + 1][None]
        next_mod = None if final else mod[l + 1]
        out = _ffn(h2, w_gate, w_up, w_down, l, x, mod[l], next_g, next_mod, seq_lens, out_split)
        if not final:
            x, h = out
    return out


def kernel(x_prompt, x_sample, c_prompt, c_sample, t5_table, norm1_g, norm2_g, w_ada, b_ada, w_in,
           out_norm_a, out_norm_b, w_out, na_rpb, w_gate, w_up, w_down, final_g):
    bp, tp, d = x_prompt.shape
    bs, ts, _ = x_sample.shape
    seq_lens = (tp,) * bp + (ts,) * bs
    x = jnp.concatenate([x_prompt.reshape(bp * tp, d), x_sample.reshape(bs * ts, d)], axis=0)
    c = jnp.concatenate([c_prompt, c_sample], axis=0)
    y_prompt, y_sample = _trunk(x, c, seq_lens, bp * tp, t5_table, norm1_g, norm2_g, w_ada, b_ada, w_in,
                                out_norm_a, out_norm_b, w_out, na_rpb, w_gate, w_up, w_down, final_g)
    return (y_prompt.reshape(bp, tp, d), y_sample.reshape(bs, ts, d))
```

```python
import functools
import math

import numpy as np
import jax
import jax.numpy as jnp
from jax import lax
from jax.experimental import pallas as pl
from jax.experimental.pallas import tpu as pltpu

D_MODEL = 2048
HEAD_DIM = 128
N_HEADS_A = 8
N_HEADS_B = 8
D_A = N_HEADS_A * HEAD_DIM
D_B = N_HEADS_B * HEAD_DIM
QK_SCALE = HEAD_DIM ** -0.5
DILATIONS = (1, 4, 16)
RADIUS = 64
N_BUCKETS = 32
MAX_DISTANCE = 1024
GRID_W = 64
NA_ROWS = 8
NA_COLS = 16
EPS = 1e-6
NEG = -1e30

DIL_BLOCK = 1024
DIL_VIEW = 16
VIEW_ROWS = DIL_BLOCK // DIL_VIEW
TILES = DIL_BLOCK // RADIUS
DIL_HEADS_PER_STEP = 2
NA_BLOCK_ROWS = 16
NA_HALO_ROWS = 8
NA_BLOCK = NA_BLOCK_ROWS * GRID_W
NA_HEADS_PER_STEP = 2
NA_HALO = NA_HALO_ROWS * GRID_W

F32 = jnp.float32
BF16 = jnp.bfloat16
MIB = 1024 * 1024


def _cparams(sem, vmem_mib):
    return pltpu.CompilerParams(dimension_semantics=sem, vmem_limit_bytes=vmem_mib * MIB)


def _seq_index(i, starts):
    idx = 0
    for s in starts[1:]:
        idx = idx + (i >= s).astype(jnp.int32)
    return idx


def _lanes(r):
    return slice(r * HEAD_DIM, (r + 1) * HEAD_DIM)


def _ada_kernel(c_ref, w_ref, b_ref, o_ref):
    c = c_ref[...]
    s = c / (1.0 + jnp.exp(-c))
    o_ref[0] = jnp.dot(s, w_ref[0], precision=lax.Precision.HIGHEST,
                       preferred_element_type=F32) + b_ref[0]


def _ada_modulation(c_pad, w_ada, b_ada):
    depth, d, n = w_ada.shape
    tn = 1024
    return pl.pallas_call(
        _ada_kernel,
        grid=(depth, n // tn),
        in_specs=[pl.BlockSpec((8, d), lambda l, j: (0, 0)),
                  pl.BlockSpec((1, d, tn), lambda l, j: (l, 0, j)),
                  pl.BlockSpec((1, 1, tn), lambda l, j: (l, 0, j))],
        out_specs=pl.BlockSpec((1, 8, tn), lambda l, j: (l, 0, j)),
        out_shape=jax.ShapeDtypeStruct((depth, 8, n), F32),
        compiler_params=_cparams(("parallel", "parallel"), 40),
        name="ada_modulation",
    )(c_pad, w_ada, b_ada.reshape(depth, 1, n))


def _modulated_norm(x, g, shift, scale):
    y = x * lax.rsqrt(jnp.mean(x * x, axis=-1, keepdims=True) + EPS) * g
    return y * (1.0 + scale) + shift


def _prenorm_kernel(x_ref, mod_ref, g_ref, h_ref):
    h = _modulated_norm(x_ref[...], g_ref[...], mod_ref[0, 0:1, :], mod_ref[0, 1:2, :])
    h_ref[...] = h.astype(BF16)


def _prenorm(x, mod_l, g, seq_lens):
    nt, d = x.shape
    tm = 512
    starts = tuple(int(s) // tm for s in np.cumsum((0,) + seq_lens[:-1]))
    return pl.pallas_call(
        _prenorm_kernel,
        grid=(nt // tm,),
        in_specs=[pl.BlockSpec((tm, d), lambda i: (i, 0)),
                  pl.BlockSpec((1, 6, d), lambda i: (_seq_index(i, starts), 0, 0)),
                  pl.BlockSpec((1, d), lambda i: (0, 0))],
        out_specs=pl.BlockSpec((tm, d), lambda i: (i, 0)),
        out_shape=jax.ShapeDtypeStruct((nt, d), BF16),
        compiler_params=_cparams(("parallel",), 32),
        name="prenorm",
    )(x, mod_l, g)


SLOT_K, SLOT_V, SLOT_Q = range(3)
KV_BLOCK = SLOT_K // 2


def _qkv_kernel(h_ref, cs_ref, w_ref, o_ref, *view, tm):
    r = jnp.dot(h_ref[...], w_ref[...], preferred_element_type=F32) * cs_ref[...]
    for c in range(o_ref.shape[0]):
        o_ref[c, 0] = r[:, _lanes(c)].astype(BF16)
    if view:
        ov_ref, r_s = view
        for c in range(o_ref.shape[0]):
            r_s[c] = r[:, _lanes(c)]
            for g in range(DIL_VIEW):
                rows = r_s[c, pl.ds(g, tm // DIL_VIEW, stride=DIL_VIEW), :]
                ov_ref[c, 0, :, _lanes(g)] = rows.astype(BF16)


def _qkv_proj(h, col_scale, w, layer, group, with_view):
    nt, d = h.shape
    tm = 1024
    width = N_HEADS_A * HEAD_DIM
    assert w.shape[2] == 6 * width and N_HEADS_A == N_HEADS_B
    slot = lambda j: lax.rem(j + 2, 3)
    out_specs = [pl.BlockSpec((N_HEADS_A, 1, tm, HEAD_DIM), lambda i, j: (0, slot(j), i, 0))]
    out_shape = [jax.ShapeDtypeStruct((N_HEADS_A, 3, nt, HEAD_DIM), BF16)]
    scratch = []
    if with_view:
        out_specs.append(pl.BlockSpec((N_HEADS_A, 1, tm // DIL_VIEW, DIL_VIEW * HEAD_DIM),
                                      lambda i, j: (0, slot(j), i, 0)))
        out_shape.append(jax.ShapeDtypeStruct((N_HEADS_A, 3, nt // DIL_VIEW, DIL_VIEW * HEAD_DIM), BF16))
        scratch.append(pltpu.VMEM((N_HEADS_A, tm, HEAD_DIM), F32))
    return pl.pallas_call(
        functools.partial(_qkv_kernel, tm=tm),
        grid=(nt // tm, 3),
        in_specs=[pl.BlockSpec((tm, d), lambda i, j: (i, 0)),
                  pl.BlockSpec((1, width), lambda i, j: (0, 3 * group + j)),
                  pl.BlockSpec((None, d, width), lambda i, j: (layer, 0, 3 * group + j))],
        out_specs=out_specs,
        out_shape=out_shape,
        scratch_shapes=scratch,
        compiler_params=_cparams(("parallel", "parallel"), 48),
        name="qkv_proj_view" if with_view else "qkv_proj",
    )(h, col_scale, w)


def _t5_bucket(rel):
    nb = N_BUCKETS // 2
    exact = nb // 2
    n = np.abs(rel)
    sign = np.where(rel > 0, nb, 0)
    large = exact + (np.log(np.maximum(n, 1) / exact) / math.log(MAX_DISTANCE / exact) * (nb - exact)).astype(np.int64)
    large = np.minimum(large, nb - 1)
    return (sign + np.where(n < exact, n, large)).astype(np.int32)


def _tile_offsets():
    i = np.arange(RADIUS)[:, None]
    j = np.arange(3 * RADIUS)[None, :]
    band = j - RADIUS - i
    qa, qi = i // 16, i % 16
    ka, kj = j // 48, j % 48
    perm = 4 * (kj - 16 - qi) + (ka - qa)
    return (band, perm, band)


def _dilated_bias(t5_table):
    table = t5_table.astype(F32).T
    tiles = []
    for dil, rel in zip(DILATIONS, _tile_offsets()):
        inside = np.abs(rel) <= RADIUS
        bucket = _t5_bucket(rel * dil)
        b = jnp.full((table.shape[0],) + rel.shape, NEG, F32)
        for o in range(N_BUCKETS):
            b = jnp.where((inside & (bucket == o))[None], table[:, o, None, None], b)
        tiles.append(b)
    return jnp.stack(tiles, axis=1)


def _view_rows(ref_p, ref_c, ref_n, lo, hi, lanes):
    parts = []
    if lo < 0:
        parts.append(ref_p[VIEW_ROWS + lo:VIEW_ROWS, lanes])
    parts.append(ref_c[max(lo, 0):min(hi, VIEW_ROWS), lanes])
    if hi > VIEW_ROWS:
        parts.append(ref_n[0:hi - VIEW_ROWS, lanes])
    return parts


def _dil_kernel(q1_blk, kv1p_blk, kv1c_blk, kv1n_blk, q16_blk, kv16p_blk, kv16c_blk, kv16n_blk,
                bias_ref, o_ref, s_all, e_all, pos_all, num_all, m_all_s, l_all, kt_all,
                *, first_blocks, last_blocks):
    n = pl.program_id(1)
    first = functools.reduce(jnp.logical_or, [n == s for s in first_blocks])
    last = functools.reduce(jnp.logical_or, [n == s for s in last_blocks])
    fpen = jnp.where(first, NEG, 0.0).astype(F32)
    lpen = jnp.where(last, NEG, 0.0).astype(F32)
    col = lax.broadcasted_iota(jnp.int32, (1, 3 * RADIUS), 1)
    pen_first = jnp.where(col < RADIUS, fpen, 0.0)
    pen_last = jnp.where(col >= 2 * RADIUS, lpen, 0.0)
    pen_perm_first = jnp.where(lax.rem(col, 48) < 16, fpen, 0.0)
    pen_perm_last = jnp.where(lax.rem(col, 48) >= 32, lpen, 0.0)
    ones = jnp.ones((3 * RADIUS, HEAD_DIM), BF16)

    def keys_d1(refs, b):
        ref_p, ref_c, ref_n = refs
        if b == 0:
            return jnp.concatenate([ref_p[...], ref_c[0:2 * RADIUS]], axis=0)
        if b == TILES - 1:
            return jnp.concatenate([ref_c[DIL_BLOCK - 2 * RADIUS:DIL_BLOCK], ref_n[...]], axis=0)
        return ref_c[RADIUS * (b - 1):RADIUS * (b + 2)]

    def keys_d4(refs, r4, b):
        parts = []
        for a in range(4):
            parts += _view_rows(*refs, 16 * b - 16, 16 * b + 32, _lanes(4 * a + r4))
        return jnp.concatenate(parts, axis=0)

    def keys_d16(refs, r):
        return jnp.concatenate([ref[:, _lanes(r)] for ref in refs], axis=0)

    def head_passes(hh):
        q1_ref, q16_ref = q1_blk.at[hh, 0], q16_blk.at[hh, 0]
        k1 = tuple(blk.at[hh, 0] for blk in (kv1p_blk, kv1c_blk, kv1n_blk))
        v1 = tuple(blk.at[hh, 1] for blk in (kv1p_blk, kv1c_blk, kv1n_blk))
        k16 = tuple(blk.at[hh, 0] for blk in (kv16p_blk, kv16c_blk, kv16n_blk))
        v16 = tuple(blk.at[hh, 1] for blk in (kv16p_blk, kv16c_blk, kv16n_blk))
        s_s, e_s, pos_s = s_all.at[hh], e_all.at[hh], pos_all.at[hh]
        num_s, m_s, l_s, kt_s = num_all.at[hh], m_all_s.at[hh], l_all.at[hh], kt_all.at[hh]

        def tile_operands(p, t):
            if p == 0:
                pen = pen_first if t == 0 else pen_last if t == TILES - 1 else None
                return q1_ref[RADIUS * t:RADIUS * (t + 1)], keys_d1(k1, t), keys_d1(v1, t), pen
            if p == 1:
                r4, b = divmod(t, 4)
                q = jnp.concatenate([q16_ref[16 * b:16 * b + 16, _lanes(4 * a + r4)] for a in range(4)],
                                    axis=0)
                pen = pen_perm_first if b == 0 else pen_perm_last if b == 3 else None
                return q, keys_d4(k16, r4, b), keys_d4(v16, r4, b), pen
            return q16_ref[:, _lanes(t)], keys_d16(k16, t), keys_d16(v16, t), pen_first + pen_last

        def store_rows(p, t, tile, pos_idx, view_ref):
            if p == 0:
                pos_s[pos_idx, RADIUS * t:RADIUS * (t + 1), :] = tile
            elif p == 1:
                r4, b = divmod(t, 4)
                for a in range(4):
                    view_ref[0, 16 * b:16 * b + 16, _lanes(4 * a + r4)] = tile[16 * a:16 * a + 16]
            else:
                view_ref[1, :, _lanes(t)] = tile

        def transpose_keys(p, t):
            kt_s[p, t] = tile_operands(p, t)[1].T

        def logits(p, t):
            q, _, _, pen = tile_operands(p, t)
            s = jnp.dot(q, kt_s[p, t], preferred_element_type=F32)
            s = s + bias_ref[hh, p]
            s_s[p, t] = s if pen is None else s + pen

        def softmax(p, t):
            s = s_s[p, t]
            m = jnp.max(s, axis=-1, keepdims=True)
            e_s[p, t] = jnp.exp(s - m).astype(BF16)
            store_rows(p, t, jnp.broadcast_to(m, (RADIUS, HEAD_DIM)), 1, m_s)

        def values(p, t):
            _, _, v, _ = tile_operands(p, t)
            nv = jnp.dot(e_s[p, t], jnp.concatenate([v, ones], axis=1), preferred_element_type=F32)
            store_rows(p, t, nv[:, :HEAD_DIM], 0, num_s)
            store_rows(p, t, nv[:, HEAD_DIM:], 2, l_s)

        def merge():
            for r in range(DIL_VIEW):
                strided = pl.ds(r, VIEW_ROWS, stride=DIL_VIEW)
                nums = (pos_s[0, strided, :], num_s[0, :, _lanes(r)], num_s[1, :, _lanes(r)])
                ms = (pos_s[1, strided, :], m_s[0, :, _lanes(r)], m_s[1, :, _lanes(r)])
                ls = (pos_s[2, strided, :], l_s[0, :, _lanes(r)], l_s[1, :, _lanes(r)])
                m_max = jnp.maximum(jnp.maximum(ms[0], ms[1]), ms[2])
                ws = [jnp.exp(m - m_max) for m in ms]
                num = ws[0] * nums[0] + ws[1] * nums[1] + ws[2] * nums[2]
                den = ws[0] * ls[0] + ws[1] * ls[1] + ws[2] * ls[2]
                o_ref[hh, strided, :] = num / den

        def all_tiles(p, *tile_stages):
            return lambda: [stage(p, t) for stage in tile_stages for t in range(TILES)]

        return [all_tiles(2, transpose_keys, logits), all_tiles(1, transpose_keys, logits),
                all_tiles(2, softmax), all_tiles(2, values),
                all_tiles(0, transpose_keys, logits), all_tiles(1, softmax), all_tiles(1, values),
                all_tiles(0, softmax), all_tiles(0, values), merge]

    first_head, second_head = head_passes(0), head_passes(1)
    for run in first_head[:7] + second_head[:2] + first_head[7:9] + second_head[2:5] + first_head[9:] + second_head[5:]:
        run()


def _dilated_attention(qkv, qkv16, bias, seq_lens):
    nt = qkv.shape[2]
    nblk = nt // DIL_BLOCK
    bounds = np.cumsum((0,) + seq_lens) // DIL_BLOCK
    first_blocks = tuple(int(b) for b in bounds[:-1])
    last_blocks = tuple(int(b) - 1 for b in bounds[1:])
    per = DIL_BLOCK // RADIUS
    prev1 = lambda n: jnp.maximum(per * n - 1, 0)
    next1 = lambda n: jnp.minimum(per * (n + 1), nt // RADIUS - 1)
    prev16 = lambda n: jnp.maximum(n - 1, 0)
    next16 = lambda n: jnp.minimum(n + 1, nblk - 1)
    kv_blk = KV_BLOCK
    hps = DIL_HEADS_PER_STEP
    blk16 = (hps, 1, VIEW_ROWS, DIL_VIEW * HEAD_DIM)
    kv16 = (hps, 2, VIEW_ROWS, DIL_VIEW * HEAD_DIM)
    npat = len(DILATIONS)
    kernel = functools.partial(_dil_kernel, first_blocks=first_blocks, last_blocks=last_blocks)
    out = pl.pallas_call(
        kernel,
        grid=(N_HEADS_A // hps, nblk),
        in_specs=[pl.BlockSpec((hps, 1, DIL_BLOCK, HEAD_DIM), lambda h, n: (h, SLOT_Q, n, 0)),
                  pl.BlockSpec((hps, 2, RADIUS, HEAD_DIM), lambda h, n: (h, kv_blk, prev1(n), 0)),
                  pl.BlockSpec((hps, 2, DIL_BLOCK, HEAD_DIM), lambda h, n: (h, kv_blk, n, 0)),
                  pl.BlockSpec((hps, 2, RADIUS, HEAD_DIM), lambda h, n: (h, kv_blk, next1(n), 0)),
                  pl.BlockSpec(blk16, lambda h, n: (h, SLOT_Q, n, 0)),
                  pl.BlockSpec(kv16, lambda h, n: (h, kv_blk, prev16(n), 0)),
                  pl.BlockSpec(kv16, lambda h, n: (h, kv_blk, n, 0)),
                  pl.BlockSpec(kv16, lambda h, n: (h, kv_blk, next16(n), 0)),
                  pl.BlockSpec((hps, npat, RADIUS, 3 * RADIUS), lambda h, n: (h, 0, 0, 0))],
        out_specs=pl.BlockSpec((hps, DIL_BLOCK, HEAD_DIM), lambda h, n: (h, n, 0)),
        out_shape=jax.ShapeDtypeStruct((N_HEADS_A, nt, HEAD_DIM), F32),
        scratch_shapes=[pltpu.VMEM((hps, npat, TILES, RADIUS, 3 * RADIUS), F32),
                        pltpu.VMEM((hps, npat, TILES, RADIUS, 3 * RADIUS), BF16),
                        pltpu.VMEM((hps, 3, DIL_BLOCK, HEAD_DIM), F32),
                        pltpu.VMEM((hps, 2, VIEW_ROWS, DIL_VIEW * HEAD_DIM), F32),
                        pltpu.VMEM((hps, 2, VIEW_ROWS, DIL_VIEW * HEAD_DIM), F32),
                        pltpu.VMEM((hps, 2, VIEW_ROWS, DIL_VIEW * HEAD_DIM), F32),
                        pltpu.VMEM((hps, npat, TILES, HEAD_DIM, 3 * RADIUS), BF16)],
        compiler_params=_cparams(("parallel", "parallel"), 48),
        name="dilated_attention",
    )(qkv, qkv, qkv, qkv, qkv16, qkv16, qkv16, qkv16, bias)
    return out


def _na_bias(rpb):
    c = np.arange(GRID_W)
    cstart = np.clip(c - NA_COLS // 2, 0, GRID_W - NA_COLS)
    cmask = (c[None, :] >= cstart[:, None]) & (c[None, :] < cstart[:, None] + NA_COLS)
    coff = np.clip(c[None, :] - c[:, None], -(NA_COLS - 1), NA_COLS - 1) + NA_COLS - 1
    cols = jnp.full(rpb.shape[:2] + (GRID_W, GRID_W), NEG, F32)
    for o in range(2 * NA_COLS - 1):
        cols = jnp.where((cmask & (coff == o))[None, None], rpb[:, :, o, None, None].astype(F32), cols)
    tiles = [cols[:, NA_ROWS - 1 - dd:2 * NA_ROWS - 1 - dd] for dd in range(NA_ROWS)]
    b = jnp.stack(tiles, axis=1).transpose(0, 1, 3, 2, 4)
    return b.reshape(rpb.shape[0], NA_ROWS, GRID_W, NA_ROWS * GRID_W)


def _na_kernel(q_blk, kvp_blk, kvc_blk, kvn_blk, bias_ref, o_ref, kw, vw,
               s_s, e_s, kt_s, *, block_starts, seq_rows):
    n = pl.program_id(1)
    nstart = jnp.int32(block_starts[0])
    rows = jnp.int32(seq_rows[0])
    for s, r in zip(block_starts[1:], seq_rows[1:]):
        nstart = jnp.where(n >= s, s, nstart)
        rows = jnp.where(n >= s, r, rows)

    nkeys = NA_ROWS * GRID_W
    shifts, starts = [], []
    for i in range(NA_BLOCK_ROWS):
        r = (n - nstart) * NA_BLOCK_ROWS + i
        dd = r - jnp.clip(r - NA_ROWS // 2, 0, rows - NA_ROWS)
        shifts.append(dd)
        starts.append(pl.multiple_of((NA_HALO_ROWS + i - dd) * GRID_W, GRID_W))
    ones = jnp.ones((nkeys, HEAD_DIM), BF16)

    def head_passes(hh):
        q_ref = q_blk.at[hh, 0]

        def logits_pass():
            for which, win in enumerate((kw, vw)):
                win[hh, 0:NA_HALO, :] = kvp_blk[hh, which]
                win[hh, NA_HALO:NA_HALO + NA_BLOCK, :] = kvc_blk[hh, which]
                win[hh, NA_HALO + NA_BLOCK:, :] = kvn_blk[hh, which]
            for i in range(NA_BLOCK_ROWS):
                kt_s[hh, i] = kw[hh, pl.ds(starts[i], nkeys), :].T
            for i in range(NA_BLOCK_ROWS):
                q = q_ref[i * GRID_W:(i + 1) * GRID_W, :]
                s = jnp.dot(q, kt_s[hh, i], preferred_element_type=F32)
                s_s[hh, i] = s + bias_ref[hh, shifts[i]]

        def softmax_pass():
            for i in range(NA_BLOCK_ROWS):
                s = s_s[hh, i]
                e_s[hh, i] = jnp.exp(s - jnp.max(s, axis=-1, keepdims=True)).astype(BF16)

        def values_pass():
            for i in range(NA_BLOCK_ROWS):
                v = vw[hh, pl.ds(starts[i], nkeys), :]
                nv = jnp.dot(e_s[hh, i], jnp.concatenate([v, ones], axis=1), preferred_element_type=F32)
                out = (nv[:, :HEAD_DIM] / nv[:, HEAD_DIM:]).astype(o_ref.dtype)
                o_ref[i * GRID_W:(i + 1) * GRID_W, _lanes(hh)] = out

        return [logits_pass, softmax_pass, values_pass]

    first_head, second_head = head_passes(0), head_passes(1)
    for run in (first_head[0], second_head[0], first_head[1], first_head[2], second_head[1], second_head[2]):
        run()


def _neighborhood_attention(qkv, bias, seq_lens):
    nt = qkv.shape[2]
    block_starts = tuple(int(s) // NA_BLOCK for s in np.cumsum((0,) + seq_lens[:-1]))
    seq_rows = tuple(int(s) // GRID_W for s in seq_lens)
    assert all(r >= NA_ROWS and r % NA_BLOCK_ROWS == 0 for r in seq_rows)
    per = NA_BLOCK // NA_HALO
    prev = lambda n: jnp.maximum(per * n - 1, 0)
    nxt = lambda n: jnp.minimum(per * (n + 1), nt // NA_HALO - 1)
    kv_blk = KV_BLOCK
    kernel = functools.partial(_na_kernel, block_starts=block_starts, seq_rows=seq_rows)
    nkeys = NA_ROWS * GRID_W
    hps = NA_HEADS_PER_STEP
    return pl.pallas_call(
        kernel,
        grid=(N_HEADS_B // hps, nt // NA_BLOCK),
        in_specs=[pl.BlockSpec((hps, 1, NA_BLOCK, HEAD_DIM), lambda h, n: (h, SLOT_Q, n, 0)),
                  pl.BlockSpec((hps, 2, NA_HALO, HEAD_DIM), lambda h, n: (h, kv_blk, prev(n), 0)),
                  pl.BlockSpec((hps, 2, NA_BLOCK, HEAD_DIM), lambda h, n: (h, kv_blk, n, 0)),
                  pl.BlockSpec((hps, 2, NA_HALO, HEAD_DIM), lambda h, n: (h, kv_blk, nxt(n), 0)),
                  pl.BlockSpec((hps, NA_ROWS, GRID_W, nkeys), lambda h, n: (h, 0, 0, 0))],
        out_specs=pl.BlockSpec((NA_BLOCK, hps * HEAD_DIM), lambda h, n: (n, h)),
        out_shape=jax.ShapeDtypeStruct((nt, D_B), BF16),
        scratch_shapes=[pltpu.VMEM((hps, NA_BLOCK + 2 * NA_HALO, HEAD_DIM), BF16),
                        pltpu.VMEM((hps, NA_BLOCK + 2 * NA_HALO, HEAD_DIM), BF16),
                        pltpu.VMEM((hps, NA_BLOCK_ROWS, GRID_W, nkeys), F32),
                        pltpu.VMEM((hps, NA_BLOCK_ROWS, GRID_W, nkeys), BF16),
                        pltpu.VMEM((hps, NA_BLOCK_ROWS, HEAD_DIM, nkeys), BF16)],
        compiler_params=_cparams(("parallel", "parallel"), 40),
        name="neighborhood_attention",
    )(qkv, qkv, qkv, qkv, bias)


def _plain_norm(y, g):
    y = y.astype(F32)
    return (y * lax.rsqrt(jnp.mean(y * y, axis=-1, keepdims=True) + EPS) * g).astype(BF16)


OUT_ROW_CHUNK = 256


def _out_kernel(ya_ref, yb_ref, ga_ref, gb_ref, w_ref, x_ref, mod_ref, g2_ref, xo_ref, h_ref):
    for r0 in range(0, x_ref.shape[0], OUT_ROW_CHUNK):
        rows = slice(r0, r0 + OUT_ROW_CHUNK)
        ya = jnp.concatenate([ya_ref[h, rows, :] for h in range(N_HEADS_A)], axis=1)
        y = jnp.dot(_plain_norm(ya, ga_ref[...]), w_ref[:D_A, :], preferred_element_type=F32)
        y = y + jnp.dot(_plain_norm(yb_ref[rows, :], gb_ref[...]), w_ref[D_A:, :],
                        preferred_element_type=F32)
        x = x_ref[rows, :] + mod_ref[0, 2:3, :] * y
        xo_ref[rows, :] = x
        h = _modulated_norm(x, g2_ref[...], mod_ref[0, 3:4, :], mod_ref[0, 4:5, :])
        h_ref[rows, :] = h.astype(BF16)


def _out_proj(ya, yb, ga, gb, w, layer, x, mod_l, g2, seq_lens):
    nt, d = x.shape
    tm = 512
    starts = tuple(int(s) // tm for s in np.cumsum((0,) + seq_lens[:-1]))
    const = lambda i: (0, 0)
    return pl.pallas_call(
        _out_kernel,
        grid=(nt // tm,),
        in_specs=[pl.BlockSpec((N_HEADS_A, tm, HEAD_DIM), lambda i: (0, i, 0)),
                  pl.BlockSpec((tm, D_B), lambda i: (i, 0)),
                  pl.BlockSpec((1, D_A), const),
                  pl.BlockSpec((1, D_B), const),
                  pl.BlockSpec((None, D_A + D_B, d), lambda i: (layer, 0, 0), pipeline_mode=pl.Buffered(1)),
                  pl.BlockSpec((tm, d), lambda i: (i, 0)),
                  pl.BlockSpec((1, 6, d), lambda i: (_seq_index(i, starts), 0, 0)),
                  pl.BlockSpec((1, d), const)],
        out_specs=[pl.BlockSpec((tm, d), lambda i: (i, 0)),
                   pl.BlockSpec((tm, d), lambda i: (i, 0))],
        out_shape=[jax.ShapeDtypeStruct((nt, d), F32), jax.ShapeDtypeStruct((nt, d), BF16)],
        compiler_params=_cparams(("parallel",), 48),
        name="out_proj",
    )(ya, yb, ga, gb, w, x, mod_l, g2)


def _ffn_kernel(h_ref, wg_ref, wu_ref, wd_ref, x_ref, mod_ref, ng_ref, *refs, split_block):
    acc_ref = refs[-1]
    f = pl.program_id(1)

    @pl.when(f == 0)
    def _():
        acc_ref[...] = jnp.zeros_like(acc_ref)

    h = h_ref[...]
    a = jnp.dot(h, wg_ref[...], preferred_element_type=F32)
    b = jnp.dot(h, wu_ref[...], preferred_element_type=F32)
    act = (a / (1.0 + jnp.exp(-a)) * b).astype(BF16)
    acc_ref[...] += jnp.dot(act, wd_ref[...], preferred_element_type=F32)

    @pl.when(f == pl.num_programs(1) - 1)
    def _():
        x = x_ref[...] + mod_ref[0, 5:6, :] * acc_ref[...]
        if split_block is not None:
            y0_ref, y1_ref = refs[:-1]
            y = x * lax.rsqrt(jnp.mean(x * x, axis=-1, keepdims=True) + EPS) * ng_ref[...]
            i = pl.program_id(0)

            @pl.when(i < split_block)
            def _():
                y0_ref[...] = y

            @pl.when(i >= split_block)
            def _():
                y1_ref[...] = y
        else:
            nmod_ref, xo_ref, hn_ref = refs[:-1]
            xo_ref[...] = x
            hn = _modulated_norm(x, ng_ref[...], nmod_ref[0, 0:1, :], nmod_ref[0, 1:2, :])
            hn_ref[...] = hn.astype(BF16)


def _ffn(h, wg, wu, wd, layer, x, mod_l, next_g, next_mod, seq_lens, out_split=None):
    nt, d = x.shape
    dff = wg.shape[2]
    tm, tf = 512, 512
    final = next_mod is None
    split_block = out_split // tm if final else None
    starts = tuple(int(s) // tm for s in np.cumsum((0,) + seq_lens[:-1]))
    row_blk = pl.BlockSpec((tm, d), lambda i, f: (i, 0))
    mod_blk = pl.BlockSpec((1, 6, d), lambda i, f: (_seq_index(i, starts), 0, 0))
    in_specs = [row_blk,
                pl.BlockSpec((None, d, tf), lambda i, f: (layer, 0, f)),
                pl.BlockSpec((None, d, tf), lambda i, f: (layer, 0, f)),
                pl.BlockSpec((None, tf, d), lambda i, f: (layer, f, 0)),
                row_blk,
                mod_blk,
                pl.BlockSpec((1, d), lambda i, f: (0, 0))]
    operands = [h, wg, wu, wd, x, mod_l, next_g]
    if final:
        out_specs = [pl.BlockSpec((tm, d), lambda i, f: (jnp.minimum(i, split_block - 1), 0)),
                     pl.BlockSpec((tm, d), lambda i, f: (jnp.maximum(i - split_block, 0), 0))]
        out_shape = [jax.ShapeDtypeStruct((out_split, d), F32),
                     jax.ShapeDtypeStruct((nt - out_split, d), F32)]
    else:
        in_specs.append(mod_blk)
        operands.append(next_mod)
        out_specs = [row_blk, row_blk]
        out_shape = [jax.ShapeDtypeStruct((nt, d), F32), jax.ShapeDtypeStruct((nt, d), BF16)]
    return pl.pallas_call(
        functools.partial(_ffn_kernel, split_block=split_block),
        grid=(nt // tm, dff // tf),
        in_specs=in_specs,
        out_specs=out_specs,
        out_shape=out_shape,
        scratch_shapes=[pltpu.VMEM((tm, d), F32)],
        compiler_params=_cparams(("arbitrary" if final else "parallel", "arbitrary"), 56),
        name="ffn_final" if final else "ffn",
    )(*operands)


def _trunk(x, c, seq_lens, out_split, t5_table, norm1_g, norm2_g, w_ada, b_ada, w_in, out_norm_a,
           out_norm_b, w_out, na_rpb, w_gate, w_up, w_down, final_g):
    depth = w_in.shape[0]
    nseq = len(seq_lens)
    d = x.shape[1]
    assert all(s % DIL_BLOCK == 0 for s in seq_lens)

    c_pad = jnp.zeros((8, d), F32).at[:nseq].set(c)
    mod = _ada_modulation(c_pad, w_ada, b_ada)[:, :nseq].reshape(depth, nseq, 6, d)

    col_scale = jnp.ones((6, D_A), F32).at[0].set(QK_SCALE).at[3].set(QK_SCALE).reshape(1, 6 * D_A)
    dil_bias = _dilated_bias(t5_table)
    w_in, w_out, w_gate, w_up, w_down = (w.astype(BF16) for w in (w_in, w_out, w_gate, w_up, w_down))
    h = _prenorm(x, mod[0], norm1_g[0][None], seq_lens)
    for l in range(depth):
        final = l == depth - 1
        qkv_a, qkv_a16 = _qkv_proj(h, col_scale, w_in, l, 0, True)
        qkv_b, = _qkv_proj(h, col_scale, w_in, l, 1, False)
        ya = _dilated_attention(qkv_a, qkv_a16, dil_bias, seq_lens)
        yb = _neighborhood_attention(qkv_b, _na_bias(na_rpb[l]), seq_lens)
        x, h2 = _out_proj(ya, yb, out_norm_a[l][None], out_norm_b[l][None], w_out, l,
                          x, mod[l], norm2_g[l][None], seq_lens)
        next_g = final_g[None] if final else norm1_g[l + 1][None]
        next_mod = None if final else mod[l + 1]
        out = _ffn(h2, w_gate, w_up, w_down, l, x, mod[l], next_g, next_mod, seq_lens, out_split)
        if not final:
            x, h = out
    return out


def kernel(x_prompt, x_sample, c_prompt, c_sample, t5_table, norm1_g, norm2_g, w_ada, b_ada, w_in,
           out_norm_a, out_norm_b, w_out, na_rpb, w_gate, w_up, w_down, final_g):
    bp, tp, d = x_prompt.shape
    bs, ts, _ = x_sample.shape
    seq_lens = (tp,) * bp + (ts,) * bs
    x = jnp.concatenate([x_prompt.reshape(bp * tp, d), x_sample.reshape(bs * ts, d)], axis=0)
    c = jnp.concatenate([c_prompt, c_sample], axis=0)
    y_prompt, y_sample = _trunk(x, c, seq_lens, bp * tp, t5_table, norm1_g, norm2_g, w_ada, b_ada, w_in,
                                out_norm_a, out_norm_b, w_out, na_rpb, w_gate, w_up, w_down, final_g)
    return (y_prompt.reshape(bp, tp, d), y_sample.reshape(bs, ts, d))
```

```python
import functools
import math

import numpy as np
import jax
import jax.numpy as jnp
from jax import lax
from jax.experimental import pallas as pl
from jax.experimental.pallas import tpu as pltpu

D_MODEL = 2048
HEAD_DIM = 128
N_HEADS_A = 8
N_HEADS_B = 8
D_A = N_HEADS_A * HEAD_DIM
D_B = N_HEADS_B * HEAD_DIM
QK_SCALE = HEAD_DIM ** -0.5
DILATIONS = (1, 4, 16)
RADIUS = 64
N_BUCKETS = 32
MAX_DISTANCE = 1024
GRID_W = 64
NA_ROWS = 8
NA_COLS = 16
EPS = 1e-6
NEG = -1e30

DIL_BLOCK = 1024
DIL_VIEW = 16
VIEW_ROWS = DIL_BLOCK // DIL_VIEW
TILES = DIL_BLOCK // RADIUS
DIL_HEADS_PER_STEP = 2
NA_BLOCK_ROWS = 16
NA_HALO_ROWS = 8
NA_BLOCK = NA_BLOCK_ROWS * GRID_W
NA_HEADS_PER_STEP = 2
NA_HALO = NA_HALO_ROWS * GRID_W
NA_WINDOW = NA_BLOCK + 2 * NA_HALO

F32 = jnp.float32
BF16 = jnp.bfloat16
MIB = 1024 * 1024


def _cparams(sem, vmem_mib):
    return pltpu.CompilerParams(dimension_semantics=sem, vmem_limit_bytes=vmem_mib * MIB)


def _seq_index(i, starts):
    idx = 0
    for s in starts[1:]:
        idx = idx + (i >= s).astype(jnp.int32)
    return idx


def _lanes(r):
    return slice(r * HEAD_DIM, (r + 1) * HEAD_DIM)


def _ada_kernel(c_ref, w_ref, b_ref, o_ref):
    c = c_ref[...]
    s = c / (1.0 + jnp.exp(-c))
    o_ref[0] = jnp.dot(s, w_ref[0], precision=lax.Precision.HIGHEST,
                       preferred_element_type=F32) + b_ref[0]


def _ada_modulation(c_pad, w_ada, b_ada):
    depth, d, n = w_ada.shape
    tn = 1024
    return pl.pallas_call(
        _ada_kernel,
        grid=(depth, n // tn),
        in_specs=[pl.BlockSpec((8, d), lambda l, j: (0, 0)),
                  pl.BlockSpec((1, d, tn), lambda l, j: (l, 0, j)),
                  pl.BlockSpec((1, 1, tn), lambda l, j: (l, 0, j))],
        out_specs=pl.BlockSpec((1, 8, tn), lambda l, j: (l, 0, j)),
        out_shape=jax.ShapeDtypeStruct((depth, 8, n), F32),
        compiler_params=_cparams(("parallel", "parallel"), 40),
        name="ada_modulation",
    )(c_pad, w_ada, b_ada.reshape(depth, 1, n))


def _modulated_norm(x, g, shift, scale):
    y = x * lax.rsqrt(jnp.mean(x * x, axis=-1, keepdims=True) + EPS) * g
    return y * (1.0 + scale) + shift


def _prenorm_kernel(x_ref, mod_ref, g_ref, h_ref):
    h = _modulated_norm(x_ref[...], g_ref[...], mod_ref[0, 0:1, :], mod_ref[0, 1:2, :])
    h_ref[...] = h.astype(BF16)


def _prenorm(x, mod_l, g, seq_lens):
    nt, d = x.shape
    tm = 512
    starts = tuple(int(s) // tm for s in np.cumsum((0,) + seq_lens[:-1]))
    return pl.pallas_call(
        _prenorm_kernel,
        grid=(nt // tm,),
        in_specs=[pl.BlockSpec((tm, d), lambda i: (i, 0)),
                  pl.BlockSpec((1, 6, d), lambda i: (_seq_index(i, starts), 0, 0)),
                  pl.BlockSpec((1, d), lambda i: (0, 0))],
        out_specs=pl.BlockSpec((tm, d), lambda i: (i, 0)),
        out_shape=jax.ShapeDtypeStruct((nt, d), BF16),
        compiler_params=_cparams(("parallel",), 32),
        name="prenorm",
    )(x, mod_l, g)


SLOT_K, SLOT_V, SLOT_Q = range(3)
KV_BLOCK = SLOT_K // 2


def _qkv_kernel(h_ref, cs_ref, w_ref, o_ref, *view, tm):
    r = jnp.dot(h_ref[...], w_ref[...], preferred_element_type=F32) * cs_ref[...]
    for c in range(o_ref.shape[0]):
        o_ref[c, 0] = r[:, _lanes(c)].astype(BF16)
    if view:
        ov_ref, r_s = view
        for c in range(o_ref.shape[0]):
            r_s[c] = r[:, _lanes(c)]
            for g in range(DIL_VIEW):
                rows = r_s[c, pl.ds(g, tm // DIL_VIEW, stride=DIL_VIEW), :]
                ov_ref[c, 0, :, _lanes(g)] = rows.astype(BF16)


def _qkv_proj(h, col_scale, w, layer, group, with_view):
    nt, d = h.shape
    tm = 1024
    width = N_HEADS_A * HEAD_DIM
    assert w.shape[2] == 6 * width and N_HEADS_A == N_HEADS_B
    slot = lambda j: lax.rem(j + 2, 3)
    out_specs = [pl.BlockSpec((N_HEADS_A, 1, tm, HEAD_DIM), lambda i, j: (0, slot(j), i, 0))]
    out_shape = [jax.ShapeDtypeStruct((N_HEADS_A, 3, nt, HEAD_DIM), BF16)]
    scratch = []
    if with_view:
        out_specs.append(pl.BlockSpec((N_HEADS_A, 1, tm // DIL_VIEW, DIL_VIEW * HEAD_DIM),
                                      lambda i, j: (0, slot(j), i, 0)))
        out_shape.append(jax.ShapeDtypeStruct((N_HEADS_A, 3, nt // DIL_VIEW, DIL_VIEW * HEAD_DIM), BF16))
        scratch.append(pltpu.VMEM((N_HEADS_A, tm, HEAD_DIM), F32))
    return pl.pallas_call(
        functools.partial(_qkv_kernel, tm=tm),
        grid=(nt // tm, 3),
        in_specs=[pl.BlockSpec((tm, d), lambda i, j: (i, 0)),
                  pl.BlockSpec((1, width), lambda i, j: (0, 3 * group + j)),
                  pl.BlockSpec((None, d, width), lambda i, j: (layer, 0, 3 * group + j))],
        out_specs=out_specs,
        out_shape=out_shape,
        scratch_shapes=scratch,
        compiler_params=_cparams(("parallel", "parallel"), 48),
        name="qkv_proj_view" if with_view else "qkv_proj",
    )(h, col_scale, w)


def _t5_bucket(rel):
    nb = N_BUCKETS // 2
    exact = nb // 2
    n = np.abs(rel)
    sign = np.where(rel > 0, nb, 0)
    large = exact + (np.log(np.maximum(n, 1) / exact) / math.log(MAX_DISTANCE / exact) * (nb - exact)).astype(np.int64)
    large = np.minimum(large, nb - 1)
    return (sign + np.where(n < exact, n, large)).astype(np.int32)


def _tile_offsets():
    i = np.arange(RADIUS)[:, None]
    j = np.arange(3 * RADIUS)[None, :]
    band = j - RADIUS - i
    qa, qi = i // 16, i % 16
    ka, kj = j // 48, j % 48
    perm = 4 * (kj - 16 - qi) + (ka - qa)
    return (band, perm, band)


def _dilated_bias(t5_table):
    table = t5_table.astype(F32).T
    tiles = []
    for dil, rel in zip(DILATIONS, _tile_offsets()):
        inside = np.abs(rel) <= RADIUS
        bucket = _t5_bucket(rel * dil)
        b = jnp.full((table.shape[0],) + rel.shape, NEG, F32)
        for o in range(N_BUCKETS):
            b = jnp.where((inside & (bucket == o))[None], table[:, o, None, None], b)
        tiles.append(b)
    return jnp.stack(tiles, axis=1)


def _view_rows(ref_p, ref_c, ref_n, lo, hi, lanes):
    parts = []
    if lo < 0:
        parts.append(ref_p[VIEW_ROWS + lo:VIEW_ROWS, lanes])
    parts.append(ref_c[max(lo, 0):min(hi, VIEW_ROWS), lanes])
    if hi > VIEW_ROWS:
        parts.append(ref_n[0:hi - VIEW_ROWS, lanes])
    return parts


def _dil_kernel(q1_blk, kv1p_blk, kv1c_blk, kv1n_blk, q16_blk, kv16p_blk, kv16c_blk, kv16n_blk,
                bias_ref, o_ref, s_all, e_all, pos_all, num_all, m_all_s, l_all, kt_all,
                *, first_blocks, last_blocks):
    n = pl.program_id(1)
    first = functools.reduce(jnp.logical_or, [n == s for s in first_blocks])
    last = functools.reduce(jnp.logical_or, [n == s for s in last_blocks])
    fpen = jnp.where(first, NEG, 0.0).astype(F32)
    lpen = jnp.where(last, NEG, 0.0).astype(F32)
    col = lax.broadcasted_iota(jnp.int32, (1, 3 * RADIUS), 1)
    pen_first = jnp.where(col < RADIUS, fpen, 0.0)
    pen_last = jnp.where(col >= 2 * RADIUS, lpen, 0.0)
    pen_perm_first = jnp.where(lax.rem(col, 48) < 16, fpen, 0.0)
    pen_perm_last = jnp.where(lax.rem(col, 48) >= 32, lpen, 0.0)
    ones = jnp.ones((3 * RADIUS, HEAD_DIM), BF16)

    def keys_d1(refs, b):
        ref_p, ref_c, ref_n = refs
        if b == 0:
            return jnp.concatenate([ref_p[...], ref_c[0:2 * RADIUS]], axis=0)
        if b == TILES - 1:
            return jnp.concatenate([ref_c[DIL_BLOCK - 2 * RADIUS:DIL_BLOCK], ref_n[...]], axis=0)
        return ref_c[RADIUS * (b - 1):RADIUS * (b + 2)]

    def keys_d4(refs, r4, b):
        parts = []
        for a in range(4):
            parts += _view_rows(*refs, 16 * b - 16, 16 * b + 32, _lanes(4 * a + r4))
        return jnp.concatenate(parts, axis=0)

    def keys_d16(refs, r):
        return jnp.concatenate([ref[:, _lanes(r)] for ref in refs], axis=0)

    def head_passes(hh):
        q1_ref, q16_ref = q1_blk.at[hh, 0], q16_blk.at[hh, 0]
        k1 = tuple(blk.at[hh, 0] for blk in (kv1p_blk, kv1c_blk, kv1n_blk))
        v1 = tuple(blk.at[hh, 1] for blk in (kv1p_blk, kv1c_blk, kv1n_blk))
        k16 = tuple(blk.at[hh, 0] for blk in (kv16p_blk, kv16c_blk, kv16n_blk))
        v16 = tuple(blk.at[hh, 1] for blk in (kv16p_blk, kv16c_blk, kv16n_blk))
        s_s, e_s, pos_s = s_all.at[hh], e_all.at[hh], pos_all.at[hh]
        num_s, m_s, l_s, kt_s = num_all.at[hh], m_all_s.at[hh], l_all.at[hh], kt_all.at[hh]

        def tile_operands(p, t):
            if p == 0:
                pen = pen_first if t == 0 else pen_last if t == TILES - 1 else None
                return q1_ref[RADIUS * t:RADIUS * (t + 1)], keys_d1(k1, t), keys_d1(v1, t), pen
            if p == 1:
                r4, b = divmod(t, 4)
                q = jnp.concatenate([q16_ref[16 * b:16 * b + 16, _lanes(4 * a + r4)] for a in range(4)],
                                    axis=0)
                pen = pen_perm_first if b == 0 else pen_perm_last if b == 3 else None
                return q, keys_d4(k16, r4, b), keys_d4(v16, r4, b), pen
            return q16_ref[:, _lanes(t)], keys_d16(k16, t), keys_d16(v16, t), pen_first + pen_last

        def store_rows(p, t, tile, pos_idx, view_ref):
            if p == 0:
                pos_s[pos_idx, RADIUS * t:RADIUS * (t + 1), :] = tile
            elif p == 1:
                r4, b = divmod(t, 4)
                for a in range(4):
                    view_ref[0, 16 * b:16 * b + 16, _lanes(4 * a + r4)] = tile[16 * a:16 * a + 16]
            else:
                view_ref[1, :, _lanes(t)] = tile

        def transpose_keys(p, t):
            kt_s[p, t] = tile_operands(p, t)[1].T

        def logits(p, t):
            q, _, _, pen = tile_operands(p, t)
            s = jnp.dot(q, kt_s[p, t], preferred_element_type=F32)
            s = s + bias_ref[hh, p]
            s_s[p, t] = s if pen is None else s + pen

        def softmax(p, t):
            s = s_s[p, t]
            m = jnp.max(s, axis=-1, keepdims=True)
            e_s[p, t] = jnp.exp(s - m).astype(BF16)
            store_rows(p, t, jnp.broadcast_to(m, (RADIUS, HEAD_DIM)), 1, m_s)

        def values(p, t):
            _, _, v, _ = tile_operands(p, t)
            nv = jnp.dot(e_s[p, t], jnp.concatenate([v, ones], axis=1), preferred_element_type=F32)
            store_rows(p, t, nv[:, :HEAD_DIM], 0, num_s)
            store_rows(p, t, nv[:, HEAD_DIM:], 2, l_s)

        def merge():
            for r in range(DIL_VIEW):
                strided = pl.ds(r, VIEW_ROWS, stride=DIL_VIEW)
                nums = (pos_s[0, strided, :], num_s[0, :, _lanes(r)], num_s[1, :, _lanes(r)])
                ms = (pos_s[1, strided, :], m_s[0, :, _lanes(r)], m_s[1, :, _lanes(r)])
                ls = (pos_s[2, strided, :], l_s[0, :, _lanes(r)], l_s[1, :, _lanes(r)])
                m_max = jnp.maximum(jnp.maximum(ms[0], ms[1]), ms[2])
                ws = [jnp.exp(m - m_max) for m in ms]
                num = ws[0] * nums[0] + ws[1] * nums[1] + ws[2] * nums[2]
                den = ws[0] * ls[0] + ws[1] * ls[1] + ws[2] * ls[2]
                o_ref[hh, strided, :] = num / den

        def all_tiles(p, *tile_stages):
            return lambda: [stage(p, t) for stage in tile_stages for t in range(TILES)]

        return [all_tiles(2, transpose_keys, logits), all_tiles(1, transpose_keys, logits),
                all_tiles(2, softmax), all_tiles(2, values),
                all_tiles(0, transpose_keys, logits), all_tiles(1, softmax), all_tiles(1, values),
                all_tiles(0, softmax), all_tiles(0, values), merge]

    first_head, second_head = head_passes(0), head_passes(1)
    for run in first_head[:7] + second_head[:2] + first_head[7:9] + second_head[2:5] + first_head[9:] + second_head[5:]:
        run()


def _dilated_attention(qkv, qkv16, bias, seq_lens):
    nt = qkv.shape[2]
    nblk = nt // DIL_BLOCK
    bounds = np.cumsum((0,) + seq_lens) // DIL_BLOCK
    first_blocks = tuple(int(b) for b in bounds[:-1])
    last_blocks = tuple(int(b) - 1 for b in bounds[1:])
    per = DIL_BLOCK // RADIUS
    prev1 = lambda n: jnp.maximum(per * n - 1, 0)
    next1 = lambda n: jnp.minimum(per * (n + 1), nt // RADIUS - 1)
    prev16 = lambda n: jnp.maximum(n - 1, 0)
    next16 = lambda n: jnp.minimum(n + 1, nblk - 1)
    kv_blk = KV_BLOCK
    hps = DIL_HEADS_PER_STEP
    blk16 = (hps, 1, VIEW_ROWS, DIL_VIEW * HEAD_DIM)
    kv16 = (hps, 2, VIEW_ROWS, DIL_VIEW * HEAD_DIM)
    npat = len(DILATIONS)
    kernel = functools.partial(_dil_kernel, first_blocks=first_blocks, last_blocks=last_blocks)
    out = pl.pallas_call(
        kernel,
        grid=(N_HEADS_A // hps, nblk),
        in_specs=[pl.BlockSpec((hps, 1, DIL_BLOCK, HEAD_DIM), lambda h, n: (h, SLOT_Q, n, 0)),
                  pl.BlockSpec((hps, 2, RADIUS, HEAD_DIM), lambda h, n: (h, kv_blk, prev1(n), 0)),
                  pl.BlockSpec((hps, 2, DIL_BLOCK, HEAD_DIM), lambda h, n: (h, kv_blk, n, 0)),
                  pl.BlockSpec((hps, 2, RADIUS, HEAD_DIM), lambda h, n: (h, kv_blk, next1(n), 0)),
                  pl.BlockSpec(blk16, lambda h, n: (h, SLOT_Q, n, 0)),
                  pl.BlockSpec(kv16, lambda h, n: (h, kv_blk, prev16(n), 0)),
                  pl.BlockSpec(kv16, lambda h, n: (h, kv_blk, n, 0)),
                  pl.BlockSpec(kv16, lambda h, n: (h, kv_blk, next16(n), 0)),
                  pl.BlockSpec((hps, npat, RADIUS, 3 * RADIUS), lambda h, n: (h, 0, 0, 0))],
        out_specs=pl.BlockSpec((hps, DIL_BLOCK, HEAD_DIM), lambda h, n: (h, n, 0)),
        out_shape=jax.ShapeDtypeStruct((N_HEADS_A, nt, HEAD_DIM), F32),
        scratch_shapes=[pltpu.VMEM((hps, npat, TILES, RADIUS, 3 * RADIUS), F32),
                        pltpu.VMEM((hps, npat, TILES, RADIUS, 3 * RADIUS), BF16),
                        pltpu.VMEM((hps, 3, DIL_BLOCK, HEAD_DIM), F32),
                        pltpu.VMEM((hps, 2, VIEW_ROWS, DIL_VIEW * HEAD_DIM), F32),
                        pltpu.VMEM((hps, 2, VIEW_ROWS, DIL_VIEW * HEAD_DIM), F32),
                        pltpu.VMEM((hps, 2, VIEW_ROWS, DIL_VIEW * HEAD_DIM), F32),
                        pltpu.VMEM((hps, npat, TILES, HEAD_DIM, 3 * RADIUS), BF16)],
        compiler_params=_cparams(("parallel", "parallel"), 48),
        name="dilated_attention",
    )(qkv, qkv, qkv, qkv, qkv16, qkv16, qkv16, qkv16, bias)
    return out


def _na_bias(rpb):
    c = np.arange(GRID_W)
    cstart = np.clip(c - NA_COLS // 2, 0, GRID_W - NA_COLS)
    cmask = (c[None, :] >= cstart[:, None]) & (c[None, :] < cstart[:, None] + NA_COLS)
    coff = np.clip(c[None, :] - c[:, None], -(NA_COLS - 1), NA_COLS - 1) + NA_COLS - 1
    cols = jnp.full(rpb.shape[:2] + (GRID_W, GRID_W), NEG, F32)
    for o in range(2 * NA_COLS - 1):
        cols = jnp.where((cmask & (coff == o))[None, None], rpb[:, :, o, None, None].astype(F32), cols)
    tiles = [cols[:, NA_ROWS - 1 - dd:2 * NA_ROWS - 1 - dd] for dd in range(NA_ROWS)]
    b = jnp.stack(tiles, axis=1).transpose(0, 1, 3, 2, 4)
    return b.reshape(rpb.shape[0], NA_ROWS, GRID_W, NA_ROWS * GRID_W)


def _na_kernel(q_blk, kvp_blk, kvc_blk, kvn_blk, bias_ref, o_ref, kw, vw,
               s_s, e_s, kt_s, *, block_starts, seq_rows):
    n = pl.program_id(1)
    nstart = jnp.int32(block_starts[0])
    rows = jnp.int32(seq_rows[0])
    for s, r in zip(block_starts[1:], seq_rows[1:]):
        nstart = jnp.where(n >= s, s, nstart)
        rows = jnp.where(n >= s, r, rows)

    nkeys = NA_ROWS * GRID_W
    shifts, starts = [], []
    for i in range(NA_BLOCK_ROWS):
        r = (n - nstart) * NA_BLOCK_ROWS + i
        dd = r - jnp.clip(r - NA_ROWS // 2, 0, rows - NA_ROWS)
        shifts.append(dd)
        starts.append(pl.multiple_of((NA_HALO_ROWS + i - dd) * GRID_W, GRID_W))
    ones = jnp.ones((nkeys, HEAD_DIM), BF16)

    def head_passes(hh):
        q_ref = q_blk.at[hh, 0]

        def logits_pass():
            for which, win in enumerate((kw, vw)):
                win[hh, 0:NA_HALO, :] = kvp_blk[hh, which]
                win[hh, NA_HALO:NA_HALO + NA_BLOCK, :] = kvc_blk[hh, which]
                win[hh, NA_HALO + NA_BLOCK:NA_WINDOW, :] = kvn_blk[hh, which]
            kw[hh, NA_WINDOW:, :] = jnp.zeros((GRID_W, HEAD_DIM), BF16)
            kt_s[hh, 0] = kw[hh, 0:NA_WINDOW, :].T
            kt_s[hh, 1] = kw[hh, GRID_W:NA_WINDOW + GRID_W, :].T
            for i in range(NA_BLOCK_ROWS):
                first_row = NA_HALO_ROWS + i - shifts[i]
                odd = lax.rem(first_row, 2)
                col = pl.multiple_of((first_row - odd) * GRID_W, 2 * GRID_W)
                q = q_ref[i * GRID_W:(i + 1) * GRID_W, :]
                s = jnp.dot(q, kt_s[hh, odd, :, pl.ds(col, nkeys)], preferred_element_type=F32)
                s_s[hh, i] = s + bias_ref[hh, shifts[i]]

        def softmax_pass():
            for i in range(NA_BLOCK_ROWS):
                s = s_s[hh, i]
                e_s[hh, i] = jnp.exp(s - jnp.max(s, axis=-1, keepdims=True)).astype(BF16)

        def values_pass():
            for i in range(NA_BLOCK_ROWS):
                v = vw[hh, pl.ds(starts[i], nkeys), :]
                nv = jnp.dot(e_s[hh, i], jnp.concatenate([v, ones], axis=1), preferred_element_type=F32)
                out = (nv[:, :HEAD_DIM] / nv[:, HEAD_DIM:]).astype(o_ref.dtype)
                o_ref[i * GRID_W:(i + 1) * GRID_W, _lanes(hh)] = out

        return [logits_pass, softmax_pass, values_pass]

    first_head, second_head = head_passes(0), head_passes(1)
    for run in (first_head[0], second_head[0], first_head[1], first_head[2], second_head[1], second_head[2]):
        run()


def _neighborhood_attention(qkv, bias, seq_lens):
    nt = qkv.shape[2]
    block_starts = tuple(int(s) // NA_BLOCK for s in np.cumsum((0,) + seq_lens[:-1]))
    seq_rows = tuple(int(s) // GRID_W for s in seq_lens)
    assert all(r >= NA_ROWS and r % NA_BLOCK_ROWS == 0 for r in seq_rows)
    per = NA_BLOCK // NA_HALO
    prev = lambda n: jnp.maximum(per * n - 1, 0)
    nxt = lambda n: jnp.minimum(per * (n + 1), nt // NA_HALO - 1)
    kv_blk = KV_BLOCK
    kernel = functools.partial(_na_kernel, block_starts=block_starts, seq_rows=seq_rows)
    nkeys = NA_ROWS * GRID_W
    hps = NA_HEADS_PER_STEP
    return pl.pallas_call(
        kernel,
        grid=(N_HEADS_B // hps, nt // NA_BLOCK),
        in_specs=[pl.BlockSpec((hps, 1, NA_BLOCK, HEAD_DIM), lambda h, n: (h, SLOT_Q, n, 0)),
                  pl.BlockSpec((hps, 2, NA_HALO, HEAD_DIM), lambda h, n: (h, kv_blk, prev(n), 0)),
                  pl.BlockSpec((hps, 2, NA_BLOCK, HEAD_DIM), lambda h, n: (h, kv_blk, n, 0)),
                  pl.BlockSpec((hps, 2, NA_HALO, HEAD_DIM), lambda h, n: (h, kv_blk, nxt(n), 0)),
                  pl.BlockSpec((hps, NA_ROWS, GRID_W, nkeys), lambda h, n: (h, 0, 0, 0))],
        out_specs=pl.BlockSpec((NA_BLOCK, hps * HEAD_DIM), lambda h, n: (n, h)),
        out_shape=jax.ShapeDtypeStruct((nt, D_B), BF16),
        scratch_shapes=[pltpu.VMEM((hps, NA_WINDOW + GRID_W, HEAD_DIM), BF16),
                        pltpu.VMEM((hps, NA_WINDOW, HEAD_DIM), BF16),
                        pltpu.VMEM((hps, NA_BLOCK_ROWS, GRID_W, nkeys), F32),
                        pltpu.VMEM((hps, NA_BLOCK_ROWS, GRID_W, nkeys), BF16),
                        pltpu.VMEM((hps, 2, HEAD_DIM, NA_WINDOW), BF16)],
        compiler_params=_cparams(("parallel", "parallel"), 40),
        name="neighborhood_attention",
    )(qkv, qkv, qkv, qkv, bias)


def _plain_norm(y, g):
    y = y.astype(F32)
    return (y * lax.rsqrt(jnp.mean(y * y, axis=-1, keepdims=True) + EPS) * g).astype(BF16)


OUT_ROW_CHUNK = 256


def _out_kernel(ya_ref, yb_ref, ga_ref, gb_ref, w_ref, x_ref, mod_ref, g2_ref, xo_ref, h_ref):
    for r0 in range(0, x_ref.shape[0], OUT_ROW_CHUNK):
        rows = slice(r0, r0 + OUT_ROW_CHUNK)
        ya = jnp.concatenate([ya_ref[h, rows, :] for h in range(N_HEADS_A)], axis=1)
        y = jnp.dot(_plain_norm(ya, ga_ref[...]), w_ref[:D_A, :], preferred_element_type=F32)
        y = y + jnp.dot(_plain_norm(yb_ref[rows, :], gb_ref[...]), w_ref[D_A:, :],
                        preferred_element_type=F32)
        x = x_ref[rows, :] + mod_ref[0, 2:3, :] * y
        xo_ref[rows, :] = x
        h = _modulated_norm(x, g2_ref[...], mod_ref[0, 3:4, :], mod_ref[0, 4:5, :])
        h_ref[rows, :] = h.astype(BF16)


def _out_proj(ya, yb, ga, gb, w, layer, x, mod_l, g2, seq_lens):
    nt, d = x.shape
    tm = 512
    starts = tuple(int(s) // tm for s in np.cumsum((0,) + seq_lens[:-1]))
    const = lambda i: (0, 0)
    return pl.pallas_call(
        _out_kernel,
        grid=(nt // tm,),
        in_specs=[pl.BlockSpec((N_HEADS_A, tm, HEAD_DIM), lambda i: (0, i, 0)),
                  pl.BlockSpec((tm, D_B), lambda i: (i, 0)),
                  pl.BlockSpec((1, D_A), const),
                  pl.BlockSpec((1, D_B), const),
                  pl.BlockSpec((None, D_A + D_B, d), lambda i: (layer, 0, 0), pipeline_mode=pl.Buffered(1)),
                  pl.BlockSpec((tm, d), lambda i: (i, 0)),
                  pl.BlockSpec((1, 6, d), lambda i: (_seq_index(i, starts), 0, 0)),
                  pl.BlockSpec((1, d), const)],
        out_specs=[pl.BlockSpec((tm, d), lambda i: (i, 0)),
                   pl.BlockSpec((tm, d), lambda i: (i, 0))],
        out_shape=[jax.ShapeDtypeStruct((nt, d), F32), jax.ShapeDtypeStruct((nt, d), BF16)],
        compiler_params=_cparams(("parallel",), 48),
        name="out_proj",
    )(ya, yb, ga, gb, w, x, mod_l, g2)


def _ffn_kernel(h_ref, wg_ref, wu_ref, wd_ref, x_ref, mod_ref, ng_ref, *refs, split_block):
    acc_ref = refs[-1]
    f = pl.program_id(1)

    @pl.when(f == 0)
    def _():
        acc_ref[...] = jnp.zeros_like(acc_ref)

    h = h_ref[...]
    a = jnp.dot(h, wg_ref[...], preferred_element_type=F32)
    b = jnp.dot(h, wu_ref[...], preferred_element_type=F32)
    act = (a / (1.0 + jnp.exp(-a)) * b).astype(BF16)
    acc_ref[...] += jnp.dot(act, wd_ref[...], preferred_element_type=F32)

    @pl.when(f == pl.num_programs(1) - 1)
    def _():
        x = x_ref[...] + mod_ref[0, 5:6, :] * acc_ref[...]
        if split_block is not None:
            y0_ref, y1_ref = refs[:-1]
            y = x * lax.rsqrt(jnp.mean(x * x, axis=-1, keepdims=True) + EPS) * ng_ref[...]
            i = pl.program_id(0)

            @pl.when(i < split_block)
            def _():
                y0_ref[...] = y

            @pl.when(i >= split_block)
            def _():
                y1_ref[...] = y
        else:
            nmod_ref, xo_ref, hn_ref = refs[:-1]
            xo_ref[...] = x
            hn = _modulated_norm(x, ng_ref[...], nmod_ref[0, 0:1, :], nmod_ref[0, 1:2, :])
            hn_ref[...] = hn.astype(BF16)


def _ffn(h, wg, wu, wd, layer, x, mod_l, next_g, next_mod, seq_lens, out_split=None):
    nt, d = x.shape
    dff = wg.shape[2]
    tm, tf = 512, 512
    final = next_mod is None
    split_block = out_split // tm if final else None
    starts = tuple(int(s) // tm for s in np.cumsum((0,) + seq_lens[:-1]))
    row_blk = pl.BlockSpec((tm, d), lambda i, f: (i, 0))
    mod_blk = pl.BlockSpec((1, 6, d), lambda i, f: (_seq_index(i, starts), 0, 0))
    in_specs = [row_blk,
                pl.BlockSpec((None, d, tf), lambda i, f: (layer, 0, f)),
                pl.BlockSpec((None, d, tf), lambda i, f: (layer, 0, f)),
                pl.BlockSpec((None, tf, d), lambda i, f: (layer, f, 0)),
                row_blk,
                mod_blk,
                pl.BlockSpec((1, d), lambda i, f: (0, 0))]
    operands = [h, wg, wu, wd, x, mod_l, next_g]
    if final:
        out_specs = [pl.BlockSpec((tm, d), lambda i, f: (jnp.minimum(i, split_block - 1), 0)),
                     pl.BlockSpec((tm, d), lambda i, f: (jnp.maximum(i - split_block, 0), 0))]
        out_shape = [jax.ShapeDtypeStruct((out_split, d), F32),
                     jax.ShapeDtypeStruct((nt - out_split, d), F32)]
    else:
        in_specs.append(mod_blk)
        operands.append(next_mod)
        out_specs = [row_blk, row_blk]
        out_shape = [jax.ShapeDtypeStruct((nt, d), F32), jax.ShapeDtypeStruct((nt, d), BF16)]
    return pl.pallas_call(
        functools.partial(_ffn_kernel, split_block=split_block),
        grid=(nt // tm, dff // tf),
        in_specs=in_specs,
        out_specs=out_specs,
        out_shape=out_shape,
        scratch_shapes=[pltpu.VMEM((tm, d), F32)],
        compiler_params=_cparams(("arbitrary" if final else "parallel", "arbitrary"), 56),
        name="ffn_final" if final else "ffn",
    )(*operands)


def _trunk(x, c, seq_lens, out_split, t5_table, norm1_g, norm2_g, w_ada, b_ada, w_in, out_norm_a,
           out_norm_b, w_out, na_rpb, w_gate, w_up, w_down, final_g):
    depth = w_in.shape[0]
    nseq = len(seq_lens)
    d = x.shape[1]
    assert all(s % DIL_BLOCK == 0 for s in seq_lens)

    c_pad = jnp.zeros((8, d), F32).at[:nseq].set(c)
    mod = _ada_modulation(c_pad, w_ada, b_ada)[:, :nseq].reshape(depth, nseq, 6, d)

    col_scale = jnp.ones((6, D_A), F32).at[0].set(QK_SCALE).at[3].set(QK_SCALE).reshape(1, 6 * D_A)
    dil_bias = _dilated_bias(t5_table)
    w_in, w_out, w_gate, w_up, w_down = (w.astype(BF16) for w in (w_in, w_out, w_gate, w_up, w_down))
    h = _prenorm(x, mod[0], norm1_g[0][None], seq_lens)
    for l in range(depth):
        final = l == depth - 1
        qkv_a, qkv_a16 = _qkv_proj(h, col_scale, w_in, l, 0, True)
        qkv_b, = _qkv_proj(h, col_scale, w_in, l, 1, False)
        ya = _dilated_attention(qkv_a, qkv_a16, dil_bias, seq_lens)
        yb = _neighborhood_attention(qkv_b, _na_bias(na_rpb[l]), seq_lens)
        x, h2 = _out_proj(ya, yb, out_norm_a[l][None], out_norm_b[l][None], w_out, l,
                          x, mod[l], norm2_g[l][None], seq_lens)
        next_g = final_g[None] if final else norm1_g[l + 1][None]
        next_mod = None if final else mod[l + 1]
        out = _ffn(h2, w_gate, w_up, w_down, l, x, mod[l], next_g, next_mod, seq_lens, out_split)
        if not final:
            x, h = out
    return out


def kernel(x_prompt, x_sample, c_prompt, c_sample, t5_table, norm1_g, norm2_g, w_ada, b_ada, w_in,
           out_norm_a, out_norm_b, w_out, na_rpb, w_gate, w_up, w_down, final_g):
    bp, tp, d = x_prompt.shape
    bs, ts, _ = x_sample.shape
    seq_lens = (tp,) * bp + (ts,) * bs
    x = jnp.concatenate([x_prompt.reshape(bp * tp, d), x_sample.reshape(bs * ts, d)], axis=0)
    c = jnp.concatenate([c_prompt, c_sample], axis=0)
    y_prompt, y_sample = _trunk(x, c, seq_lens, bp * tp, t5_table, norm1_g, norm2_g, w_ada, b_ada, w_in,
                                out_norm_a, out_norm_b, w_out, na_rpb, w_gate, w_up, w_down, final_g)
    return (y_prompt.reshape(bp, tp, d), y_sample.reshape(bs, ts, d))
```

```python
import functools
import math

import numpy as np
import jax
import jax.numpy as jnp
from jax import lax
from jax.experimental import pallas as pl
from jax.experimental.pallas import tpu as pltpu

D_MODEL = 2048
HEAD_DIM = 128
N_HEADS_A = 8
N_HEADS_B = 8
D_A = N_HEADS_A * HEAD_DIM
D_B = N_HEADS_B * HEAD_DIM
QK_SCALE = HEAD_DIM ** -0.5
DILATIONS = (1, 4, 16)
RADIUS = 64
N_BUCKETS = 32
MAX_DISTANCE = 1024
GRID_W = 64
NA_ROWS = 8
NA_COLS = 16
EPS = 1e-6
NEG = -1e30

DIL_BLOCK = 1024
DIL_VIEW = 16
VIEW_ROWS = DIL_BLOCK // DIL_VIEW
TILES = DIL_BLOCK // RADIUS
DIL_HEADS_PER_STEP = 2
NA_BLOCK_ROWS = 16
NA_HALO_ROWS = 8
NA_BLOCK = NA_BLOCK_ROWS * GRID_W
NA_HEADS_PER_STEP = 2
NA_HALO = NA_HALO_ROWS * GRID_W
NA_WINDOW = NA_BLOCK + 2 * NA_HALO

F32 = jnp.float32
BF16 = jnp.bfloat16
MIB = 1024 * 1024


def _cparams(sem, vmem_mib):
    return pltpu.CompilerParams(dimension_semantics=sem, vmem_limit_bytes=vmem_mib * MIB)


def _seq_index(i, starts):
    idx = 0
    for s in starts[1:]:
        idx = idx + (i >= s).astype(jnp.int32)
    return idx


def _lanes(r):
    return slice(r * HEAD_DIM, (r + 1) * HEAD_DIM)


def _ada_kernel(c_ref, w_ref, b_ref, o_ref):
    c = c_ref[...]
    s = c / (1.0 + jnp.exp(-c))
    o_ref[0] = jnp.dot(s, w_ref[0], precision=lax.Precision.HIGHEST,
                       preferred_element_type=F32) + b_ref[0]


def _ada_modulation(c_pad, w_ada, b_ada):
    depth, d, n = w_ada.shape
    tn = 1024
    return pl.pallas_call(
        _ada_kernel,
        grid=(depth, n // tn),
        in_specs=[pl.BlockSpec((8, d), lambda l, j: (0, 0)),
                  pl.BlockSpec((1, d, tn), lambda l, j: (l, 0, j)),
                  pl.BlockSpec((1, 1, tn), lambda l, j: (l, 0, j))],
        out_specs=pl.BlockSpec((1, 8, tn), lambda l, j: (l, 0, j)),
        out_shape=jax.ShapeDtypeStruct((depth, 8, n), F32),
        compiler_params=_cparams(("parallel", "parallel"), 40),
        name="ada_modulation",
    )(c_pad, w_ada, b_ada.reshape(depth, 1, n))


def _modulated_norm(x, g, shift, scale):
    y = x * lax.rsqrt(jnp.mean(x * x, axis=-1, keepdims=True) + EPS) * g
    return y * (1.0 + scale) + shift


def _prenorm_kernel(xa_ref, xb_ref, mod_ref, g_ref, x_ref, h_ref, *, split_block):
    x = jnp.where(pl.program_id(0) < split_block, xa_ref[...], xb_ref[...])
    x_ref[...] = x
    h = _modulated_norm(x, g_ref[...], mod_ref[0, 0:1, :], mod_ref[0, 1:2, :])
    h_ref[...] = h.astype(BF16)


def _prenorm(xa, xb, mod_l, g, seq_lens):
    d = xa.shape[1]
    nt = xa.shape[0] + xb.shape[0]
    tm = 512
    split_block = xa.shape[0] // tm
    starts = tuple(int(s) // tm for s in np.cumsum((0,) + seq_lens[:-1]))
    row_blk = pl.BlockSpec((tm, d), lambda i: (i, 0))
    return pl.pallas_call(
        functools.partial(_prenorm_kernel, split_block=split_block),
        grid=(nt // tm,),
        in_specs=[pl.BlockSpec((tm, d), lambda i: (jnp.minimum(i, split_block - 1), 0)),
                  pl.BlockSpec((tm, d), lambda i: (jnp.maximum(i - split_block, 0), 0)),
                  pl.BlockSpec((1, 6, d), lambda i: (_seq_index(i, starts), 0, 0)),
                  pl.BlockSpec((1, d), lambda i: (0, 0))],
        out_specs=[row_blk, row_blk],
        out_shape=[jax.ShapeDtypeStruct((nt, d), F32), jax.ShapeDtypeStruct((nt, d), BF16)],
        compiler_params=_cparams(("parallel",), 40),
        name="prenorm",
    )(xa, xb, mod_l, g)


SLOT_K, SLOT_V, SLOT_Q = range(3)
KV_BLOCK = SLOT_K // 2


def _qkv_kernel(h_ref, cs_ref, w_ref, o_ref, *view, tm):
    r = jnp.dot(h_ref[...], w_ref[...], preferred_element_type=F32) * cs_ref[...]
    for c in range(o_ref.shape[0]):
        o_ref[c, 0] = r[:, _lanes(c)].astype(BF16)
    if view:
        ov_ref, r_s = view
        for c in range(o_ref.shape[0]):
            r_s[c] = r[:, _lanes(c)]
            for g in range(DIL_VIEW):
                rows = r_s[c, pl.ds(g, tm // DIL_VIEW, stride=DIL_VIEW), :]
                ov_ref[c, 0, :, _lanes(g)] = rows.astype(BF16)


def _qkv_proj(h, col_scale, w, layer, group, with_view):
    nt, d = h.shape
    tm = 1024
    width = N_HEADS_A * HEAD_DIM
    assert w.shape[2] == 6 * width and N_HEADS_A == N_HEADS_B
    slot = lambda j: lax.rem(j + 2, 3)
    out_specs = [pl.BlockSpec((N_HEADS_A, 1, tm, HEAD_DIM), lambda i, j: (0, slot(j), i, 0))]
    out_shape = [jax.ShapeDtypeStruct((N_HEADS_A, 3, nt, HEAD_DIM), BF16)]
    scratch = []
    if with_view:
        out_specs.append(pl.BlockSpec((N_HEADS_A, 1, tm // DIL_VIEW, DIL_VIEW * HEAD_DIM),
                                      lambda i, j: (0, slot(j), i, 0)))
        out_shape.append(jax.ShapeDtypeStruct((N_HEADS_A, 3, nt // DIL_VIEW, DIL_VIEW * HEAD_DIM), BF16))
        scratch.append(pltpu.VMEM((N_HEADS_A, tm, HEAD_DIM), F32))
    return pl.pallas_call(
        functools.partial(_qkv_kernel, tm=tm),
        grid=(nt // tm, 3),
        in_specs=[pl.BlockSpec((tm, d), lambda i, j: (i, 0)),
                  pl.BlockSpec((1, width), lambda i, j: (0, 3 * group + j)),
                  pl.BlockSpec((None, d, width), lambda i, j: (layer, 0, 3 * group + j))],
        out_specs=out_specs,
        out_shape=out_shape,
        scratch_shapes=scratch,
        compiler_params=_cparams(("parallel", "parallel"), 48),
        name="qkv_proj_view" if with_view else "qkv_proj",
    )(h, col_scale, w)


def _t5_bucket(rel):
    nb = N_BUCKETS // 2
    exact = nb // 2
    n = np.abs(rel)
    sign = np.where(rel > 0, nb, 0)
    large = exact + (np.log(np.maximum(n, 1) / exact) / math.log(MAX_DISTANCE / exact) * (nb - exact)).astype(np.int64)
    large = np.minimum(large, nb - 1)
    return (sign + np.where(n < exact, n, large)).astype(np.int32)


def _tile_offsets():
    i = np.arange(RADIUS)[:, None]
    j = np.arange(3 * RADIUS)[None, :]
    band = j - RADIUS - i
    qa, qi = i // 16, i % 16
    ka, kj = j // 48, j % 48
    perm = 4 * (kj - 16 - qi) + (ka - qa)
    return (band, perm, band)


def _dilated_bias(t5_table):
    table = t5_table.astype(F32).T
    tiles = []
    for dil, rel in zip(DILATIONS, _tile_offsets()):
        inside = np.abs(rel) <= RADIUS
        bucket = _t5_bucket(rel * dil)
        b = jnp.full((table.shape[0],) + rel.shape, NEG, F32)
        for o in range(N_BUCKETS):
            b = jnp.where((inside & (bucket == o))[None], table[:, o, None, None], b)
        tiles.append(b)
    return jnp.stack(tiles, axis=1)


def _view_rows(ref_p, ref_c, ref_n, lo, hi, lanes):
    parts = []
    if lo < 0:
        parts.append(ref_p[VIEW_ROWS + lo:VIEW_ROWS, lanes])
    parts.append(ref_c[max(lo, 0):min(hi, VIEW_ROWS), lanes])
    if hi > VIEW_ROWS:
        parts.append(ref_n[0:hi - VIEW_ROWS, lanes])
    return parts


def _dil_kernel(q1_blk, kv1p_blk, kv1c_blk, kv1n_blk, q16_blk, kv16p_blk, kv16c_blk, kv16n_blk,
                bias_ref, o_ref, s_all, e_all, pos_all, num_all, m_all_s, l_all, kt_all,
                *, first_blocks, last_blocks):
    n = pl.program_id(1)
    first = functools.reduce(jnp.logical_or, [n == s for s in first_blocks])
    last = functools.reduce(jnp.logical_or, [n == s for s in last_blocks])
    fpen = jnp.where(first, NEG, 0.0).astype(F32)
    lpen = jnp.where(last, NEG, 0.0).astype(F32)
    col = lax.broadcasted_iota(jnp.int32, (1, 3 * RADIUS), 1)
    pen_first = jnp.where(col < RADIUS, fpen, 0.0)
    pen_last = jnp.where(col >= 2 * RADIUS, lpen, 0.0)
    pen_perm_first = jnp.where(lax.rem(col, 48) < 16, fpen, 0.0)
    pen_perm_last = jnp.where(lax.rem(col, 48) >= 32, lpen, 0.0)
    ones = jnp.ones((3 * RADIUS, HEAD_DIM), BF16)

    def keys_d1(refs, b):
        ref_p, ref_c, ref_n = refs
        if b == 0:
            return jnp.concatenate([ref_p[...], ref_c[0:2 * RADIUS]], axis=0)
        if b == TILES - 1:
            return jnp.concatenate([ref_c[DIL_BLOCK - 2 * RADIUS:DIL_BLOCK], ref_n[...]], axis=0)
        return ref_c[RADIUS * (b - 1):RADIUS * (b + 2)]

    def keys_d4(refs, r4, b):
        parts = []
        for a in range(4):
            parts += _view_rows(*refs, 16 * b - 16, 16 * b + 32, _lanes(4 * a + r4))
        return jnp.concatenate(parts, axis=0)

    def keys_d16(refs, r):
        return jnp.concatenate([ref[:, _lanes(r)] for ref in refs], axis=0)

    def head_passes(hh):
        q1_ref, q16_ref = q1_blk.at[hh, 0], q16_blk.at[hh, 0]
        k1 = tuple(blk.at[hh, 0] for blk in (kv1p_blk, kv1c_blk, kv1n_blk))
        v1 = tuple(blk.at[hh, 1] for blk in (kv1p_blk, kv1c_blk, kv1n_blk))
        k16 = tuple(blk.at[hh, 0] for blk in (kv16p_blk, kv16c_blk, kv16n_blk))
        v16 = tuple(blk.at[hh, 1] for blk in (kv16p_blk, kv16c_blk, kv16n_blk))
        s_s, e_s, pos_s = s_all.at[hh], e_all.at[hh], pos_all.at[hh]
        num_s, m_s, l_s, kt_s = num_all.at[hh], m_all_s.at[hh], l_all.at[hh], kt_all.at[hh]

        def tile_operands(p, t):
            if p == 0:
                pen = pen_first if t == 0 else pen_last if t == TILES - 1 else None
                return q1_ref[RADIUS * t:RADIUS * (t + 1)], keys_d1(k1, t), keys_d1(v1, t), pen
            if p == 1:
                r4, b = divmod(t, 4)
                q = jnp.concatenate([q16_ref[16 * b:16 * b + 16, _lanes(4 * a + r4)] for a in range(4)],
                                    axis=0)
                pen = pen_perm_first if b == 0 else pen_perm_last if b == 3 else None
                return q, keys_d4(k16, r4, b), keys_d4(v16, r4, b), pen
            return q16_ref[:, _lanes(t)], keys_d16(k16, t), keys_d16(v16, t), pen_first + pen_last

        def store_rows(p, t, tile, pos_idx, view_ref):
            if p == 0:
                pos_s[pos_idx, RADIUS * t:RADIUS * (t + 1), :] = tile
            elif p == 1:
                r4, b = divmod(t, 4)
                for a in range(4):
                    view_ref[0, 16 * b:16 * b + 16, _lanes(4 * a + r4)] = tile[16 * a:16 * a + 16]
            else:
                view_ref[1, :, _lanes(t)] = tile

        def transpose_keys(p, t):
            kt_s[p, t] = tile_operands(p, t)[1].T

        def logits(p, t):
            q, _, _, pen = tile_operands(p, t)
            s = jnp.dot(q, kt_s[p, t], preferred_element_type=F32)
            s = s + bias_ref[hh, p]
            s_s[p, t] = s if pen is None else s + pen

        def softmax(p, t):
            s = s_s[p, t]
            m = jnp.max(s, axis=-1, keepdims=True)
            e_s[p, t] = jnp.exp(s - m).astype(BF16)
            store_rows(p, t, jnp.broadcast_to(m, (RADIUS, HEAD_DIM)), 1, m_s)

        def values(p, t):
            _, _, v, _ = tile_operands(p, t)
            nv = jnp.dot(e_s[p, t], jnp.concatenate([v, ones], axis=1), preferred_element_type=F32)
            store_rows(p, t, nv[:, :HEAD_DIM], 0, num_s)
            store_rows(p, t, nv[:, HEAD_DIM:], 2, l_s)

        def merge():
            for r in range(DIL_VIEW):
                strided = pl.ds(r, VIEW_ROWS, stride=DIL_VIEW)
                nums = (pos_s[0, strided, :], num_s[0, :, _lanes(r)], num_s[1, :, _lanes(r)])
                ms = (pos_s[1, strided, :], m_s[0, :, _lanes(r)], m_s[1, :, _lanes(r)])
                ls = (pos_s[2, strided, :], l_s[0, :, _lanes(r)], l_s[1, :, _lanes(r)])
                m_max = jnp.maximum(jnp.maximum(ms[0], ms[1]), ms[2])
                ws = [jnp.exp(m - m_max) for m in ms]
                num = ws[0] * nums[0] + ws[1] * nums[1] + ws[2] * nums[2]
                den = ws[0] * ls[0] + ws[1] * ls[1] + ws[2] * ls[2]
                o_ref[hh, strided, :] = num / den

        def all_tiles(p, *tile_stages):
            return lambda: [stage(p, t) for stage in tile_stages for t in range(TILES)]

        return [all_tiles(2, transpose_keys, logits), all_tiles(1, transpose_keys, logits),
                all_tiles(2, softmax), all_tiles(2, values),
                all_tiles(0, transpose_keys, logits), all_tiles(1, softmax), all_tiles(1, values),
                all_tiles(0, softmax), all_tiles(0, values), merge]

    first_head, second_head = head_passes(0), head_passes(1)
    for run in first_head[:7] + second_head[:2] + first_head[7:9] + second_head[2:5] + first_head[9:] + second_head[5:]:
        run()


def _dilated_attention(qkv, qkv16, bias, seq_lens):
    nt = qkv.shape[2]
    nblk = nt // DIL_BLOCK
    bounds = np.cumsum((0,) + seq_lens) // DIL_BLOCK
    first_blocks = tuple(int(b) for b in bounds[:-1])
    last_blocks = tuple(int(b) - 1 for b in bounds[1:])
    per = DIL_BLOCK // RADIUS
    prev1 = lambda n: jnp.maximum(per * n - 1, 0)
    next1 = lambda n: jnp.minimum(per * (n + 1), nt // RADIUS - 1)
    prev16 = lambda n: jnp.maximum(n - 1, 0)
    next16 = lambda n: jnp.minimum(n + 1, nblk - 1)
    kv_blk = KV_BLOCK
    hps = DIL_HEADS_PER_STEP
    blk16 = (hps, 1, VIEW_ROWS, DIL_VIEW * HEAD_DIM)
    kv16 = (hps, 2, VIEW_ROWS, DIL_VIEW * HEAD_DIM)
    npat = len(DILATIONS)
    kernel = functools.partial(_dil_kernel, first_blocks=first_blocks, last_blocks=last_blocks)
    out = pl.pallas_call(
        kernel,
        grid=(N_HEADS_A // hps, nblk),
        in_specs=[pl.BlockSpec((hps, 1, DIL_BLOCK, HEAD_DIM), lambda h, n: (h, SLOT_Q, n, 0)),
                  pl.BlockSpec((hps, 2, RADIUS, HEAD_DIM), lambda h, n: (h, kv_blk, prev1(n), 0)),
                  pl.BlockSpec((hps, 2, DIL_BLOCK, HEAD_DIM), lambda h, n: (h, kv_blk, n, 0)),
                  pl.BlockSpec((hps, 2, RADIUS, HEAD_DIM), lambda h, n: (h, kv_blk, next1(n), 0)),
                  pl.BlockSpec(blk16, lambda h, n: (h, SLOT_Q, n, 0)),
                  pl.BlockSpec(kv16, lambda h, n: (h, kv_blk, prev16(n), 0)),
                  pl.BlockSpec(kv16, lambda h, n: (h, kv_blk, n, 0)),
                  pl.BlockSpec(kv16, lambda h, n: (h, kv_blk, next16(n), 0)),
                  pl.BlockSpec((hps, npat, RADIUS, 3 * RADIUS), lambda h, n: (h, 0, 0, 0))],
        out_specs=pl.BlockSpec((hps, DIL_BLOCK, HEAD_DIM), lambda h, n: (h, n, 0)),
        out_shape=jax.ShapeDtypeStruct((N_HEADS_A, nt, HEAD_DIM), F32),
        scratch_shapes=[pltpu.VMEM((hps, npat, TILES, RADIUS, 3 * RADIUS), F32),
                        pltpu.VMEM((hps, npat, TILES, RADIUS, 3 * RADIUS), BF16),
                        pltpu.VMEM((hps, 3, DIL_BLOCK, HEAD_DIM), F32),
                        pltpu.VMEM((hps, 2, VIEW_ROWS, DIL_VIEW * HEAD_DIM), F32),
                        pltpu.VMEM((hps, 2, VIEW_ROWS, DIL_VIEW * HEAD_DIM), F32),
                        pltpu.VMEM((hps, 2, VIEW_ROWS, DIL_VIEW * HEAD_DIM), F32),
                        pltpu.VMEM((hps, npat, TILES, HEAD_DIM, 3 * RADIUS), BF16)],
        compiler_params=_cparams(("parallel", "parallel"), 48),
        name="dilated_attention",
    )(qkv, qkv, qkv, qkv, qkv16, qkv16, qkv16, qkv16, bias)
    return out


def _na_bias(rpb):
    c = np.arange(GRID_W)
    cstart = np.clip(c - NA_COLS // 2, 0, GRID_W - NA_COLS)
    cmask = (c[None, :] >= cstart[:, None]) & (c[None, :] < cstart[:, None] + NA_COLS)
    coff = np.clip(c[None, :] - c[:, None], -(NA_COLS - 1), NA_COLS - 1) + NA_COLS - 1
    cols = jnp.full(rpb.shape[:2] + (GRID_W, GRID_W), NEG, F32)
    for o in range(2 * NA_COLS - 1):
        cols = jnp.where((cmask & (coff == o))[None, None], rpb[:, :, o, None, None].astype(F32), cols)
    tiles = [cols[:, NA_ROWS - 1 - dd:2 * NA_ROWS - 1 - dd] for dd in range(NA_ROWS)]
    b = jnp.stack(tiles, axis=1).transpose(0, 1, 3, 2, 4)
    return b.reshape(rpb.shape[0], NA_ROWS, GRID_W, NA_ROWS * GRID_W)


def _na_kernel(q_blk, kvp_blk, kvc_blk, kvn_blk, bias_ref, o_ref, kw, vw,
               s_s, e_s, *, block_starts, seq_rows):
    n = pl.program_id(1)
    nstart = jnp.int32(block_starts[0])
    rows = jnp.int32(seq_rows[0])
    for s, r in zip(block_starts[1:], seq_rows[1:]):
        nstart = jnp.where(n >= s, s, nstart)
        rows = jnp.where(n >= s, r, rows)

    nkeys = NA_ROWS * GRID_W
    shifts, starts = [], []
    for i in range(NA_BLOCK_ROWS):
        r = (n - nstart) * NA_BLOCK_ROWS + i
        dd = r - jnp.clip(r - NA_ROWS // 2, 0, rows - NA_ROWS)
        shifts.append(dd)
        starts.append(pl.multiple_of((NA_HALO_ROWS + i - dd) * GRID_W, GRID_W))
    ones = jnp.ones((nkeys, HEAD_DIM), BF16)

    def head_passes(hh):
        q_ref = q_blk.at[hh, 0]

        def logits_pass():
            for which, win in enumerate((kw, vw)):
                win[hh, 0:NA_HALO, :] = kvp_blk[hh, which]
                win[hh, NA_HALO:NA_HALO + NA_BLOCK, :] = kvc_blk[hh, which]
                win[hh, NA_HALO + NA_BLOCK:NA_WINDOW, :] = kvn_blk[hh, which]
            for i in range(NA_BLOCK_ROWS):
                k = kw[hh, pl.ds(starts[i], nkeys), :]
                q = q_ref[i * GRID_W:(i + 1) * GRID_W, :]
                s = lax.dot_general(q, k, (((1,), (1,)), ((), ())), preferred_element_type=F32)
                s_s[hh, i] = s + bias_ref[hh, shifts[i]]

        def softmax_pass():
            for i in range(NA_BLOCK_ROWS):
                s = s_s[hh, i]
                e_s[hh, i] = jnp.exp(s - jnp.max(s, axis=-1, keepdims=True)).astype(BF16)

        def values_pass():
            for i in range(NA_BLOCK_ROWS):
                v = vw[hh, pl.ds(starts[i], nkeys), :]
                nv = jnp.dot(e_s[hh, i], jnp.concatenate([v, ones], axis=1), preferred_element_type=F32)
                out = (nv[:, :HEAD_DIM] / nv[:, HEAD_DIM:]).astype(o_ref.dtype)
                o_ref[i * GRID_W:(i + 1) * GRID_W, _lanes(hh)] = out

        return [logits_pass, softmax_pass, values_pass]

    first_head, second_head = head_passes(0), head_passes(1)
    for run in (first_head[0], second_head[0], first_head[1], first_head[2], second_head[1], second_head[2]):
        run()


def _neighborhood_attention(qkv, bias, seq_lens):
    nt = qkv.shape[2]
    block_starts = tuple(int(s) // NA_BLOCK for s in np.cumsum((0,) + seq_lens[:-1]))
    seq_rows = tuple(int(s) // GRID_W for s in seq_lens)
    assert all(r >= NA_ROWS and r % NA_BLOCK_ROWS == 0 for r in seq_rows)
    per = NA_BLOCK // NA_HALO
    prev = lambda n: jnp.maximum(per * n - 1, 0)
    nxt = lambda n: jnp.minimum(per * (n + 1), nt // NA_HALO - 1)
    kv_blk = KV_BLOCK
    kernel = functools.partial(_na_kernel, block_starts=block_starts, seq_rows=seq_rows)
    nkeys = NA_ROWS * GRID_W
    hps = NA_HEADS_PER_STEP
    return pl.pallas_call(
        kernel,
        grid=(N_HEADS_B // hps, nt // NA_BLOCK),
        in_specs=[pl.BlockSpec((hps, 1, NA_BLOCK, HEAD_DIM), lambda h, n: (h, SLOT_Q, n, 0)),
                  pl.BlockSpec((hps, 2, NA_HALO, HEAD_DIM), lambda h, n: (h, kv_blk, prev(n), 0)),
                  pl.BlockSpec((hps, 2, NA_BLOCK, HEAD_DIM), lambda h, n: (h, kv_blk, n, 0)),
                  pl.BlockSpec((hps, 2, NA_HALO, HEAD_DIM), lambda h, n: (h, kv_blk, nxt(n), 0)),
                  pl.BlockSpec((hps, NA_ROWS, GRID_W, nkeys), lambda h, n: (h, 0, 0, 0))],
        out_specs=pl.BlockSpec((NA_BLOCK, hps * HEAD_DIM), lambda h, n: (n, h)),
        out_shape=jax.ShapeDtypeStruct((nt, D_B), BF16),
        scratch_shapes=[pltpu.VMEM((hps, NA_WINDOW, HEAD_DIM), BF16),
                        pltpu.VMEM((hps, NA_WINDOW, HEAD_DIM), BF16),
                        pltpu.VMEM((hps, NA_BLOCK_ROWS, GRID_W, nkeys), F32),
                        pltpu.VMEM((hps, NA_BLOCK_ROWS, GRID_W, nkeys), BF16)],
        compiler_params=_cparams(("parallel", "parallel"), 40),
        name="neighborhood_attention",
    )(qkv, qkv, qkv, qkv, bias)


def _plain_norm(y, g):
    y = y.astype(F32)
    return (y * lax.rsqrt(jnp.mean(y * y, axis=-1, keepdims=True) + EPS) * g).astype(BF16)


OUT_ROW_CHUNK = 256


def _out_kernel(ya_ref, yb_ref, ga_ref, gb_ref, w_ref, x_ref, mod_ref, g2_ref, xo_ref, h_ref):
    for r0 in range(0, x_ref.shape[0], OUT_ROW_CHUNK):
        rows = slice(r0, r0 + OUT_ROW_CHUNK)
        ya = jnp.concatenate([ya_ref[h, rows, :] for h in range(N_HEADS_A)], axis=1)
        y = jnp.dot(_plain_norm(ya, ga_ref[...]), w_ref[:D_A, :], preferred_element_type=F32)
        y = y + jnp.dot(_plain_norm(yb_ref[rows, :], gb_ref[...]), w_ref[D_A:, :],
                        preferred_element_type=F32)
        x = x_ref[rows, :] + mod_ref[0, 2:3, :] * y
        xo_ref[rows, :] = x
        h = _modulated_norm(x, g2_ref[...], mod_ref[0, 3:4, :], mod_ref[0, 4:5, :])
        h_ref[rows, :] = h.astype(BF16)


def _out_proj(ya, yb, ga, gb, w, layer, x, mod_l, g2, seq_lens):
    nt, d = x.shape
    tm = 512
    starts = tuple(int(s) // tm for s in np.cumsum((0,) + seq_lens[:-1]))
    const = lambda i: (0, 0)
    return pl.pallas_call(
        _out_kernel,
        grid=(nt // tm,),
        in_specs=[pl.BlockSpec((N_HEADS_A, tm, HEAD_DIM), lambda i: (0, i, 0)),
                  pl.BlockSpec((tm, D_B), lambda i: (i, 0)),
                  pl.BlockSpec((1, D_A), const),
                  pl.BlockSpec((1, D_B), const),
                  pl.BlockSpec((None, D_A + D_B, d), lambda i: (layer, 0, 0), pipeline_mode=pl.Buffered(1)),
                  pl.BlockSpec((tm, d), lambda i: (i, 0)),
                  pl.BlockSpec((1, 6, d), lambda i: (_seq_index(i, starts), 0, 0)),
                  pl.BlockSpec((1, d), const)],
        out_specs=[pl.BlockSpec((tm, d), lambda i: (i, 0)),
                   pl.BlockSpec((tm, d), lambda i: (i, 0))],
        out_shape=[jax.ShapeDtypeStruct((nt, d), F32), jax.ShapeDtypeStruct((nt, d), BF16)],
        compiler_params=_cparams(("parallel",), 48),
        name="out_proj",
    )(ya, yb, ga, gb, w, x, mod_l, g2)


def _ffn_kernel(h_ref, wg_ref, wu_ref, wd_ref, x_ref, mod_ref, ng_ref, *refs, split_block):
    acc_ref = refs[-1]
    f = pl.program_id(1)

    @pl.when(f == 0)
    def _():
        acc_ref[...] = jnp.zeros_like(acc_ref)

    h = h_ref[...]
    a = jnp.dot(h, wg_ref[...], preferred_element_type=F32)
    b = jnp.dot(h, wu_ref[...], preferred_element_type=F32)
    act = (a / (1.0 + jnp.exp(-a)) * b).astype(BF16)
    acc_ref[...] += jnp.dot(act, wd_ref[...], preferred_element_type=F32)

    @pl.when(f == pl.num_programs(1) - 1)
    def _():
        x = x_ref[...] + mod_ref[0, 5:6, :] * acc_ref[...]
        if split_block is not None:
            y0_ref, y1_ref = refs[:-1]
            y = x * lax.rsqrt(jnp.mean(x * x, axis=-1, keepdims=True) + EPS) * ng_ref[...]
            i = pl.program_id(0)

            @pl.when(i < split_block)
            def _():
                y0_ref[...] = y

            @pl.when(i >= split_block)
            def _():
                y1_ref[...] = y
        else:
            nmod_ref, xo_ref, hn_ref = refs[:-1]
            xo_ref[...] = x
            hn = _modulated_norm(x, ng_ref[...], nmod_ref[0, 0:1, :], nmod_ref[0, 1:2, :])
            hn_ref[...] = hn.astype(BF16)


def _ffn(h, wg, wu, wd, layer, x, mod_l, next_g, next_mod, seq_lens, out_split=None):
    nt, d = x.shape
    dff = wg.shape[2]
    tm, tf = 512, 512
    final = next_mod is None
    split_block = out_split // tm if final else None
    starts = tuple(int(s) // tm for s in np.cumsum((0,) + seq_lens[:-1]))
    row_blk = pl.BlockSpec((tm, d), lambda i, f: (i, 0))
    mod_blk = pl.BlockSpec((1, 6, d), lambda i, f: (_seq_index(i, starts), 0, 0))
    in_specs = [row_blk,
                pl.BlockSpec((None, d, tf), lambda i, f: (layer, 0, f)),
                pl.BlockSpec((None, d, tf), lambda i, f: (layer, 0, f)),
                pl.BlockSpec((None, tf, d), lambda i, f: (layer, f, 0)),
                row_blk,
                mod_blk,
                pl.BlockSpec((1, d), lambda i, f: (0, 0))]
    operands = [h, wg, wu, wd, x, mod_l, next_g]
    if final:
        out_specs = [pl.BlockSpec((tm, d), lambda i, f: (jnp.minimum(i, split_block - 1), 0)),
                     pl.BlockSpec((tm, d), lambda i, f: (jnp.maximum(i - split_block, 0), 0))]
        out_shape = [jax.ShapeDtypeStruct((out_split, d), F32),
                     jax.ShapeDtypeStruct((nt - out_split, d), F32)]
    else:
        in_specs.append(mod_blk)
        operands.append(next_mod)
        out_specs = [row_blk, row_blk]
        out_shape = [jax.ShapeDtypeStruct((nt, d), F32), jax.ShapeDtypeStruct((nt, d), BF16)]
    return pl.pallas_call(
        functools.partial(_ffn_kernel, split_block=split_block),
        grid=(nt // tm, dff // tf),
        in_specs=in_specs,
        out_specs=out_specs,
        out_shape=out_shape,
        scratch_shapes=[pltpu.VMEM((tm, d), F32)],
        compiler_params=_cparams(("arbitrary" if final else "parallel", "arbitrary"), 56),
        name="ffn_final" if final else "ffn",
    )(*operands)


def _trunk(xa, xb, c, seq_lens, t5_table, norm1_g, norm2_g, w_ada, b_ada, w_in, out_norm_a,
           out_norm_b, w_out, na_rpb, w_gate, w_up, w_down, final_g):
    depth = w_in.shape[0]
    nseq = len(seq_lens)
    d = xa.shape[1]
    out_split = xa.shape[0]
    assert all(s % DIL_BLOCK == 0 for s in seq_lens)

    c_pad = jnp.zeros((8, d), F32).at[:nseq].set(c)
    mod = _ada_modulation(c_pad, w_ada, b_ada)[:, :nseq].reshape(depth, nseq, 6, d)

    col_scale = jnp.ones((6, D_A), F32).at[0].set(QK_SCALE).at[3].set(QK_SCALE).reshape(1, 6 * D_A)
    dil_bias = _dilated_bias(t5_table)
    w_in, w_out, w_gate, w_up, w_down = (w.astype(BF16) for w in (w_in, w_out, w_gate, w_up, w_down))
    x, h = _prenorm(xa, xb, mod[0], norm1_g[0][None], seq_lens)
    for l in range(depth):
        final = l == depth - 1
        qkv_a, qkv_a16 = _qkv_proj(h, col_scale, w_in, l, 0, True)
        qkv_b, = _qkv_proj(h, col_scale, w_in, l, 1, False)
        ya = _dilated_attention(qkv_a, qkv_a16, dil_bias, seq_lens)
        yb = _neighborhood_attention(qkv_b, _na_bias(na_rpb[l]), seq_lens)
        x, h2 = _out_proj(ya, yb, out_norm_a[l][None], out_norm_b[l][None], w_out, l,
                          x, mod[l], norm2_g[l][None], seq_lens)
        next_g = final_g[None] if final else norm1_g[l + 1][None]
        next_mod = None if final else mod[l + 1]
        out = _ffn(h2, w_gate, w_up, w_down, l, x, mod[l], next_g, next_mod, seq_lens, out_split)
        if not final:
            x, h = out
    return out


def kernel(x_prompt, x_sample, c_prompt, c_sample, t5_table, norm1_g, norm2_g, w_ada, b_ada, w_in,
           out_norm_a, out_norm_b, w_out, na_rpb, w_gate, w_up, w_down, final_g):
    bp, tp, d = x_prompt.shape
    bs, ts, _ = x_sample.shape
    seq_lens = (tp,) * bp + (ts,) * bs
    c = jnp.concatenate([c_prompt, c_sample], axis=0)
    y_prompt, y_sample = _trunk(x_prompt.reshape(bp * tp, d), x_sample.reshape(bs * ts, d), c, seq_lens,
                                t5_table, norm1_g, norm2_g, w_ada, b_ada, w_in, out_norm_a, out_norm_b,
                                w_out, na_rpb, w_gate, w_up, w_down, final_g)
    return (y_prompt.reshape(bp, tp, d), y_sample.reshape(bs, ts, d))
```

```python
import functools
import math

import numpy as np
import jax
import jax.numpy as jnp
from jax import lax
from jax.experimental import pallas as pl
from jax.experimental.pallas import tpu as pltpu

D_MODEL = 2048
HEAD_DIM = 128
N_HEADS_A = 8
N_HEADS_B = 8
D_A = N_HEADS_A * HEAD_DIM
D_B = N_HEADS_B * HEAD_DIM
QK_SCALE = HEAD_DIM ** -0.5
DILATIONS = (1, 4, 16)
RADIUS = 64
N_BUCKETS = 32
MAX_DISTANCE = 1024
GRID_W = 64
NA_ROWS = 8
NA_COLS = 16
EPS = 1e-6
NEG = -1e30

DIL_BLOCK = 1024
DIL_VIEW = 16
VIEW_ROWS = DIL_BLOCK // DIL_VIEW
TILES = DIL_BLOCK // RADIUS
D4_GROUPS = DIL_VIEW // 4
D4_ROWS = RADIUS // D4_GROUPS
D4_BLOCKS = VIEW_ROWS // D4_ROWS
DIL_HEADS_PER_STEP = 2
NA_BLOCK_ROWS = 16
NA_HALO_ROWS = 8
NA_BLOCK = NA_BLOCK_ROWS * GRID_W
NA_HEADS_PER_STEP = 2
NA_HALO = NA_HALO_ROWS * GRID_W
NA_WINDOW = NA_BLOCK + 2 * NA_HALO

F32 = jnp.float32
BF16 = jnp.bfloat16
MIB = 1024 * 1024


def _cparams(sem, vmem_mib):
    return pltpu.CompilerParams(dimension_semantics=sem, vmem_limit_bytes=vmem_mib * MIB)


def _seq_index(i, starts):
    idx = 0
    for s in starts[1:]:
        idx = idx + (i >= s).astype(jnp.int32)
    return idx


def _lanes(r):
    return slice(r * HEAD_DIM, (r + 1) * HEAD_DIM)


def _ada_kernel(c_ref, w_ref, b_ref, o_ref):
    c = c_ref[...]
    s = c / (1.0 + jnp.exp(-c))
    o_ref[0] = jnp.dot(s, w_ref[0], precision=lax.Precision.HIGHEST,
                       preferred_element_type=F32) + b_ref[0]


def _ada_modulation(c_pad, w_ada, b_ada):
    depth, d, n = w_ada.shape
    tn = 1024
    return pl.pallas_call(
        _ada_kernel,
        grid=(depth, n // tn),
        in_specs=[pl.BlockSpec((8, d), lambda l, j: (0, 0)),
                  pl.BlockSpec((1, d, tn), lambda l, j: (l, 0, j)),
                  pl.BlockSpec((1, 1, tn), lambda l, j: (l, 0, j))],
        out_specs=pl.BlockSpec((1, 8, tn), lambda l, j: (l, 0, j)),
        out_shape=jax.ShapeDtypeStruct((depth, 8, n), F32),
        compiler_params=_cparams(("parallel", "parallel"), 40),
        name="ada_modulation",
    )(c_pad, w_ada, b_ada.reshape(depth, 1, n))


def _modulated_norm(x, g, shift, scale):
    y = x * lax.rsqrt(jnp.mean(x * x, axis=-1, keepdims=True) + EPS) * g
    return y * (1.0 + scale) + shift


def _prenorm_kernel(xa_ref, xb_ref, mod_ref, g_ref, x_ref, h_ref, *, split_block):
    x = jnp.where(pl.program_id(0) < split_block, xa_ref[...], xb_ref[...])
    x_ref[...] = x
    h = _modulated_norm(x, g_ref[...], mod_ref[0, 0:1, :], mod_ref[0, 1:2, :])
    h_ref[...] = h.astype(BF16)


def _prenorm(xa, xb, mod_l, g, seq_lens):
    d = xa.shape[1]
    nt = xa.shape[0] + xb.shape[0]
    tm = 512
    split_block = xa.shape[0] // tm
    starts = tuple(int(s) // tm for s in np.cumsum((0,) + seq_lens[:-1]))
    row_blk = pl.BlockSpec((tm, d), lambda i: (i, 0))
    return pl.pallas_call(
        functools.partial(_prenorm_kernel, split_block=split_block),
        grid=(nt // tm,),
        in_specs=[pl.BlockSpec((tm, d), lambda i: (jnp.minimum(i, split_block - 1), 0)),
                  pl.BlockSpec((tm, d), lambda i: (jnp.maximum(i - split_block, 0), 0)),
                  pl.BlockSpec((1, 6, d), lambda i: (_seq_index(i, starts), 0, 0)),
                  pl.BlockSpec((1, d), lambda i: (0, 0))],
        out_specs=[row_blk, row_blk],
        out_shape=[jax.ShapeDtypeStruct((nt, d), F32), jax.ShapeDtypeStruct((nt, d), BF16)],
        compiler_params=_cparams(("parallel",), 40),
        name="prenorm",
    )(xa, xb, mod_l, g)


SLOT_K, SLOT_V, SLOT_Q = range(3)
KV_BLOCK = SLOT_K // 2


def _qkv_kernel(h_ref, cs_ref, w_ref, o_ref, *view, tm):
    r = jnp.dot(h_ref[...], w_ref[...], preferred_element_type=F32) * cs_ref[...]
    for c in range(o_ref.shape[0]):
        o_ref[c, 0] = r[:, _lanes(c)].astype(BF16)
    if view:
        ov_ref, r_s = view
        for c in range(o_ref.shape[0]):
            r_s[c] = r[:, _lanes(c)]
            for g in range(DIL_VIEW):
                rows = r_s[c, pl.ds(g, tm // DIL_VIEW, stride=DIL_VIEW), :]
                ov_ref[c, 0, :, _lanes(g)] = rows.astype(BF16)


def _qkv_proj(h, col_scale, w, layer, group, with_view):
    nt, d = h.shape
    tm = 1024
    width = N_HEADS_A * HEAD_DIM
    assert w.shape[2] == 6 * width and N_HEADS_A == N_HEADS_B
    slot = lambda j: lax.rem(j + 2, 3)
    out_specs = [pl.BlockSpec((N_HEADS_A, 1, tm, HEAD_DIM), lambda i, j: (0, slot(j), i, 0))]
    out_shape = [jax.ShapeDtypeStruct((N_HEADS_A, 3, nt, HEAD_DIM), BF16)]
    scratch = []
    if with_view:
        out_specs.append(pl.BlockSpec((N_HEADS_A, 1, tm // DIL_VIEW, DIL_VIEW * HEAD_DIM),
                                      lambda i, j: (0, slot(j), i, 0)))
        out_shape.append(jax.ShapeDtypeStruct((N_HEADS_A, 3, nt // DIL_VIEW, DIL_VIEW * HEAD_DIM), BF16))
        scratch.append(pltpu.VMEM((N_HEADS_A, tm, HEAD_DIM), F32))
    return pl.pallas_call(
        functools.partial(_qkv_kernel, tm=tm),
        grid=(nt // tm, 3),
        in_specs=[pl.BlockSpec((tm, d), lambda i, j: (i, 0)),
                  pl.BlockSpec((1, width), lambda i, j: (0, 3 * group + j)),
                  pl.BlockSpec((None, d, width), lambda i, j: (layer, 0, 3 * group + j))],
        out_specs=out_specs,
        out_shape=out_shape,
        scratch_shapes=scratch,
        compiler_params=_cparams(("parallel", "parallel"), 48),
        name="qkv_proj_view" if with_view else "qkv_proj",
    )(h, col_scale, w)


def _t5_bucket(rel):
    nb = N_BUCKETS // 2
    exact = nb // 2
    n = np.abs(rel)
    sign = np.where(rel > 0, nb, 0)
    large = exact + (np.log(np.maximum(n, 1) / exact) / math.log(MAX_DISTANCE / exact) * (nb - exact)).astype(np.int64)
    large = np.minimum(large, nb - 1)
    return (sign + np.where(n < exact, n, large)).astype(np.int32)


def _tile_offsets():
    i = np.arange(RADIUS)[:, None]
    j = np.arange(3 * RADIUS)[None, :]
    band = j - RADIUS - i
    qa, qi = i // D4_ROWS, i % D4_ROWS
    ka, kj = j // (3 * D4_ROWS), j % (3 * D4_ROWS)
    perm = D4_GROUPS * (kj - D4_ROWS - qi) + (ka - qa)
    return (band, perm, band)


def _dilated_bias(t5_table):
    table = t5_table.astype(F32).T
    tiles = []
    for dil, rel in zip(DILATIONS, _tile_offsets()):
        inside = np.abs(rel) <= RADIUS
        bucket = _t5_bucket(rel * dil)
        b = jnp.full((table.shape[0],) + rel.shape, NEG, F32)
        for o in range(N_BUCKETS):
            b = jnp.where((inside & (bucket == o))[None], table[:, o, None, None], b)
        tiles.append(b)
    return jnp.stack(tiles, axis=1)


def _view_rows(ref_p, ref_c, ref_n, lo, hi, lanes):
    parts = []
    if lo < 0:
        parts.append(ref_p[VIEW_ROWS + lo:VIEW_ROWS, lanes])
    parts.append(ref_c[max(lo, 0):min(hi, VIEW_ROWS), lanes])
    if hi > VIEW_ROWS:
        parts.append(ref_n[0:hi - VIEW_ROWS, lanes])
    return parts


def _dil_kernel(q1_blk, kv1p_blk, kv1c_blk, kv1n_blk, q16_blk, kv16p_blk, kv16c_blk, kv16n_blk,
                bias_ref, o_ref, s_all, e_all, pos_all, num_all, m_all_s, l_all, kt_all,
                *, first_blocks, last_blocks):
    n = pl.program_id(1)
    first = functools.reduce(jnp.logical_or, [n == s for s in first_blocks])
    last = functools.reduce(jnp.logical_or, [n == s for s in last_blocks])
    fpen = jnp.where(first, NEG, 0.0).astype(F32)
    lpen = jnp.where(last, NEG, 0.0).astype(F32)
    col = lax.broadcasted_iota(jnp.int32, (1, 3 * RADIUS), 1)
    pen_first = jnp.where(col < RADIUS, fpen, 0.0)
    pen_last = jnp.where(col >= 2 * RADIUS, lpen, 0.0)
    perm_row = lax.rem(col, 3 * D4_ROWS)
    pen_perm_first = jnp.where(perm_row < D4_ROWS, fpen, 0.0)
    pen_perm_last = jnp.where(perm_row >= 2 * D4_ROWS, lpen, 0.0)
    ones = jnp.ones((3 * RADIUS, HEAD_DIM), BF16)

    def keys_d1(refs, b):
        ref_p, ref_c, ref_n = refs
        if b == 0:
            return jnp.concatenate([ref_p[...], ref_c[0:2 * RADIUS]], axis=0)
        if b == TILES - 1:
            return jnp.concatenate([ref_c[DIL_BLOCK - 2 * RADIUS:DIL_BLOCK], ref_n[...]], axis=0)
        return ref_c[RADIUS * (b - 1):RADIUS * (b + 2)]

    def keys_d4(refs, r4, b):
        parts = []
        for a in range(D4_GROUPS):
            parts += _view_rows(*refs, D4_ROWS * (b - 1), D4_ROWS * (b + 2), _lanes(D4_GROUPS * a + r4))
        return jnp.concatenate(parts, axis=0)

    def keys_d16(refs, r):
        return jnp.concatenate([ref[:, _lanes(r)] for ref in refs], axis=0)

    def head_passes(hh):
        q1_ref, q16_ref = q1_blk.at[hh, 0], q16_blk.at[hh, 0]
        k1 = tuple(blk.at[hh, 0] for blk in (kv1p_blk, kv1c_blk, kv1n_blk))
        v1 = tuple(blk.at[hh, 1] for blk in (kv1p_blk, kv1c_blk, kv1n_blk))
        k16 = tuple(blk.at[hh, 0] for blk in (kv16p_blk, kv16c_blk, kv16n_blk))
        v16 = tuple(blk.at[hh, 1] for blk in (kv16p_blk, kv16c_blk, kv16n_blk))
        s_s, e_s, pos_s = s_all.at[hh], e_all.at[hh], pos_all.at[hh]
        num_s, m_s, l_s, kt_s = num_all.at[hh], m_all_s.at[hh], l_all.at[hh], kt_all.at[hh]

        def tile_operands(p, t):
            if p == 0:
                pen = pen_first if t == 0 else pen_last if t == TILES - 1 else None
                return q1_ref[RADIUS * t:RADIUS * (t + 1)], keys_d1(k1, t), keys_d1(v1, t), pen
            if p == 1:
                r4, b = divmod(t, D4_BLOCKS)
                q = jnp.concatenate([q16_ref[D4_ROWS * b:D4_ROWS * (b + 1), _lanes(D4_GROUPS * a + r4)]
                                     for a in range(D4_GROUPS)], axis=0)
                pen = pen_perm_first if b == 0 else pen_perm_last if b == D4_BLOCKS - 1 else None
                return q, keys_d4(k16, r4, b), keys_d4(v16, r4, b), pen
            return q16_ref[:, _lanes(t)], keys_d16(k16, t), keys_d16(v16, t), pen_first + pen_last

        def store_rows(p, t, tile, pos_idx, view_ref):
            if p == 0:
                pos_s[pos_idx, RADIUS * t:RADIUS * (t + 1), :] = tile
            elif p == 1:
                r4, b = divmod(t, D4_BLOCKS)
                for a in range(D4_GROUPS):
                    rows = slice(D4_ROWS * b, D4_ROWS * (b + 1))
                    view_ref[0, rows, _lanes(D4_GROUPS * a + r4)] = tile[D4_ROWS * a:D4_ROWS * (a + 1)]
            else:
                view_ref[1, :, _lanes(t)] = tile

        def transpose_keys(p, t):
            kt_s[p, t] = tile_operands(p, t)[1].T

        def logits(p, t):
            q, _, _, pen = tile_operands(p, t)
            s = jnp.dot(q, kt_s[p, t], preferred_element_type=F32)
            s = s + bias_ref[hh, p]
            s_s[p, t] = s if pen is None else s + pen

        def softmax(p, t):
            s = s_s[p, t]
            m = jnp.max(s, axis=-1, keepdims=True)
            e_s[p, t] = jnp.exp(s - m).astype(BF16)
            store_rows(p, t, jnp.broadcast_to(m, (RADIUS, HEAD_DIM)), 1, m_s)

        def values(p, t):
            _, _, v, _ = tile_operands(p, t)
            nv = jnp.dot(e_s[p, t], jnp.concatenate([v, ones], axis=1), preferred_element_type=F32)
            store_rows(p, t, nv[:, :HEAD_DIM], 0, num_s)
            store_rows(p, t, nv[:, HEAD_DIM:], 2, l_s)

        def merge():
            for r in range(DIL_VIEW):
                strided = pl.ds(r, VIEW_ROWS, stride=DIL_VIEW)
                nums = (pos_s[0, strided, :], num_s[0, :, _lanes(r)], num_s[1, :, _lanes(r)])
                ms = (pos_s[1, strided, :], m_s[0, :, _lanes(r)], m_s[1, :, _lanes(r)])
                ls = (pos_s[2, strided, :], l_s[0, :, _lanes(r)], l_s[1, :, _lanes(r)])
                m_max = jnp.maximum(jnp.maximum(ms[0], ms[1]), ms[2])
                ws = [jnp.exp(m - m_max) for m in ms]
                num = ws[0] * nums[0] + ws[1] * nums[1] + ws[2] * nums[2]
                den = ws[0] * ls[0] + ws[1] * ls[1] + ws[2] * ls[2]
                o_ref[hh, strided, :] = num / den

        def all_tiles(p, *tile_stages):
            return lambda: [stage(p, t) for stage in tile_stages for t in range(TILES)]

        return [all_tiles(2, transpose_keys, logits), all_tiles(1, transpose_keys, logits),
                all_tiles(2, softmax), all_tiles(2, values),
                all_tiles(0, transpose_keys, logits), all_tiles(1, softmax), all_tiles(1, values),
                all_tiles(0, softmax), all_tiles(0, values), merge]

    first_head, second_head = head_passes(0), head_passes(1)
    for run in first_head[:7] + second_head[:2] + first_head[7:9] + second_head[2:5] + first_head[9:] + second_head[5:]:
        run()


def _dilated_attention(qkv, qkv16, bias, seq_lens):
    nt = qkv.shape[2]
    nblk = nt // DIL_BLOCK
    bounds = np.cumsum((0,) + seq_lens) // DIL_BLOCK
    first_blocks = tuple(int(b) for b in bounds[:-1])
    last_blocks = tuple(int(b) - 1 for b in bounds[1:])
    per = DIL_BLOCK // RADIUS
    prev1 = lambda n: jnp.maximum(per * n - 1, 0)
    next1 = lambda n: jnp.minimum(per * (n + 1), nt // RADIUS - 1)
    prev16 = lambda n: jnp.maximum(n - 1, 0)
    next16 = lambda n: jnp.minimum(n + 1, nblk - 1)
    kv_blk = KV_BLOCK
    hps = DIL_HEADS_PER_STEP
    blk16 = (hps, 1, VIEW_ROWS, DIL_VIEW * HEAD_DIM)
    kv16 = (hps, 2, VIEW_ROWS, DIL_VIEW * HEAD_DIM)
    npat = len(DILATIONS)
    kernel = functools.partial(_dil_kernel, first_blocks=first_blocks, last_blocks=last_blocks)
    out = pl.pallas_call(
        kernel,
        grid=(N_HEADS_A // hps, nblk),
        in_specs=[pl.BlockSpec((hps, 1, DIL_BLOCK, HEAD_DIM), lambda h, n: (h, SLOT_Q, n, 0)),
                  pl.BlockSpec((hps, 2, RADIUS, HEAD_DIM), lambda h, n: (h, kv_blk, prev1(n), 0)),
                  pl.BlockSpec((hps, 2, DIL_BLOCK, HEAD_DIM), lambda h, n: (h, kv_blk, n, 0)),
                  pl.BlockSpec((hps, 2, RADIUS, HEAD_DIM), lambda h, n: (h, kv_blk, next1(n), 0)),
                  pl.BlockSpec(blk16, lambda h, n: (h, SLOT_Q, n, 0)),
                  pl.BlockSpec(kv16, lambda h, n: (h, kv_blk, prev16(n), 0)),
                  pl.BlockSpec(kv16, lambda h, n: (h, kv_blk, n, 0)),
                  pl.BlockSpec(kv16, lambda h, n: (h, kv_blk, next16(n), 0)),
                  pl.BlockSpec((hps, npat, RADIUS, 3 * RADIUS), lambda h, n: (h, 0, 0, 0))],
        out_specs=pl.BlockSpec((hps, DIL_BLOCK, HEAD_DIM), lambda h, n: (h, n, 0)),
        out_shape=jax.ShapeDtypeStruct((N_HEADS_A, nt, HEAD_DIM), F32),
        scratch_shapes=[pltpu.VMEM((hps, npat, TILES, RADIUS, 3 * RADIUS), F32),
                        pltpu.VMEM((hps, npat, TILES, RADIUS, 3 * RADIUS), BF16),
                        pltpu.VMEM((hps, 3, DIL_BLOCK, HEAD_DIM), F32),
                        pltpu.VMEM((hps, 2, VIEW_ROWS, DIL_VIEW * HEAD_DIM), F32),
                        pltpu.VMEM((hps, 2, VIEW_ROWS, DIL_VIEW * HEAD_DIM), F32),
                        pltpu.VMEM((hps, 2, VIEW_ROWS, DIL_VIEW * HEAD_DIM), F32),
                        pltpu.VMEM((hps, npat, TILES, HEAD_DIM, 3 * RADIUS), BF16)],
        compiler_params=_cparams(("parallel", "parallel"), 48),
        name="dilated_attention",
    )(qkv, qkv, qkv, qkv, qkv16, qkv16, qkv16, qkv16, bias)
    return out


def _na_bias(rpb):
    c = np.arange(GRID_W)
    cstart = np.clip(c - NA_COLS // 2, 0, GRID_W - NA_COLS)
    cmask = (c[None, :] >= cstart[:, None]) & (c[None, :] < cstart[:, None] + NA_COLS)
    coff = np.clip(c[None, :] - c[:, None], -(NA_COLS - 1), NA_COLS - 1) + NA_COLS - 1
    cols = jnp.full(rpb.shape[:2] + (GRID_W, GRID_W), NEG, F32)
    for o in range(2 * NA_COLS - 1):
        cols = jnp.where((cmask & (coff == o))[None, None], rpb[:, :, o, None, None].astype(F32), cols)
    tiles = [cols[:, NA_ROWS - 1 - dd:2 * NA_ROWS - 1 - dd] for dd in range(NA_ROWS)]
    b = jnp.stack(tiles, axis=1).transpose(0, 1, 3, 2, 4)
    return b.reshape(rpb.shape[0], NA_ROWS, GRID_W, NA_ROWS * GRID_W)


def _na_kernel(q_blk, kvp_blk, kvc_blk, kvn_blk, bias_ref, o_ref, kw, vw,
               s_s, e_s, *, block_starts, seq_rows):
    n = pl.program_id(1)
    nstart = jnp.int32(block_starts[0])
    rows = jnp.int32(seq_rows[0])
    for s, r in zip(block_starts[1:], seq_rows[1:]):
        nstart = jnp.where(n >= s, s, nstart)
        rows = jnp.where(n >= s, r, rows)

    nkeys = NA_ROWS * GRID_W
    shifts, starts = [], []
    for i in range(NA_BLOCK_ROWS):
        r = (n - nstart) * NA_BLOCK_ROWS + i
        dd = r - jnp.clip(r - NA_ROWS // 2, 0, rows - NA_ROWS)
        shifts.append(dd)
        starts.append(pl.multiple_of((NA_HALO_ROWS + i - dd) * GRID_W, GRID_W))
    ones = jnp.ones((nkeys, HEAD_DIM), BF16)

    def head_passes(hh):
        q_ref = q_blk.at[hh, 0]

        def logits_pass():
            for which, win in enumerate((kw, vw)):
                win[hh, 0:NA_HALO, :] = kvp_blk[hh, which]
                win[hh, NA_HALO:NA_HALO + NA_BLOCK, :] = kvc_blk[hh, which]
                win[hh, NA_HALO + NA_BLOCK:NA_WINDOW, :] = kvn_blk[hh, which]
            for i in range(NA_BLOCK_ROWS):
                k = kw[hh, pl.ds(starts[i], nkeys), :]
                q = q_ref[i * GRID_W:(i + 1) * GRID_W, :]
                s = lax.dot_general(q, k, (((1,), (1,)), ((), ())), preferred_element_type=F32)
                s_s[hh, i] = s + bias_ref[hh, shifts[i]]

        def softmax_pass():
            for i in range(NA_BLOCK_ROWS):
                s = s_s[hh, i]
                e_s[hh, i] = jnp.exp(s - jnp.max(s, axis=-1, keepdims=True)).astype(BF16)

        def values_pass():
            for i in range(NA_BLOCK_ROWS):
                v = vw[hh, pl.ds(starts[i], nkeys), :]
                nv = jnp.dot(e_s[hh, i], jnp.concatenate([v, ones], axis=1), preferred_element_type=F32)
                out = (nv[:, :HEAD_DIM] / nv[:, HEAD_DIM:]).astype(o_ref.dtype)
                o_ref[i * GRID_W:(i + 1) * GRID_W, _lanes(hh)] = out

        return [logits_pass, softmax_pass, values_pass]

    first_head, second_head = head_passes(0), head_passes(1)
    for run in (first_head[0], second_head[0], first_head[1], first_head[2], second_head[1], second_head[2]):
        run()


def _neighborhood_attention(qkv, bias, seq_lens):
    nt = qkv.shape[2]
    block_starts = tuple(int(s) // NA_BLOCK for s in np.cumsum((0,) + seq_lens[:-1]))
    seq_rows = tuple(int(s) // GRID_W for s in seq_lens)
    assert all(r >= NA_ROWS and r % NA_BLOCK_ROWS == 0 for r in seq_rows)
    per = NA_BLOCK // NA_HALO
    prev = lambda n: jnp.maximum(per * n - 1, 0)
    nxt = lambda n: jnp.minimum(per * (n + 1), nt // NA_HALO - 1)
    kv_blk = KV_BLOCK
    kernel = functools.partial(_na_kernel, block_starts=block_starts, seq_rows=seq_rows)
    nkeys = NA_ROWS * GRID_W
    hps = NA_HEADS_PER_STEP
    return pl.pallas_call(
        kernel,
        grid=(N_HEADS_B // hps, nt // NA_BLOCK),
        in_specs=[pl.BlockSpec((hps, 1, NA_BLOCK, HEAD_DIM), lambda h, n: (h, SLOT_Q, n, 0)),
                  pl.BlockSpec((hps, 2, NA_HALO, HEAD_DIM), lambda h, n: (h, kv_blk, prev(n), 0)),
                  pl.BlockSpec((hps, 2, NA_BLOCK, HEAD_DIM), lambda h, n: (h, kv_blk, n, 0)),
                  pl.BlockSpec((hps, 2, NA_HALO, HEAD_DIM), lambda h, n: (h, kv_blk, nxt(n), 0)),
                  pl.BlockSpec((hps, NA_ROWS, GRID_W, nkeys), lambda h, n: (h, 0, 0, 0))],
        out_specs=pl.BlockSpec((NA_BLOCK, hps * HEAD_DIM), lambda h, n: (n, h)),
        out_shape=jax.ShapeDtypeStruct((nt, D_B), BF16),
        scratch_shapes=[pltpu.VMEM((hps, NA_WINDOW, HEAD_DIM), BF16),
                        pltpu.VMEM((hps, NA_WINDOW, HEAD_DIM), BF16),
                        pltpu.VMEM((hps, NA_BLOCK_ROWS, GRID_W, nkeys), F32),
                        pltpu.VMEM((hps, NA_BLOCK_ROWS, GRID_W, nkeys), BF16)],
        compiler_params=_cparams(("parallel", "parallel"), 40),
        name="neighborhood_attention",
    )(qkv, qkv, qkv, qkv, bias)


def _plain_norm(y, g):
    y = y.astype(F32)
    return (y * lax.rsqrt(jnp.mean(y * y, axis=-1, keepdims=True) + EPS) * g).astype(BF16)


OUT_ROW_CHUNK = 256


def _out_kernel(ya_ref, yb_ref, ga_ref, gb_ref, w_ref, x_ref, mod_ref, g2_ref, xo_ref, h_ref):
    for r0 in range(0, x_ref.shape[0], OUT_ROW_CHUNK):
        rows = slice(r0, r0 + OUT_ROW_CHUNK)
        ya = jnp.concatenate([ya_ref[h, rows, :] for h in range(N_HEADS_A)], axis=1)
        y = jnp.dot(_plain_norm(ya, ga_ref[...]), w_ref[:D_A, :], preferred_element_type=F32)
        y = y + jnp.dot(_plain_norm(yb_ref[rows, :], gb_ref[...]), w_ref[D_A:, :],
                        preferred_element_type=F32)
        x = x_ref[rows, :] + mod_ref[0, 2:3, :] * y
        xo_ref[rows, :] = x
        h = _modulated_norm(x, g2_ref[...], mod_ref[0, 3:4, :], mod_ref[0, 4:5, :])
        h_ref[rows, :] = h.astype(BF16)


def _out_proj(ya, yb, ga, gb, w, layer, x, mod_l, g2, seq_lens):
    nt, d = x.shape
    tm = 512
    starts = tuple(int(s) // tm for s in np.cumsum((0,) + seq_lens[:-1]))
    const = lambda i: (0, 0)
    return pl.pallas_call(
        _out_kernel,
        grid=(nt // tm,),
        in_specs=[pl.BlockSpec((N_HEADS_A, tm, HEAD_DIM), lambda i: (0, i, 0)),
                  pl.BlockSpec((tm, D_B), lambda i: (i, 0)),
                  pl.BlockSpec((1, D_A), const),
                  pl.BlockSpec((1, D_B), const),
                  pl.BlockSpec((None, D_A + D_B, d), lambda i: (layer, 0, 0), pipeline_mode=pl.Buffered(1)),
                  pl.BlockSpec((tm, d), lambda i: (i, 0)),
                  pl.BlockSpec((1, 6, d), lambda i: (_seq_index(i, starts), 0, 0)),
                  pl.BlockSpec((1, d), const)],
        out_specs=[pl.BlockSpec((tm, d), lambda i: (i, 0)),
                   pl.BlockSpec((tm, d), lambda i: (i, 0))],
        out_shape=[jax.ShapeDtypeStruct((nt, d), F32), jax.ShapeDtypeStruct((nt, d), BF16)],
        compiler_params=_cparams(("parallel",), 48),
        name="out_proj",
    )(ya, yb, ga, gb, w, x, mod_l, g2)


def _ffn_kernel(h_ref, wg_ref, wu_ref, wd_ref, x_ref, mod_ref, ng_ref, *refs, split_block):
    acc_ref = refs[-1]
    f = pl.program_id(1)

    @pl.when(f == 0)
    def _():
        acc_ref[...] = jnp.zeros_like(acc_ref)

    h = h_ref[...]
    a = jnp.dot(h, wg_ref[...], preferred_element_type=F32)
    b = jnp.dot(h, wu_ref[...], preferred_element_type=F32)
    act = (a / (1.0 + jnp.exp(-a)) * b).astype(BF16)
    acc_ref[...] += jnp.dot(act, wd_ref[...], preferred_element_type=F32)

    @pl.when(f == pl.num_programs(1) - 1)
    def _():
        x = x_ref[...] + mod_ref[0, 5:6, :] * acc_ref[...]
        if split_block is not None:
            y0_ref, y1_ref = refs[:-1]
            y = x * lax.rsqrt(jnp.mean(x * x, axis=-1, keepdims=True) + EPS) * ng_ref[...]
            i = pl.program_id(0)

            @pl.when(i < split_block)
            def _():
                y0_ref[...] = y

            @pl.when(i >= split_block)
            def _():
                y1_ref[...] = y
        else:
            nmod_ref, xo_ref, hn_ref = refs[:-1]
            xo_ref[...] = x
            hn = _modulated_norm(x, ng_ref[...], nmod_ref[0, 0:1, :], nmod_ref[0, 1:2, :])
            hn_ref[...] = hn.astype(BF16)


def _ffn(h, wg, wu, wd, layer, x, mod_l, next_g, next_mod, seq_lens, out_split=None):
    nt, d = x.shape
    dff = wg.shape[2]
    tm, tf = 512, 512
    final = next_mod is None
    split_block = out_split // tm if final else None
    starts = tuple(int(s) // tm for s in np.cumsum((0,) + seq_lens[:-1]))
    row_blk = pl.BlockSpec((tm, d), lambda i, f: (i, 0))
    mod_blk = pl.BlockSpec((1, 6, d), lambda i, f: (_seq_index(i, starts), 0, 0))
    in_specs = [row_blk,
                pl.BlockSpec((None, d, tf), lambda i, f: (layer, 0, f)),
                pl.BlockSpec((None, d, tf), lambda i, f: (layer, 0, f)),
                pl.BlockSpec((None, tf, d), lambda i, f: (layer, f, 0)),
                row_blk,
                mod_blk,
                pl.BlockSpec((1, d), lambda i, f: (0, 0))]
    operands = [h, wg, wu, wd, x, mod_l, next_g]
    if final:
        out_specs = [pl.BlockSpec((tm, d), lambda i, f: (jnp.minimum(i, split_block - 1), 0)),
                     pl.BlockSpec((tm, d), lambda i, f: (jnp.maximum(i - split_block, 0), 0))]
        out_shape = [jax.ShapeDtypeStruct((out_split, d), F32),
                     jax.ShapeDtypeStruct((nt - out_split, d), F32)]
    else:
        in_specs.append(mod_blk)
        operands.append(next_mod)
        out_specs = [row_blk, row_blk]
        out_shape = [jax.ShapeDtypeStruct((nt, d), F32), jax.ShapeDtypeStruct((nt, d), BF16)]
    return pl.pallas_call(
        functools.partial(_ffn_kernel, split_block=split_block),
        grid=(nt // tm, dff // tf),
        in_specs=in_specs,
        out_specs=out_specs,
        out_shape=out_shape,
        scratch_shapes=[pltpu.VMEM((tm, d), F32)],
        compiler_params=_cparams(("arbitrary" if final else "parallel", "arbitrary"), 56),
        name="ffn_final" if final else "ffn",
    )(*operands)


def _trunk(xa, xb, c, seq_lens, t5_table, norm1_g, norm2_g, w_ada, b_ada, w_in, out_norm_a,
           out_norm_b, w_out, na_rpb, w_gate, w_up, w_down, final_g):
    depth = w_in.shape[0]
    nseq = len(seq_lens)
    d = xa.shape[1]
    out_split = xa.shape[0]
    assert all(s % DIL_BLOCK == 0 for s in seq_lens)

    c_pad = jnp.zeros((8, d), F32).at[:nseq].set(c)
    mod = _ada_modulation(c_pad, w_ada, b_ada)[:, :nseq].reshape(depth, nseq, 6, d)

    col_scale = jnp.ones((6, D_A), F32).at[0].set(QK_SCALE).at[3].set(QK_SCALE).reshape(1, 6 * D_A)
    dil_bias = _dilated_bias(t5_table)
    w_in, w_out, w_gate, w_up, w_down = (w.astype(BF16) for w in (w_in, w_out, w_gate, w_up, w_down))
    x, h = _prenorm(xa, xb, mod[0], norm1_g[0][None], seq_lens)
    for l in range(depth):
        final = l == depth - 1
        qkv_a, qkv_a16 = _qkv_proj(h, col_scale, w_in, l, 0, True)
        qkv_b, = _qkv_proj(h, col_scale, w_in, l, 1, False)
        ya = _dilated_attention(qkv_a, qkv_a16, dil_bias, seq_lens)
        yb = _neighborhood_attention(qkv_b, _na_bias(na_rpb[l]), seq_lens)
        x, h2 = _out_proj(ya, yb, out_norm_a[l][None], out_norm_b[l][None], w_out, l,
                          x, mod[l], norm2_g[l][None], seq_lens)
        next_g = final_g[None] if final else norm1_g[l + 1][None]
        next_mod = None if final else mod[l + 1]
        out = _ffn(h2, w_gate, w_up, w_down, l, x, mod[l], next_g, next_mod, seq_lens, out_split)
        if not final:
            x, h = out
    return out


def kernel(x_prompt, x_sample, c_prompt, c_sample, t5_table, norm1_g, norm2_g, w_ada, b_ada, w_in,
           out_norm_a, out_norm_b, w_out, na_rpb, w_gate, w_up, w_down, final_g):
    bp, tp, d = x_prompt.shape
    bs, ts, _ = x_sample.shape
    seq_lens = (tp,) * bp + (ts,) * bs
    c = jnp.concatenate([c_prompt, c_sample], axis=0)
    y_prompt, y_sample = _trunk(x_prompt.reshape(bp * tp, d), x_sample.reshape(bs * ts, d), c, seq_lens,
                                t5_table, norm1_g, norm2_g, w_ada, b_ada, w_in, out_norm_a, out_norm_b,
                                w_out, na_rpb, w_gate, w_up, w_down, final_g)
    return (y_prompt.reshape(bp, tp, d), y_sample.reshape(bs, ts, d))
```

```python
import functools
import math

import numpy as np
import jax
import jax.numpy as jnp
from jax import lax
from jax.experimental import pallas as pl
from jax.experimental.pallas import tpu as pltpu

D_MODEL = 2048
HEAD_DIM = 128
N_HEADS_A = 8
N_HEADS_B = 8
D_A = N_HEADS_A * HEAD_DIM
D_B = N_HEADS_B * HEAD_DIM
QK_SCALE = HEAD_DIM ** -0.5
DILATIONS = (1, 4, 16)
RADIUS = 64
N_BUCKETS = 32
MAX_DISTANCE = 1024
GRID_W = 64
NA_ROWS = 8
NA_COLS = 16
EPS = 1e-6
NEG = -1e30

DIL_BLOCK = 1024
DIL_VIEW = 16
VIEW_ROWS = DIL_BLOCK // DIL_VIEW
TILES = DIL_BLOCK // RADIUS
D4_GROUPS = DIL_VIEW // 4
D4_ROWS = RADIUS // D4_GROUPS
D4_BLOCKS = VIEW_ROWS // D4_ROWS
DIL_HEADS_PER_STEP = 2
NA_BLOCK_ROWS = 16
NA_HALO_ROWS = 8
NA_BLOCK = NA_BLOCK_ROWS * GRID_W
NA_HEADS_PER_STEP = 2
NA_HALO = NA_HALO_ROWS * GRID_W
NA_WINDOW = NA_BLOCK + 2 * NA_HALO

F32 = jnp.float32
BF16 = jnp.bfloat16
MIB = 1024 * 1024


def _cparams(sem, vmem_mib):
    return pltpu.CompilerParams(dimension_semantics=sem, vmem_limit_bytes=vmem_mib * MIB)


def _seq_index(i, starts):
    idx = 0
    for s in starts[1:]:
        idx = idx + (i >= s).astype(jnp.int32)
    return idx


def _lanes(r):
    return slice(r * HEAD_DIM, (r + 1) * HEAD_DIM)


def _ada_kernel(c_ref, w_ref, b_ref, o_ref):
    c = c_ref[...]
    s = c / (1.0 + jnp.exp(-c))
    o_ref[0] = jnp.dot(s, w_ref[0], precision=lax.Precision.HIGHEST,
                       preferred_element_type=F32) + b_ref[0]


def _ada_modulation(c_pad, w_ada, b_ada):
    depth, d, n = w_ada.shape
    tn = 1024
    return pl.pallas_call(
        _ada_kernel,
        grid=(depth, n // tn),
        in_specs=[pl.BlockSpec((8, d), lambda l, j: (0, 0)),
                  pl.BlockSpec((1, d, tn), lambda l, j: (l, 0, j)),
                  pl.BlockSpec((1, 1, tn), lambda l, j: (l, 0, j))],
        out_specs=pl.BlockSpec((1, 8, tn), lambda l, j: (l, 0, j)),
        out_shape=jax.ShapeDtypeStruct((depth, 8, n), F32),
        compiler_params=_cparams(("parallel", "parallel"), 40),
        name="ada_modulation",
    )(c_pad, w_ada, b_ada.reshape(depth, 1, n))


def _modulated_norm(x, g, shift, scale):
    y = x * lax.rsqrt(jnp.mean(x * x, axis=-1, keepdims=True) + EPS) * g
    return y * (1.0 + scale) + shift


def _prenorm_kernel(xa_ref, xb_ref, mod_ref, g_ref, x_ref, h_ref, *, split_block):
    x = jnp.where(pl.program_id(0) < split_block, xa_ref[...], xb_ref[...])
    x_ref[...] = x
    h = _modulated_norm(x, g_ref[...], mod_ref[0, 0:1, :], mod_ref[0, 1:2, :])
    h_ref[...] = h.astype(BF16)


def _prenorm(xa, xb, mod_l, g, seq_lens):
    d = xa.shape[1]
    nt = xa.shape[0] + xb.shape[0]
    tm = 512
    split_block = xa.shape[0] // tm
    starts = tuple(int(s) // tm for s in np.cumsum((0,) + seq_lens[:-1]))
    row_blk = pl.BlockSpec((tm, d), lambda i: (i, 0))
    return pl.pallas_call(
        functools.partial(_prenorm_kernel, split_block=split_block),
        grid=(nt // tm,),
        in_specs=[pl.BlockSpec((tm, d), lambda i: (jnp.minimum(i, split_block - 1), 0)),
                  pl.BlockSpec((tm, d), lambda i: (jnp.maximum(i - split_block, 0), 0)),
                  pl.BlockSpec((1, 6, d), lambda i: (_seq_index(i, starts), 0, 0)),
                  pl.BlockSpec((1, d), lambda i: (0, 0))],
        out_specs=[row_blk, row_blk],
        out_shape=[jax.ShapeDtypeStruct((nt, d), F32), jax.ShapeDtypeStruct((nt, d), BF16)],
        compiler_params=_cparams(("parallel",), 40),
        name="prenorm",
    )(xa, xb, mod_l, g)


SLOT_K, SLOT_V, SLOT_Q = range(3)
KV_BLOCK = SLOT_K // 2


def _qkv_kernel(h_ref, cs_ref, w_ref, o_ref, *view, tm):
    r = jnp.dot(h_ref[...], w_ref[...], preferred_element_type=F32) * cs_ref[...]
    for c in range(o_ref.shape[0]):
        o_ref[c, 0] = r[:, _lanes(c)].astype(BF16)
    if view:
        ov_ref, r_s, a_s = view
        for c in range(o_ref.shape[0]):
            r_s[c] = r[:, _lanes(c)]
            for r4 in range(4):
                a_s[c, r4] = r_s[c, pl.ds(r4, tm // 4, stride=4), :]
            for r4 in range(4):
                for a in range(4):
                    rows = a_s[c, r4, pl.ds(a, tm // DIL_VIEW, stride=4), :]
                    ov_ref[c, 0, :, _lanes(4 * a + r4)] = rows.astype(BF16)


def _qkv_proj(h, col_scale, w, layer, group, with_view):
    nt, d = h.shape
    tm = 1024
    width = N_HEADS_A * HEAD_DIM
    assert w.shape[2] == 6 * width and N_HEADS_A == N_HEADS_B
    slot = lambda j: lax.rem(j + 2, 3)
    out_specs = [pl.BlockSpec((N_HEADS_A, 1, tm, HEAD_DIM), lambda i, j: (0, slot(j), i, 0))]
    out_shape = [jax.ShapeDtypeStruct((N_HEADS_A, 3, nt, HEAD_DIM), BF16)]
    scratch = []
    if with_view:
        out_specs.append(pl.BlockSpec((N_HEADS_A, 1, tm // DIL_VIEW, DIL_VIEW * HEAD_DIM),
                                      lambda i, j: (0, slot(j), i, 0)))
        out_shape.append(jax.ShapeDtypeStruct((N_HEADS_A, 3, nt // DIL_VIEW, DIL_VIEW * HEAD_DIM), BF16))
        scratch.append(pltpu.VMEM((N_HEADS_A, tm, HEAD_DIM), F32))
        scratch.append(pltpu.VMEM((N_HEADS_A, 4, tm // 4, HEAD_DIM), F32))
    return pl.pallas_call(
        functools.partial(_qkv_kernel, tm=tm),
        grid=(nt // tm, 3),
        in_specs=[pl.BlockSpec((tm, d), lambda i, j: (i, 0)),
                  pl.BlockSpec((1, width), lambda i, j: (0, 3 * group + j)),
                  pl.BlockSpec((None, d, width), lambda i, j: (layer, 0, 3 * group + j))],
        out_specs=out_specs,
        out_shape=out_shape,
        scratch_shapes=scratch,
        compiler_params=_cparams(("parallel", "parallel"), 48),
        name="qkv_proj_view" if with_view else "qkv_proj",
    )(h, col_scale, w)


def _t5_bucket(rel):
    nb = N_BUCKETS // 2
    exact = nb // 2
    n = np.abs(rel)
    sign = np.where(rel > 0, nb, 0)
    large = exact + (np.log(np.maximum(n, 1) / exact) / math.log(MAX_DISTANCE / exact) * (nb - exact)).astype(np.int64)
    large = np.minimum(large, nb - 1)
    return (sign + np.where(n < exact, n, large)).astype(np.int32)


def _tile_offsets():
    i = np.arange(RADIUS)[:, None]
    j = np.arange(3 * RADIUS)[None, :]
    band = j - RADIUS - i
    qa, qi = i // D4_ROWS, i % D4_ROWS
    ka, kj = j // (3 * D4_ROWS), j % (3 * D4_ROWS)
    perm = D4_GROUPS * (kj - D4_ROWS - qi) + (ka - qa)
    return (band, perm, band)


def _dilated_bias(t5_table):
    table = t5_table.astype(F32).T
    tiles = []
    for dil, rel in zip(DILATIONS, _tile_offsets()):
        inside = np.abs(rel) <= RADIUS
        bucket = _t5_bucket(rel * dil)
        b = jnp.full((table.shape[0],) + rel.shape, NEG, F32)
        for o in range(N_BUCKETS):
            b = jnp.where((inside & (bucket == o))[None], table[:, o, None, None], b)
        tiles.append(b)
    return jnp.stack(tiles, axis=1)


def _view_rows(ref_p, ref_c, ref_n, lo, hi, lanes):
    parts = []
    if lo < 0:
        parts.append(ref_p[VIEW_ROWS + lo:VIEW_ROWS, lanes])
    parts.append(ref_c[max(lo, 0):min(hi, VIEW_ROWS), lanes])
    if hi > VIEW_ROWS:
        parts.append(ref_n[0:hi - VIEW_ROWS, lanes])
    return parts


def _dil_kernel(q1_blk, kv1p_blk, kv1c_blk, kv1n_blk, q16_blk, kv16p_blk, kv16c_blk, kv16n_blk,
                bias_ref, o_ref, s_all, e_all, pos_all, num_all, m_all_s, l_all, kt_all,
                *, first_blocks, last_blocks):
    n = pl.program_id(1)
    first = functools.reduce(jnp.logical_or, [n == s for s in first_blocks])
    last = functools.reduce(jnp.logical_or, [n == s for s in last_blocks])
    fpen = jnp.where(first, NEG, 0.0).astype(F32)
    lpen = jnp.where(last, NEG, 0.0).astype(F32)
    col = lax.broadcasted_iota(jnp.int32, (1, 3 * RADIUS), 1)
    pen_first = jnp.where(col < RADIUS, fpen, 0.0)
    pen_last = jnp.where(col >= 2 * RADIUS, lpen, 0.0)
    perm_row = lax.rem(col, 3 * D4_ROWS)
    pen_perm_first = jnp.where(perm_row < D4_ROWS, fpen, 0.0)
    pen_perm_last = jnp.where(perm_row >= 2 * D4_ROWS, lpen, 0.0)
    ones = jnp.ones((3 * RADIUS, HEAD_DIM), BF16)

    def keys_d1(refs, b):
        ref_p, ref_c, ref_n = refs
        if b == 0:
            return jnp.concatenate([ref_p[...], ref_c[0:2 * RADIUS]], axis=0)
        if b == TILES - 1:
            return jnp.concatenate([ref_c[DIL_BLOCK - 2 * RADIUS:DIL_BLOCK], ref_n[...]], axis=0)
        return ref_c[RADIUS * (b - 1):RADIUS * (b + 2)]

    def keys_d4(refs, r4, b):
        parts = []
        for a in range(D4_GROUPS):
            parts += _view_rows(*refs, D4_ROWS * (b - 1), D4_ROWS * (b + 2), _lanes(D4_GROUPS * a + r4))
        return jnp.concatenate(parts, axis=0)

    def keys_d16(refs, r):
        return jnp.concatenate([ref[:, _lanes(r)] for ref in refs], axis=0)

    def head_passes(hh):
        q1_ref, q16_ref = q1_blk.at[hh, 0], q16_blk.at[hh, 0]
        k1 = tuple(blk.at[hh, 0] for blk in (kv1p_blk, kv1c_blk, kv1n_blk))
        v1 = tuple(blk.at[hh, 1] for blk in (kv1p_blk, kv1c_blk, kv1n_blk))
        k16 = tuple(blk.at[hh, 0] for blk in (kv16p_blk, kv16c_blk, kv16n_blk))
        v16 = tuple(blk.at[hh, 1] for blk in (kv16p_blk, kv16c_blk, kv16n_blk))
        s_s, e_s, pos_s = s_all.at[hh], e_all.at[hh], pos_all.at[hh]
        num_s, m_s, l_s, kt_s = num_all.at[hh], m_all_s.at[hh], l_all.at[hh], kt_all.at[hh]

        def tile_operands(p, t):
            if p == 0:
                pen = pen_first if t == 0 else pen_last if t == TILES - 1 else None
                return q1_ref[RADIUS * t:RADIUS * (t + 1)], keys_d1(k1, t), keys_d1(v1, t), pen
            if p == 1:
                r4, b = divmod(t, D4_BLOCKS)
                q = jnp.concatenate([q16_ref[D4_ROWS * b:D4_ROWS * (b + 1), _lanes(D4_GROUPS * a + r4)]
                                     for a in range(D4_GROUPS)], axis=0)
                pen = pen_perm_first if b == 0 else pen_perm_last if b == D4_BLOCKS - 1 else None
                return q, keys_d4(k16, r4, b), keys_d4(v16, r4, b), pen
            return q16_ref[:, _lanes(t)], keys_d16(k16, t), keys_d16(v16, t), pen_first + pen_last

        def store_rows(p, t, tile, pos_idx, view_ref):
            if p == 0:
                pos_s[pos_idx, RADIUS * t:RADIUS * (t + 1), :] = tile
            elif p == 1:
                r4, b = divmod(t, D4_BLOCKS)
                for a in range(D4_GROUPS):
                    rows = slice(D4_ROWS * b, D4_ROWS * (b + 1))
                    view_ref[0, rows, _lanes(D4_GROUPS * a + r4)] = tile[D4_ROWS * a:D4_ROWS * (a + 1)]
            else:
                view_ref[1, :, _lanes(t)] = tile

        def transpose_keys(p, t):
            kt_s[p, t] = tile_operands(p, t)[1].T

        def logits(p, t):
            q, _, _, pen = tile_operands(p, t)
            s = jnp.dot(q, kt_s[p, t], preferred_element_type=F32)
            s = s + bias_ref[hh, p]
            s_s[p, t] = s if pen is None else s + pen

        def softmax(p, t):
            s = s_s[p, t]
            m = jnp.max(s, axis=-1, keepdims=True)
            e_s[p, t] = jnp.exp(s - m).astype(BF16)
            store_rows(p, t, jnp.broadcast_to(m, (RADIUS, HEAD_DIM)), 1, m_s)

        def values(p, t):
            _, _, v, _ = tile_operands(p, t)
            nv = jnp.dot(e_s[p, t], jnp.concatenate([v, ones], axis=1), preferred_element_type=F32)
            store_rows(p, t, nv[:, :HEAD_DIM], 0, num_s)
            store_rows(p, t, nv[:, HEAD_DIM:], 2, l_s)

        def merge():
            for r in range(DIL_VIEW):
                strided = pl.ds(r, VIEW_ROWS, stride=DIL_VIEW)
                nums = (pos_s[0, strided, :], num_s[0, :, _lanes(r)], num_s[1, :, _lanes(r)])
                ms = (pos_s[1, strided, :], m_s[0, :, _lanes(r)], m_s[1, :, _lanes(r)])
                ls = (pos_s[2, strided, :], l_s[0, :, _lanes(r)], l_s[1, :, _lanes(r)])
                m_max = jnp.maximum(jnp.maximum(ms[0], ms[1]), ms[2])
                ws = [jnp.exp(m - m_max) for m in ms]
                num = ws[0] * nums[0] + ws[1] * nums[1] + ws[2] * nums[2]
                den = ws[0] * ls[0] + ws[1] * ls[1] + ws[2] * ls[2]
                o_ref[hh, strided, :] = num / den

        def all_tiles(p, *tile_stages):
            return lambda: [stage(p, t) for stage in tile_stages for t in range(TILES)]

        return [all_tiles(2, transpose_keys, logits), all_tiles(1, transpose_keys, logits),
                all_tiles(2, softmax), all_tiles(2, values),
                all_tiles(0, transpose_keys, logits), all_tiles(1, softmax), all_tiles(1, values),
                all_tiles(0, softmax), all_tiles(0, values), merge]

    first_head, second_head = head_passes(0), head_passes(1)
    for run in first_head[:7] + second_head[:2] + first_head[7:9] + second_head[2:5] + first_head[9:] + second_head[5:]:
        run()


def _dilated_attention(qkv, qkv16, bias, seq_lens):
    nt = qkv.shape[2]
    nblk = nt // DIL_BLOCK
    bounds = np.cumsum((0,) + seq_lens) // DIL_BLOCK
    first_blocks = tuple(int(b) for b in bounds[:-1])
    last_blocks = tuple(int(b) - 1 for b in bounds[1:])
    per = DIL_BLOCK // RADIUS
    prev1 = lambda n: jnp.maximum(per * n - 1, 0)
    next1 = lambda n: jnp.minimum(per * (n + 1), nt // RADIUS - 1)
    prev16 = lambda n: jnp.maximum(n - 1, 0)
    next16 = lambda n: jnp.minimum(n + 1, nblk - 1)
    kv_blk = KV_BLOCK
    hps = DIL_HEADS_PER_STEP
    blk16 = (hps, 1, VIEW_ROWS, DIL_VIEW * HEAD_DIM)
    kv16 = (hps, 2, VIEW_ROWS, DIL_VIEW * HEAD_DIM)
    npat = len(DILATIONS)
    kernel = functools.partial(_dil_kernel, first_blocks=first_blocks, last_blocks=last_blocks)
    out = pl.pallas_call(
        kernel,
        grid=(N_HEADS_A // hps, nblk),
        in_specs=[pl.BlockSpec((hps, 1, DIL_BLOCK, HEAD_DIM), lambda h, n: (h, SLOT_Q, n, 0)),
                  pl.BlockSpec((hps, 2, RADIUS, HEAD_DIM), lambda h, n: (h, kv_blk, prev1(n), 0)),
                  pl.BlockSpec((hps, 2, DIL_BLOCK, HEAD_DIM), lambda h, n: (h, kv_blk, n, 0)),
                  pl.BlockSpec((hps, 2, RADIUS, HEAD_DIM), lambda h, n: (h, kv_blk, next1(n), 0)),
                  pl.BlockSpec(blk16, lambda h, n: (h, SLOT_Q, n, 0)),
                  pl.BlockSpec(kv16, lambda h, n: (h, kv_blk, prev16(n), 0)),
                  pl.BlockSpec(kv16, lambda h, n: (h, kv_blk, n, 0)),
                  pl.BlockSpec(kv16, lambda h, n: (h, kv_blk, next16(n), 0)),
                  pl.BlockSpec((hps, npat, RADIUS, 3 * RADIUS), lambda h, n: (h, 0, 0, 0))],
        out_specs=pl.BlockSpec((hps, DIL_BLOCK, HEAD_DIM), lambda h, n: (h, n, 0)),
        out_shape=jax.ShapeDtypeStruct((N_HEADS_A, nt, HEAD_DIM), F32),
        scratch_shapes=[pltpu.VMEM((hps, npat, TILES, RADIUS, 3 * RADIUS), F32),
                        pltpu.VMEM((hps, npat, TILES, RADIUS, 3 * RADIUS), BF16),
                        pltpu.VMEM((hps, 3, DIL_BLOCK, HEAD_DIM), F32),
                        pltpu.VMEM((hps, 2, VIEW_ROWS, DIL_VIEW * HEAD_DIM), F32),
                        pltpu.VMEM((hps, 2, VIEW_ROWS, DIL_VIEW * HEAD_DIM), F32),
                        pltpu.VMEM((hps, 2, VIEW_ROWS, DIL_VIEW * HEAD_DIM), F32),
                        pltpu.VMEM((hps, npat, TILES, HEAD_DIM, 3 * RADIUS), BF16)],
        compiler_params=_cparams(("parallel", "parallel"), 48),
        name="dilated_attention",
    )(qkv, qkv, qkv, qkv, qkv16, qkv16, qkv16, qkv16, bias)
    return out


def _na_bias(rpb):
    c = np.arange(GRID_W)
    cstart = np.clip(c - NA_COLS // 2, 0, GRID_W - NA_COLS)
    cmask = (c[None, :] >= cstart[:, None]) & (c[None, :] < cstart[:, None] + NA_COLS)
    coff = np.clip(c[None, :] - c[:, None], -(NA_COLS - 1), NA_COLS - 1) + NA_COLS - 1
    cols = jnp.full(rpb.shape[:2] + (GRID_W, GRID_W), NEG, F32)
    for o in range(2 * NA_COLS - 1):
        cols = jnp.where((cmask & (coff == o))[None, None], rpb[:, :, o, None, None].astype(F32), cols)
    tiles = [cols[:, NA_ROWS - 1 - dd:2 * NA_ROWS - 1 - dd] for dd in range(NA_ROWS)]
    b = jnp.stack(tiles, axis=1).transpose(0, 1, 3, 2, 4)
    return b.reshape(rpb.shape[0], NA_ROWS, GRID_W, NA_ROWS * GRID_W)


def _na_kernel(q_blk, kvp_blk, kvc_blk, kvn_blk, bias_ref, o_ref, kw, vw,
               s_s, e_s, *, block_starts, seq_rows):
    n = pl.program_id(1)
    nstart = jnp.int32(block_starts[0])
    rows = jnp.int32(seq_rows[0])
    for s, r in zip(block_starts[1:], seq_rows[1:]):
        nstart = jnp.where(n >= s, s, nstart)
        rows = jnp.where(n >= s, r, rows)

    nkeys = NA_ROWS * GRID_W
    shifts, starts = [], []
    for i in range(NA_BLOCK_ROWS):
        r = (n - nstart) * NA_BLOCK_ROWS + i
        dd = r - jnp.clip(r - NA_ROWS // 2, 0, rows - NA_ROWS)
        shifts.append(dd)
        starts.append(pl.multiple_of((NA_HALO_ROWS + i - dd) * GRID_W, GRID_W))
    ones = jnp.ones((nkeys, HEAD_DIM), BF16)

    def head_passes(hh):
        q_ref = q_blk.at[hh, 0]

        def logits_pass():
            for which, win in enumerate((kw, vw)):
                win[hh, 0:NA_HALO, :] = kvp_blk[hh, which]
                win[hh, NA_HALO:NA_HALO + NA_BLOCK, :] = kvc_blk[hh, which]
                win[hh, NA_HALO + NA_BLOCK:NA_WINDOW, :] = kvn_blk[hh, which]
            for i in range(NA_BLOCK_ROWS):
                k = kw[hh, pl.ds(starts[i], nkeys), :]
                q = q_ref[i * GRID_W:(i + 1) * GRID_W, :]
                s = lax.dot_general(q, k, (((1,), (1,)), ((), ())), preferred_element_type=F32)
                s_s[hh, i] = s + bias_ref[hh, shifts[i]]

        def softmax_pass():
            for i in range(NA_BLOCK_ROWS):
                s = s_s[hh, i]
                e_s[hh, i] = jnp.exp(s - jnp.max(s, axis=-1, keepdims=True)).astype(BF16)

        def values_pass():
            for i in range(NA_BLOCK_ROWS):
                v = vw[hh, pl.ds(starts[i], nkeys), :]
                nv = jnp.dot(e_s[hh, i], jnp.concatenate([v, ones], axis=1), preferred_element_type=F32)
                out = (nv[:, :HEAD_DIM] / nv[:, HEAD_DIM:]).astype(o_ref.dtype)
                o_ref[i * GRID_W:(i + 1) * GRID_W, _lanes(hh)] = out

        return [logits_pass, softmax_pass, values_pass]

    first_head, second_head = head_passes(0), head_passes(1)
    for run in (first_head[0], second_head[0], first_head[1], first_head[2], second_head[1], second_head[2]):
        run()


def _neighborhood_attention(qkv, bias, seq_lens):
    nt = qkv.shape[2]
    block_starts = tuple(int(s) // NA_BLOCK for s in np.cumsum((0,) + seq_lens[:-1]))
    seq_rows = tuple(int(s) // GRID_W for s in seq_lens)
    assert all(r >= NA_ROWS and r % NA_BLOCK_ROWS == 0 for r in seq_rows)
    per = NA_BLOCK // NA_HALO
    prev = lambda n: jnp.maximum(per * n - 1, 0)
    nxt = lambda n: jnp.minimum(per * (n + 1), nt // NA_HALO - 1)
    kv_blk = KV_BLOCK
    kernel = functools.partial(_na_kernel, block_starts=block_starts, seq_rows=seq_rows)
    nkeys = NA_ROWS * GRID_W
    hps = NA_HEADS_PER_STEP
    return pl.pallas_call(
        kernel,
        grid=(N_HEADS_B // hps, nt // NA_BLOCK),
        in_specs=[pl.BlockSpec((hps, 1, NA_BLOCK, HEAD_DIM), lambda h, n: (h, SLOT_Q, n, 0)),
                  pl.BlockSpec((hps, 2, NA_HALO, HEAD_DIM), lambda h, n: (h, kv_blk, prev(n), 0)),
                  pl.BlockSpec((hps, 2, NA_BLOCK, HEAD_DIM), lambda h, n: (h, kv_blk, n, 0)),
                  pl.BlockSpec((hps, 2, NA_HALO, HEAD_DIM), lambda h, n: (h, kv_blk, nxt(n), 0)),
                  pl.BlockSpec((hps, NA_ROWS, GRID_W, nkeys), lambda h, n: (h, 0, 0, 0))],
        out_specs=pl.BlockSpec((NA_BLOCK, hps * HEAD_DIM), lambda h, n: (n, h)),
        out_shape=jax.ShapeDtypeStruct((nt, D_B), BF16),
        scratch_shapes=[pltpu.VMEM((hps, NA_WINDOW, HEAD_DIM), BF16),
                        pltpu.VMEM((hps, NA_WINDOW, HEAD_DIM), BF16),
                        pltpu.VMEM((hps, NA_BLOCK_ROWS, GRID_W, nkeys), F32),
                        pltpu.VMEM((hps, NA_BLOCK_ROWS, GRID_W, nkeys), BF16)],
        compiler_params=_cparams(("parallel", "parallel"), 40),
        name="neighborhood_attention",
    )(qkv, qkv, qkv, qkv, bias)


def _plain_norm(y, g):
    y = y.astype(F32)
    return (y * lax.rsqrt(jnp.mean(y * y, axis=-1, keepdims=True) + EPS) * g).astype(BF16)


OUT_ROW_CHUNK = 256


def _out_kernel(ya_ref, yb_ref, ga_ref, gb_ref, w_ref, x_ref, mod_ref, g2_ref, xo_ref, h_ref):
    for r0 in range(0, x_ref.shape[0], OUT_ROW_CHUNK):
        rows = slice(r0, r0 + OUT_ROW_CHUNK)
        ya = jnp.concatenate([ya_ref[h, rows, :] for h in range(N_HEADS_A)], axis=1)
        y = jnp.dot(_plain_norm(ya, ga_ref[...]), w_ref[:D_A, :], preferred_element_type=F32)
        y = y + jnp.dot(_plain_norm(yb_ref[rows, :], gb_ref[...]), w_ref[D_A:, :],
                        preferred_element_type=F32)
        x = x_ref[rows, :] + mod_ref[0, 2:3, :] * y
        xo_ref[rows, :] = x
        h = _modulated_norm(x, g2_ref[...], mod_ref[0, 3:4, :], mod_ref[0, 4:5, :])
        h_ref[rows, :] = h.astype(BF16)


def _out_proj(ya, yb, ga, gb, w, layer, x, mod_l, g2, seq_lens):
    nt, d = x.shape
    tm = 512
    starts = tuple(int(s) // tm for s in np.cumsum((0,) + seq_lens[:-1]))
    const = lambda i: (0, 0)
    return pl.pallas_call(
        _out_kernel,
        grid=(nt // tm,),
        in_specs=[pl.BlockSpec((N_HEADS_A, tm, HEAD_DIM), lambda i: (0, i, 0)),
                  pl.BlockSpec((tm, D_B), lambda i: (i, 0)),
                  pl.BlockSpec((1, D_A), const),
                  pl.BlockSpec((1, D_B), const),
                  pl.BlockSpec((None, D_A + D_B, d), lambda i: (layer, 0, 0), pipeline_mode=pl.Buffered(1)),
                  pl.BlockSpec((tm, d), lambda i: (i, 0)),
                  pl.BlockSpec((1, 6, d), lambda i: (_seq_index(i, starts), 0, 0)),
                  pl.BlockSpec((1, d), const)],
        out_specs=[pl.BlockSpec((tm, d), lambda i: (i, 0)),
                   pl.BlockSpec((tm, d), lambda i: (i, 0))],
        out_shape=[jax.ShapeDtypeStruct((nt, d), F32), jax.ShapeDtypeStruct((nt, d), BF16)],
        compiler_params=_cparams(("parallel",), 48),
        name="out_proj",
    )(ya, yb, ga, gb, w, x, mod_l, g2)


def _ffn_kernel(h_ref, wg_ref, wu_ref, wd_ref, x_ref, mod_ref, ng_ref, *refs, split_block):
    acc_ref = refs[-1]
    f = pl.program_id(1)

    @pl.when(f == 0)
    def _():
        acc_ref[...] = jnp.zeros_like(acc_ref)

    h = h_ref[...]
    a = jnp.dot(h, wg_ref[...], preferred_element_type=F32)
    b = jnp.dot(h, wu_ref[...], preferred_element_type=F32)
    act = (a / (1.0 + jnp.exp(-a)) * b).astype(BF16)
    acc_ref[...] += jnp.dot(act, wd_ref[...], preferred_element_type=F32)

    @pl.when(f == pl.num_programs(1) - 1)
    def _():
        x = x_ref[...] + mod_ref[0, 5:6, :] * acc_ref[...]
        if split_block is not None:
            y0_ref, y1_ref = refs[:-1]
            y = x * lax.rsqrt(jnp.mean(x * x, axis=-1, keepdims=True) + EPS) * ng_ref[...]
            i = pl.program_id(0)

            @pl.when(i < split_block)
            def _():
                y0_ref[...] = y

            @pl.when(i >= split_block)
            def _():
                y1_ref[...] = y
        else:
            nmod_ref, xo_ref, hn_ref = refs[:-1]
            xo_ref[...] = x
            hn = _modulated_norm(x, ng_ref[...], nmod_ref[0, 0:1, :], nmod_ref[0, 1:2, :])
            hn_ref[...] = hn.astype(BF16)


def _ffn(h, wg, wu, wd, layer, x, mod_l, next_g, next_mod, seq_lens, out_split=None):
    nt, d = x.shape
    dff = wg.shape[2]
    tm, tf = 512, 512
    final = next_mod is None
    split_block = out_split // tm if final else None
    starts = tuple(int(s) // tm for s in np.cumsum((0,) + seq_lens[:-1]))
    row_blk = pl.BlockSpec((tm, d), lambda i, f: (i, 0))
    mod_blk = pl.BlockSpec((1, 6, d), lambda i, f: (_seq_index(i, starts), 0, 0))
    in_specs = [row_blk,
                pl.BlockSpec((None, d, tf), lambda i, f: (layer, 0, f)),
                pl.BlockSpec((None, d, tf), lambda i, f: (layer, 0, f)),
                pl.BlockSpec((None, tf, d), lambda i, f: (layer, f, 0)),
                row_blk,
                mod_blk,
                pl.BlockSpec((1, d), lambda i, f: (0, 0))]
    operands = [h, wg, wu, wd, x, mod_l, next_g]
    if final:
        out_specs = [pl.BlockSpec((tm, d), lambda i, f: (jnp.minimum(i, split_block - 1), 0)),
                     pl.BlockSpec((tm, d), lambda i, f: (jnp.maximum(i - split_block, 0), 0))]
        out_shape = [jax.ShapeDtypeStruct((out_split, d), F32),
                     jax.ShapeDtypeStruct((nt - out_split, d), F32)]
    else:
        in_specs.append(mod_blk)
        operands.append(next_mod)
        out_specs = [row_blk, row_blk]
        out_shape = [jax.ShapeDtypeStruct((nt, d), F32), jax.ShapeDtypeStruct((nt, d), BF16)]
    return pl.pallas_call(
        functools.partial(_ffn_kernel, split_block=split_block),
        grid=(nt // tm, dff // tf),
        in_specs=in_specs,
        out_specs=out_specs,
        out_shape=out_shape,
        scratch_shapes=[pltpu.VMEM((tm, d), F32)],
        compiler_params=_cparams(("arbitrary" if final else "parallel", "arbitrary"), 56),
        name="ffn_final" if final else "ffn",
    )(*operands)


def _trunk(xa, xb, c, seq_lens, t5_table, norm1_g, norm2_g, w_ada, b_ada, w_in, out_norm_a,
           out_norm_b, w_out, na_rpb, w_gate, w_up, w_down, final_g):
    depth = w_in.shape[0]
    nseq = len(seq_lens)
    d = xa.shape[1]
    out_split = xa.shape[0]
    assert all(s % DIL_BLOCK == 0 for s in seq_lens)

    c_pad = jnp.zeros((8, d), F32).at[:nseq].set(c)
    mod = _ada_modulation(c_pad, w_ada, b_ada)[:, :nseq].reshape(depth, nseq, 6, d)

    col_scale = jnp.ones((6, D_A), F32).at[0].set(QK_SCALE).at[3].set(QK_SCALE).reshape(1, 6 * D_A)
    dil_bias = _dilated_bias(t5_table)
    w_in, w_out, w_gate, w_up, w_down = (w.astype(BF16) for w in (w_in, w_out, w_gate, w_up, w_down))
    x, h = _prenorm(xa, xb, mod[0], norm1_g[0][None], seq_lens)
    for l in range(depth):
        final = l == depth - 1
        qkv_a, qkv_a16 = _qkv_proj(h, col_scale, w_in, l, 0, True)
        qkv_b, = _qkv_proj(h, col_scale, w_in, l, 1, False)
        ya = _dilated_attention(qkv_a, qkv_a16, dil_bias, seq_lens)
        yb = _neighborhood_attention(qkv_b, _na_bias(na_rpb[l]), seq_lens)
        x, h2 = _out_proj(ya, yb, out_norm_a[l][None], out_norm_b[l][None], w_out, l,
                          x, mod[l], norm2_g[l][None], seq_lens)
        next_g = final_g[None] if final else norm1_g[l + 1][None]
        next_mod = None if final else mod[l + 1]
        out = _ffn(h2, w_gate, w_up, w_down, l, x, mod[l], next_g, next_mod, seq_lens, out_split)
        if not final:
            x, h = out
    return out


def kernel(x_prompt, x_sample, c_prompt, c_sample, t5_table, norm1_g, norm2_g, w_ada, b_ada, w_in,
           out_norm_a, out_norm_b, w_out, na_rpb, w_gate, w_up, w_down, final_g):
    bp, tp, d = x_prompt.shape
    bs, ts, _ = x_sample.shape
    seq_lens = (tp,) * bp + (ts,) * bs
    c = jnp.concatenate([c_prompt, c_sample], axis=0)
    y_prompt, y_sample = _trunk(x_prompt.reshape(bp * tp, d), x_sample.reshape(bs * ts, d), c, seq_lens,
                                t5_table, norm1_g, norm2_g, w_ada, b_ada, w_in, out_norm_a, out_norm_b,
                                w_out, na_rpb, w_gate, w_up, w_down, final_g)
    return (y_prompt.reshape(bp, tp, d), y_sample.reshape(bs, ts, d))
```

```python
import functools
import math

import numpy as np
import jax
import jax.numpy as jnp
from jax import lax
from jax.experimental import pallas as pl
from jax.experimental.pallas import tpu as pltpu

D_MODEL = 2048
HEAD_DIM = 128
N_HEADS_A = 8
N_HEADS_B = 8
D_A = N_HEADS_A * HEAD_DIM
D_B = N_HEADS_B * HEAD_DIM
QK_SCALE = HEAD_DIM ** -0.5
DILATIONS = (1, 4, 16)
RADIUS = 64
N_BUCKETS = 32
MAX_DISTANCE = 1024
GRID_W = 64
NA_ROWS = 8
NA_COLS = 16
EPS = 1e-6
NEG = -1e30

DIL_BLOCK = 1024
DIL_VIEW = 16
VIEW_ROWS = DIL_BLOCK // DIL_VIEW
TILES = DIL_BLOCK // RADIUS
D4_GROUPS = DIL_VIEW // 4
D4_ROWS = RADIUS // D4_GROUPS
D4_BLOCKS = VIEW_ROWS // D4_ROWS
DIL_HEADS_PER_STEP = 2
NA_BLOCK_ROWS = 16
NA_HALO_ROWS = 8
NA_BLOCK = NA_BLOCK_ROWS * GRID_W
NA_HEADS_PER_STEP = 2
NA_HALO = NA_HALO_ROWS * GRID_W
NA_WINDOW = NA_BLOCK + 2 * NA_HALO

F32 = jnp.float32
BF16 = jnp.bfloat16
MIB = 1024 * 1024


def _cparams(sem, vmem_mib):
    return pltpu.CompilerParams(dimension_semantics=sem, vmem_limit_bytes=vmem_mib * MIB)


def _seq_index(i, starts):
    idx = 0
    for s in starts[1:]:
        idx = idx + (i >= s).astype(jnp.int32)
    return idx


def _lanes(r):
    return slice(r * HEAD_DIM, (r + 1) * HEAD_DIM)


def _ada_kernel(c_ref, w_ref, b_ref, o_ref):
    c = c_ref[...]
    s = c / (1.0 + jnp.exp(-c))
    o_ref[0] = jnp.dot(s, w_ref[0], precision=lax.Precision.HIGHEST,
                       preferred_element_type=F32) + b_ref[0]


def _ada_modulation(c_pad, w_ada, b_ada):
    depth, d, n = w_ada.shape
    tn = 1024
    return pl.pallas_call(
        _ada_kernel,
        grid=(depth, n // tn),
        in_specs=[pl.BlockSpec((8, d), lambda l, j: (0, 0)),
                  pl.BlockSpec((1, d, tn), lambda l, j: (l, 0, j)),
                  pl.BlockSpec((1, 1, tn), lambda l, j: (l, 0, j))],
        out_specs=pl.BlockSpec((1, 8, tn), lambda l, j: (l, 0, j)),
        out_shape=jax.ShapeDtypeStruct((depth, 8, n), F32),
        compiler_params=_cparams(("parallel", "parallel"), 40),
        name="ada_modulation",
    )(c_pad, w_ada, b_ada.reshape(depth, 1, n))


def _modulated_norm(x, g, shift, scale):
    y = x * lax.rsqrt(jnp.mean(x * x, axis=-1, keepdims=True) + EPS) * g
    return y * (1.0 + scale) + shift


def _prenorm_kernel(xa_ref, xb_ref, mod_ref, g_ref, x_ref, h_ref, *, split_block):
    x = jnp.where(pl.program_id(0) < split_block, xa_ref[...], xb_ref[...])
    x_ref[...] = x
    h = _modulated_norm(x, g_ref[...], mod_ref[0, 0:1, :], mod_ref[0, 1:2, :])
    h_ref[...] = h.astype(BF16)


def _prenorm(xa, xb, mod_l, g, seq_lens):
    d = xa.shape[1]
    nt = xa.shape[0] + xb.shape[0]
    tm = 512
    split_block = xa.shape[0] // tm
    starts = tuple(int(s) // tm for s in np.cumsum((0,) + seq_lens[:-1]))
    row_blk = pl.BlockSpec((tm, d), lambda i: (i, 0))
    return pl.pallas_call(
        functools.partial(_prenorm_kernel, split_block=split_block),
        grid=(nt // tm,),
        in_specs=[pl.BlockSpec((tm, d), lambda i: (jnp.minimum(i, split_block - 1), 0)),
                  pl.BlockSpec((tm, d), lambda i: (jnp.maximum(i - split_block, 0), 0)),
                  pl.BlockSpec((1, 6, d), lambda i: (_seq_index(i, starts), 0, 0)),
                  pl.BlockSpec((1, d), lambda i: (0, 0))],
        out_specs=[row_blk, row_blk],
        out_shape=[jax.ShapeDtypeStruct((nt, d), F32), jax.ShapeDtypeStruct((nt, d), BF16)],
        compiler_params=_cparams(("parallel",), 40),
        name="prenorm",
    )(xa, xb, mod_l, g)


SLOT_K, SLOT_V, SLOT_Q = range(3)
KV_BLOCK = SLOT_K // 2


def _qkv_kernel(h_ref, cs_ref, w_ref, o_ref, *view, tm):
    r = jnp.dot(h_ref[...], w_ref[...], preferred_element_type=F32) * cs_ref[...]
    for c in range(o_ref.shape[0]):
        o_ref[c, 0] = r[:, _lanes(c)].astype(BF16)
    if view:
        ov_ref, r_s, a_s = view
        for c in range(o_ref.shape[0]):
            r_s[c] = r[:, _lanes(c)]
            for r4 in range(4):
                a_s[c, r4] = r_s[c, pl.ds(r4, tm // 4, stride=4), :]
            for r4 in range(4):
                for a in range(4):
                    rows = a_s[c, r4, pl.ds(a, tm // DIL_VIEW, stride=4), :]
                    ov_ref[c, 0, :, _lanes(4 * a + r4)] = rows.astype(BF16)


def _qkv_proj(h, col_scale, w, layer, group, with_view):
    nt, d = h.shape
    tm = 1024
    width = N_HEADS_A * HEAD_DIM
    assert w.shape[2] == 6 * width and N_HEADS_A == N_HEADS_B
    slot = lambda j: lax.rem(j + 2, 3)
    out_specs = [pl.BlockSpec((N_HEADS_A, 1, tm, HEAD_DIM), lambda i, j: (0, slot(j), i, 0))]
    out_shape = [jax.ShapeDtypeStruct((N_HEADS_A, 3, nt, HEAD_DIM), BF16)]
    scratch = []
    if with_view:
        out_specs.append(pl.BlockSpec((N_HEADS_A, 1, tm // DIL_VIEW, DIL_VIEW * HEAD_DIM),
                                      lambda i, j: (0, slot(j), i, 0)))
        out_shape.append(jax.ShapeDtypeStruct((N_HEADS_A, 3, nt // DIL_VIEW, DIL_VIEW * HEAD_DIM), BF16))
        scratch.append(pltpu.VMEM((N_HEADS_A, tm, HEAD_DIM), F32))
        scratch.append(pltpu.VMEM((N_HEADS_A, 4, tm // 4, HEAD_DIM), F32))
    return pl.pallas_call(
        functools.partial(_qkv_kernel, tm=tm),
        grid=(nt // tm, 3),
        in_specs=[pl.BlockSpec((tm, d), lambda i, j: (i, 0)),
                  pl.BlockSpec((1, width), lambda i, j: (0, 3 * group + j)),
                  pl.BlockSpec((None, d, width), lambda i, j: (layer, 0, 3 * group + j))],
        out_specs=out_specs,
        out_shape=out_shape,
        scratch_shapes=scratch,
        compiler_params=_cparams(("parallel", "parallel"), 48),
        name="qkv_proj_view" if with_view else "qkv_proj",
    )(h, col_scale, w)


def _t5_bucket(rel):
    nb = N_BUCKETS // 2
    exact = nb // 2
    n = np.abs(rel)
    sign = np.where(rel > 0, nb, 0)
    large = exact + (np.log(np.maximum(n, 1) / exact) / math.log(MAX_DISTANCE / exact) * (nb - exact)).astype(np.int64)
    large = np.minimum(large, nb - 1)
    return (sign + np.where(n < exact, n, large)).astype(np.int32)


def _tile_offsets():
    i = np.arange(RADIUS)[:, None]
    j = np.arange(3 * RADIUS)[None, :]
    band = j - RADIUS - i
    qa, qi = i // D4_ROWS, i % D4_ROWS
    ka, kj = j // (3 * D4_ROWS), j % (3 * D4_ROWS)
    perm = D4_GROUPS * (kj - D4_ROWS - qi) + (ka - qa)
    return (band, perm, band)


def _dilated_bias(t5_table):
    table = t5_table.astype(F32).T
    tiles = []
    for dil, rel in zip(DILATIONS, _tile_offsets()):
        inside = np.abs(rel) <= RADIUS
        bucket = _t5_bucket(rel * dil)
        b = jnp.full((table.shape[0],) + rel.shape, NEG, F32)
        for o in range(N_BUCKETS):
            b = jnp.where((inside & (bucket == o))[None], table[:, o, None, None], b)
        tiles.append(b)
    return jnp.stack(tiles, axis=1)


def _view_rows(ref_p, ref_c, ref_n, lo, hi, lanes):
    parts = []
    if lo < 0:
        parts.append(ref_p[VIEW_ROWS + lo:VIEW_ROWS, lanes])
    parts.append(ref_c[max(lo, 0):min(hi, VIEW_ROWS), lanes])
    if hi > VIEW_ROWS:
        parts.append(ref_n[0:hi - VIEW_ROWS, lanes])
    return parts


def _dil_kernel(q1_blk, kv1p_blk, kv1c_blk, kv1n_blk, q16_blk, kv16p_blk, kv16c_blk, kv16n_blk,
                bias_ref, o_ref, s_all, e_all, pos_all, num_all, m_all_s, l_all, kt_all,
                *, first_blocks, last_blocks):
    n = pl.program_id(1)
    first = functools.reduce(jnp.logical_or, [n == s for s in first_blocks])
    last = functools.reduce(jnp.logical_or, [n == s for s in last_blocks])
    fpen = jnp.where(first, NEG, 0.0).astype(F32)
    lpen = jnp.where(last, NEG, 0.0).astype(F32)
    col = lax.broadcasted_iota(jnp.int32, (1, 3 * RADIUS), 1)
    pen_first = jnp.where(col < RADIUS, fpen, 0.0)
    pen_last = jnp.where(col >= 2 * RADIUS, lpen, 0.0)
    perm_row = lax.rem(col, 3 * D4_ROWS)
    pen_perm_first = jnp.where(perm_row < D4_ROWS, fpen, 0.0)
    pen_perm_last = jnp.where(perm_row >= 2 * D4_ROWS, lpen, 0.0)
    ones = jnp.ones((3 * RADIUS, HEAD_DIM), BF16)

    def keys_d1(refs, b):
        ref_p, ref_c, ref_n = refs
        if b == 0:
            return jnp.concatenate([ref_p[...], ref_c[0:2 * RADIUS]], axis=0)
        if b == TILES - 1:
            return jnp.concatenate([ref_c[DIL_BLOCK - 2 * RADIUS:DIL_BLOCK], ref_n[...]], axis=0)
        return ref_c[RADIUS * (b - 1):RADIUS * (b + 2)]

    def keys_d4(refs, r4, b):
        parts = []
        for a in range(D4_GROUPS):
            parts += _view_rows(*refs, D4_ROWS * (b - 1), D4_ROWS * (b + 2), _lanes(D4_GROUPS * a + r4))
        return jnp.concatenate(parts, axis=0)

    def keys_d16(refs, r):
        return jnp.concatenate([ref[:, _lanes(r)] for ref in refs], axis=0)

    def head_passes(hh):
        q1_ref, q16_ref = q1_blk.at[hh, 0], q16_blk.at[hh, 0]
        k1 = tuple(blk.at[hh, 0] for blk in (kv1p_blk, kv1c_blk, kv1n_blk))
        v1 = tuple(blk.at[hh, 1] for blk in (kv1p_blk, kv1c_blk, kv1n_blk))
        k16 = tuple(blk.at[hh, 0] for blk in (kv16p_blk, kv16c_blk, kv16n_blk))
        v16 = tuple(blk.at[hh, 1] for blk in (kv16p_blk, kv16c_blk, kv16n_blk))
        s_s, e_s, pos_s = s_all.at[hh], e_all.at[hh], pos_all.at[hh]
        num_s, m_s, l_s, kt_s = num_all.at[hh], m_all_s.at[hh], l_all.at[hh], kt_all.at[hh]

        def tile_operands(p, t):
            if p == 0:
                pen = pen_first if t == 0 else pen_last if t == TILES - 1 else None
                return q1_ref[RADIUS * t:RADIUS * (t + 1)], keys_d1(k1, t), keys_d1(v1, t), pen
            if p == 1:
                r4, b = divmod(t, D4_BLOCKS)
                q = jnp.concatenate([q16_ref[D4_ROWS * b:D4_ROWS * (b + 1), _lanes(D4_GROUPS * a + r4)]
                                     for a in range(D4_GROUPS)], axis=0)
                pen = pen_perm_first if b == 0 else pen_perm_last if b == D4_BLOCKS - 1 else None
                return q, keys_d4(k16, r4, b), keys_d4(v16, r4, b), pen
            return q16_ref[:, _lanes(t)], keys_d16(k16, t), keys_d16(v16, t), pen_first + pen_last

        def store_rows(p, t, tile, pos_idx, view_ref):
            if p == 0:
                pos_s[pos_idx, RADIUS * t:RADIUS * (t + 1), :] = tile
            elif p == 1:
                r4, b = divmod(t, D4_BLOCKS)
                for a in range(D4_GROUPS):
                    rows = slice(D4_ROWS * b, D4_ROWS * (b + 1))
                    view_ref[0, rows, _lanes(D4_GROUPS * a + r4)] = tile[D4_ROWS * a:D4_ROWS * (a + 1)]
            else:
                view_ref[1, :, _lanes(t)] = tile

        def transpose_keys(p, t):
            kt_s[p, t] = tile_operands(p, t)[1].T

        def logits(p, t):
            q, _, _, pen = tile_operands(p, t)
            s = jnp.dot(q, kt_s[p, t], preferred_element_type=F32)
            s = s + bias_ref[hh, p]
            s_s[p, t] = s if pen is None else s + pen

        def softmax(p, t):
            s = s_s[p, t]
            m = jnp.max(s, axis=-1, keepdims=True)
            e_s[p, t] = jnp.exp(s - m).astype(BF16)
            store_rows(p, t, jnp.broadcast_to(m, (RADIUS, HEAD_DIM)), 1, m_s)

        def values(p, t):
            _, _, v, _ = tile_operands(p, t)
            nv = jnp.dot(e_s[p, t], jnp.concatenate([v, ones], axis=1), preferred_element_type=F32)
            store_rows(p, t, nv[:, :HEAD_DIM], 0, num_s)
            store_rows(p, t, nv[:, HEAD_DIM:], 2, l_s)

        def merge():
            for r in range(DIL_VIEW):
                strided = pl.ds(r, VIEW_ROWS, stride=DIL_VIEW)
                nums = (pos_s[0, strided, :], num_s[0, :, _lanes(r)], num_s[1, :, _lanes(r)])
                ms = (pos_s[1, strided, :], m_s[0, :, _lanes(r)], m_s[1, :, _lanes(r)])
                ls = (pos_s[2, strided, :], l_s[0, :, _lanes(r)], l_s[1, :, _lanes(r)])
                m_max = jnp.maximum(jnp.maximum(ms[0], ms[1]), ms[2])
                ws = [jnp.exp(m - m_max) for m in ms]
                num = ws[0] * nums[0] + ws[1] * nums[1] + ws[2] * nums[2]
                den = ws[0] * ls[0] + ws[1] * ls[1] + ws[2] * ls[2]
                o_ref[hh, strided, :] = num / den

        def all_tiles(p, *tile_stages):
            return lambda: [stage(p, t) for stage in tile_stages for t in range(TILES)]

        return [all_tiles(2, transpose_keys, logits), all_tiles(1, transpose_keys, logits),
                all_tiles(2, softmax), all_tiles(2, values),
                all_tiles(0, transpose_keys, logits), all_tiles(1, softmax), all_tiles(1, values),
                all_tiles(0, softmax), all_tiles(0, values), merge]

    first_head, second_head = head_passes(0), head_passes(1)
    for run in first_head[:7] + second_head[:2] + first_head[7:9] + second_head[2:5] + first_head[9:] + second_head[5:]:
        run()


def _dilated_attention(qkv, qkv16, bias, seq_lens):
    nt = qkv.shape[2]
    nblk = nt // DIL_BLOCK
    bounds = np.cumsum((0,) + seq_lens) // DIL_BLOCK
    first_blocks = tuple(int(b) for b in bounds[:-1])
    last_blocks = tuple(int(b) - 1 for b in bounds[1:])
    per = DIL_BLOCK // RADIUS
    prev1 = lambda n: jnp.maximum(per * n - 1, 0)
    next1 = lambda n: jnp.minimum(per * (n + 1), nt // RADIUS - 1)
    prev16 = lambda n: jnp.maximum(n - 1, 0)
    next16 = lambda n: jnp.minimum(n + 1, nblk - 1)
    kv_blk = KV_BLOCK
    hps = DIL_HEADS_PER_STEP
    blk16 = (hps, 1, VIEW_ROWS, DIL_VIEW * HEAD_DIM)
    kv16 = (hps, 2, VIEW_ROWS, DIL_VIEW * HEAD_DIM)
    npat = len(DILATIONS)
    kernel = functools.partial(_dil_kernel, first_blocks=first_blocks, last_blocks=last_blocks)
    out = pl.pallas_call(
        kernel,
        grid=(N_HEADS_A // hps, nblk),
        in_specs=[pl.BlockSpec((hps, 1, DIL_BLOCK, HEAD_DIM), lambda h, n: (h, SLOT_Q, n, 0)),
                  pl.BlockSpec((hps, 2, RADIUS, HEAD_DIM), lambda h, n: (h, kv_blk, prev1(n), 0)),
                  pl.BlockSpec((hps, 2, DIL_BLOCK, HEAD_DIM), lambda h, n: (h, kv_blk, n, 0)),
                  pl.BlockSpec((hps, 2, RADIUS, HEAD_DIM), lambda h, n: (h, kv_blk, next1(n), 0)),
                  pl.BlockSpec(blk16, lambda h, n: (h, SLOT_Q, n, 0)),
                  pl.BlockSpec(kv16, lambda h, n: (h, kv_blk, prev16(n), 0)),
                  pl.BlockSpec(kv16, lambda h, n: (h, kv_blk, n, 0)),
                  pl.BlockSpec(kv16, lambda h, n: (h, kv_blk, next16(n), 0)),
                  pl.BlockSpec((hps, npat, RADIUS, 3 * RADIUS), lambda h, n: (h, 0, 0, 0))],
        out_specs=pl.BlockSpec((hps, DIL_BLOCK, HEAD_DIM), lambda h, n: (h, n, 0)),
        out_shape=jax.ShapeDtypeStruct((N_HEADS_A, nt, HEAD_DIM), F32),
        scratch_shapes=[pltpu.VMEM((hps, npat, TILES, RADIUS, 3 * RADIUS), F32),
                        pltpu.VMEM((hps, npat, TILES, RADIUS, 3 * RADIUS), BF16),
                        pltpu.VMEM((hps, 3, DIL_BLOCK, HEAD_DIM), F32),
                        pltpu.VMEM((hps, 2, VIEW_ROWS, DIL_VIEW * HEAD_DIM), F32),
                        pltpu.VMEM((hps, 2, VIEW_ROWS, DIL_VIEW * HEAD_DIM), F32),
                        pltpu.VMEM((hps, 2, VIEW_ROWS, DIL_VIEW * HEAD_DIM), F32),
                        pltpu.VMEM((hps, npat, TILES, HEAD_DIM, 3 * RADIUS), BF16)],
        compiler_params=_cparams(("parallel", "parallel"), 48),
        name="dilated_attention",
    )(qkv, qkv, qkv, qkv, qkv16, qkv16, qkv16, qkv16, bias)
    return out


def _na_bias(rpb):
    c = np.arange(GRID_W)
    cstart = np.clip(c - NA_COLS // 2, 0, GRID_W - NA_COLS)
    cmask = (c[None, :] >= cstart[:, None]) & (c[None, :] < cstart[:, None] + NA_COLS)
    coff = np.clip(c[None, :] - c[:, None], -(NA_COLS - 1), NA_COLS - 1) + NA_COLS - 1
    cols = jnp.full(rpb.shape[:2] + (GRID_W, GRID_W), NEG, F32)
    for o in range(2 * NA_COLS - 1):
        cols = jnp.where((cmask & (coff == o))[None, None], rpb[:, :, o, None, None].astype(F32), cols)
    tiles = [cols[:, NA_ROWS - 1 - dd:2 * NA_ROWS - 1 - dd] for dd in range(NA_ROWS)]
    b = jnp.stack(tiles, axis=1).transpose(0, 1, 3, 2, 4)
    return b.reshape(rpb.shape[0], NA_ROWS, GRID_W, NA_ROWS * GRID_W)


def _na_kernel(q_blk, kvp_blk, kvc_blk, kvn_blk, bias_ref, o_ref, kw, vw,
               s_s, e_s, *, block_starts, seq_rows):
    n = pl.program_id(1)
    nstart = jnp.int32(block_starts[0])
    rows = jnp.int32(seq_rows[0])
    for s, r in zip(block_starts[1:], seq_rows[1:]):
        nstart = jnp.where(n >= s, s, nstart)
        rows = jnp.where(n >= s, r, rows)

    nkeys = NA_ROWS * GRID_W
    shifts, starts = [], []
    for i in range(NA_BLOCK_ROWS):
        r = (n - nstart) * NA_BLOCK_ROWS + i
        dd = r - jnp.clip(r - NA_ROWS // 2, 0, rows - NA_ROWS)
        shifts.append(dd)
        starts.append(pl.multiple_of((NA_HALO_ROWS + i - dd) * GRID_W, GRID_W))
    ones = jnp.ones((nkeys, HEAD_DIM), BF16)

    def head_passes(hh):
        q_ref = q_blk.at[hh, 0]

        def logits_pass():
            for which, win in enumerate((kw, vw)):
                win[hh, 0:NA_HALO, :] = kvp_blk[hh, which]
                win[hh, NA_HALO:NA_HALO + NA_BLOCK, :] = kvc_blk[hh, which]
                win[hh, NA_HALO + NA_BLOCK:NA_WINDOW, :] = kvn_blk[hh, which]
            for i in range(NA_BLOCK_ROWS):
                k = kw[hh, pl.ds(starts[i], nkeys), :]
                q = q_ref[i * GRID_W:(i + 1) * GRID_W, :]
                s = lax.dot_general(q, k, (((1,), (1,)), ((), ())), preferred_element_type=F32)
                s_s[hh, i] = s + bias_ref[hh, shifts[i]]

        def softmax_pass():
            for i in range(NA_BLOCK_ROWS):
                s = s_s[hh, i]
                e_s[hh, i] = jnp.exp(s - jnp.max(s, axis=-1, keepdims=True)).astype(BF16)

        def values_pass():
            for i in range(NA_BLOCK_ROWS):
                v = vw[hh, pl.ds(starts[i], nkeys), :]
                nv = jnp.dot(e_s[hh, i], jnp.concatenate([v, ones], axis=1), preferred_element_type=F32)
                out = (nv[:, :HEAD_DIM] / nv[:, HEAD_DIM:]).astype(o_ref.dtype)
                o_ref[i * GRID_W:(i + 1) * GRID_W, _lanes(hh)] = out

        return [logits_pass, softmax_pass, values_pass]

    first_head, second_head = head_passes(0), head_passes(1)
    for run in (first_head[0], second_head[0], first_head[1], first_head[2], second_head[1], second_head[2]):
        run()


def _neighborhood_attention(qkv, bias, seq_lens):
    nt = qkv.shape[2]
    block_starts = tuple(int(s) // NA_BLOCK for s in np.cumsum((0,) + seq_lens[:-1]))
    seq_rows = tuple(int(s) // GRID_W for s in seq_lens)
    assert all(r >= NA_ROWS and r % NA_BLOCK_ROWS == 0 for r in seq_rows)
    per = NA_BLOCK // NA_HALO
    prev = lambda n: jnp.maximum(per * n - 1, 0)
    nxt = lambda n: jnp.minimum(per * (n + 1), nt // NA_HALO - 1)
    kv_blk = KV_BLOCK
    kernel = functools.partial(_na_kernel, block_starts=block_starts, seq_rows=seq_rows)
    nkeys = NA_ROWS * GRID_W
    hps = NA_HEADS_PER_STEP
    return pl.pallas_call(
        kernel,
        grid=(N_HEADS_B // hps, nt // NA_BLOCK),
        in_specs=[pl.BlockSpec((hps, 1, NA_BLOCK, HEAD_DIM), lambda h, n: (h, SLOT_Q, n, 0)),
                  pl.BlockSpec((hps, 2, NA_HALO, HEAD_DIM), lambda h, n: (h, kv_blk, prev(n), 0)),
                  pl.BlockSpec((hps, 2, NA_BLOCK, HEAD_DIM), lambda h, n: (h, kv_blk, n, 0)),
                  pl.BlockSpec((hps, 2, NA_HALO, HEAD_DIM), lambda h, n: (h, kv_blk, nxt(n), 0)),
                  pl.BlockSpec((hps, NA_ROWS, GRID_W, nkeys), lambda h, n: (h, 0, 0, 0))],
        out_specs=pl.BlockSpec((NA_BLOCK, hps * HEAD_DIM), lambda h, n: (n, h)),
        out_shape=jax.ShapeDtypeStruct((nt, D_B), BF16),
        scratch_shapes=[pltpu.VMEM((hps, NA_WINDOW, HEAD_DIM), BF16),
                        pltpu.VMEM((hps, NA_WINDOW, HEAD_DIM), BF16),
                        pltpu.VMEM((hps, NA_BLOCK_ROWS, GRID_W, nkeys), F32),
                        pltpu.VMEM((hps, NA_BLOCK_ROWS, GRID_W, nkeys), BF16)],
        compiler_params=_cparams(("parallel", "parallel"), 40),
        name="neighborhood_attention",
    )(qkv, qkv, qkv, qkv, bias)


def _plain_norm(y, g):
    y = y.astype(F32)
    return (y * lax.rsqrt(jnp.mean(y * y, axis=-1, keepdims=True) + EPS) * g).astype(BF16)


OUT_ROW_CHUNK = 256


def _out_kernel(ya_ref, yb_ref, ga_ref, gb_ref, w_ref, x_ref, mod_ref, g2_ref, xo_ref, h_ref):
    for r0 in range(0, x_ref.shape[0], OUT_ROW_CHUNK):
        rows = slice(r0, r0 + OUT_ROW_CHUNK)
        ya = jnp.concatenate([ya_ref[h, rows, :] for h in range(N_HEADS_A)], axis=1)
        y = jnp.dot(_plain_norm(ya, ga_ref[...]), w_ref[:D_A, :], preferred_element_type=F32)
        y = y + jnp.dot(_plain_norm(yb_ref[rows, :], gb_ref[...]), w_ref[D_A:, :],
                        preferred_element_type=F32)
        x = x_ref[rows, :] + mod_ref[0, 2:3, :] * y
        xo_ref[rows, :] = x
        h = _modulated_norm(x, g2_ref[...], mod_ref[0, 3:4, :], mod_ref[0, 4:5, :])
        h_ref[rows, :] = h.astype(BF16)


def _out_proj(ya, yb, ga, gb, w, layer, x, mod_l, g2, seq_lens):
    nt, d = x.shape
    tm = 512
    starts = tuple(int(s) // tm for s in np.cumsum((0,) + seq_lens[:-1]))
    const = lambda i: (0, 0)
    return pl.pallas_call(
        _out_kernel,
        grid=(nt // tm,),
        in_specs=[pl.BlockSpec((N_HEADS_A, tm, HEAD_DIM), lambda i: (0, i, 0)),
                  pl.BlockSpec((tm, D_B), lambda i: (i, 0)),
                  pl.BlockSpec((1, D_A), const),
                  pl.BlockSpec((1, D_B), const),
                  pl.BlockSpec((None, D_A + D_B, d), lambda i: (layer, 0, 0), pipeline_mode=pl.Buffered(1)),
                  pl.BlockSpec((tm, d), lambda i: (i, 0)),
                  pl.BlockSpec((1, 6, d), lambda i: (_seq_index(i, starts), 0, 0)),
                  pl.BlockSpec((1, d), const)],
        out_specs=[pl.BlockSpec((tm, d), lambda i: (i, 0)),
                   pl.BlockSpec((tm, d), lambda i: (i, 0))],
        out_shape=[jax.ShapeDtypeStruct((nt, d), F32), jax.ShapeDtypeStruct((nt, d), BF16)],
        compiler_params=_cparams(("parallel",), 48),
        name="out_proj",
    )(ya, yb, ga, gb, w, x, mod_l, g2)


def _ffn_kernel(h_ref, wg_ref, wu_ref, wd_ref, x_ref, mod_ref, ng_ref, *refs, split_block):
    acc_ref = refs[-1]
    f = pl.program_id(1)

    @pl.when((f == 0) & (pl.program_id(0) == 0))
    def _():
        acc_ref[...] = jnp.zeros_like(acc_ref)

    h = h_ref[...]
    a = jnp.dot(h, wg_ref[...], preferred_element_type=F32)
    b = jnp.dot(h, wu_ref[...], preferred_element_type=F32)
    act = (a / (1.0 + jnp.exp(-a)) * b).astype(BF16)
    acc_ref[...] += jnp.dot(act, wd_ref[...], preferred_element_type=F32)

    @pl.when(f == pl.num_programs(1) - 1)
    def _():
        x = x_ref[...] + mod_ref[0, 5:6, :] * acc_ref[...]
        acc_ref[...] = jnp.zeros_like(acc_ref)
        if split_block is not None:
            y0_ref, y1_ref = refs[:-1]
            y = x * lax.rsqrt(jnp.mean(x * x, axis=-1, keepdims=True) + EPS) * ng_ref[...]
            i = pl.program_id(0)

            @pl.when(i < split_block)
            def _():
                y0_ref[...] = y

            @pl.when(i >= split_block)
            def _():
                y1_ref[...] = y
        else:
            nmod_ref, xo_ref, hn_ref = refs[:-1]
            xo_ref[...] = x
            hn = _modulated_norm(x, ng_ref[...], nmod_ref[0, 0:1, :], nmod_ref[0, 1:2, :])
            hn_ref[...] = hn.astype(BF16)


def _ffn(h, wg, wu, wd, layer, x, mod_l, next_g, next_mod, seq_lens, out_split=None):
    nt, d = x.shape
    dff = wg.shape[2]
    tm, tf = 512, 512
    final = next_mod is None
    split_block = out_split // tm if final else None
    starts = tuple(int(s) // tm for s in np.cumsum((0,) + seq_lens[:-1]))
    row_blk = pl.BlockSpec((tm, d), lambda i, f: (i, 0))
    mod_blk = pl.BlockSpec((1, 6, d), lambda i, f: (_seq_index(i, starts), 0, 0))
    in_specs = [row_blk,
                pl.BlockSpec((None, d, tf), lambda i, f: (layer, 0, f)),
                pl.BlockSpec((None, d, tf), lambda i, f: (layer, 0, f)),
                pl.BlockSpec((None, tf, d), lambda i, f: (layer, f, 0)),
                row_blk,
                mod_blk,
                pl.BlockSpec((1, d), lambda i, f: (0, 0))]
    operands = [h, wg, wu, wd, x, mod_l, next_g]
    if final:
        out_specs = [pl.BlockSpec((tm, d), lambda i, f: (jnp.minimum(i, split_block - 1), 0)),
                     pl.BlockSpec((tm, d), lambda i, f: (jnp.maximum(i - split_block, 0), 0))]
        out_shape = [jax.ShapeDtypeStruct((out_split, d), F32),
                     jax.ShapeDtypeStruct((nt - out_split, d), F32)]
    else:
        in_specs.append(mod_blk)
        operands.append(next_mod)
        out_specs = [row_blk, row_blk]
        out_shape = [jax.ShapeDtypeStruct((nt, d), F32), jax.ShapeDtypeStruct((nt, d), BF16)]
    return pl.pallas_call(
        functools.partial(_ffn_kernel, split_block=split_block),
        grid=(nt // tm, dff // tf),
        in_specs=in_specs,
        out_specs=out_specs,
        out_shape=out_shape,
        scratch_shapes=[pltpu.VMEM((tm, d), F32)],
        compiler_params=_cparams(("arbitrary", "arbitrary"), 56),
        name="ffn_final" if final else "ffn",
    )(*operands)


def _trunk(xa, xb, c, seq_lens, t5_table, norm1_g, norm2_g, w_ada, b_ada, w_in, out_norm_a,
           out_norm_b, w_out, na_rpb, w_gate, w_up, w_down, final_g):
    depth = w_in.shape[0]
    nseq = len(seq_lens)
    d = xa.shape[1]
    out_split = xa.shape[0]
    assert all(s % DIL_BLOCK == 0 for s in seq_lens)

    c_pad = jnp.zeros((8, d), F32).at[:nseq].set(c)
    mod = _ada_modulation(c_pad, w_ada, b_ada)[:, :nseq].reshape(depth, nseq, 6, d)

    col_scale = jnp.ones((6, D_A), F32).at[0].set(QK_SCALE).at[3].set(QK_SCALE).reshape(1, 6 * D_A)
    dil_bias = _dilated_bias(t5_table)
    w_in, w_out, w_gate, w_up, w_down = (w.astype(BF16) for w in (w_in, w_out, w_gate, w_up, w_down))
    x, h = _prenorm(xa, xb, mod[0], norm1_g[0][None], seq_lens)
    for l in range(depth):
        final = l == depth - 1
        qkv_a, qkv_a16 = _qkv_proj(h, col_scale, w_in, l, 0, True)
        qkv_b, = _qkv_proj(h, col_scale, w_in, l, 1, False)
        ya = _dilated_attention(qkv_a, qkv_a16, dil_bias, seq_lens)
        yb = _neighborhood_attention(qkv_b, _na_bias(na_rpb[l]), seq_lens)
        x, h2 = _out_proj(ya, yb, out_norm_a[l][None], out_norm_b[l][None], w_out, l,
                          x, mod[l], norm2_g[l][None], seq_lens)
        next_g = final_g[None] if final else norm1_g[l + 1][None]
        next_mod = None if final else mod[l + 1]
        out = _ffn(h2, w_gate, w_up, w_down, l, x, mod[l], next_g, next_mod, seq_lens, out_split)
        if not final:
            x, h = out
    return out


def kernel(x_prompt, x_sample, c_prompt, c_sample, t5_table, norm1_g, norm2_g, w_ada, b_ada, w_in,
           out_norm_a, out_norm_b, w_out, na_rpb, w_gate, w_up, w_down, final_g):
    bp, tp, d = x_prompt.shape
    bs, ts, _ = x_sample.shape
    seq_lens = (tp,) * bp + (ts,) * bs
    c = jnp.concatenate([c_prompt, c_sample], axis=0)
    y_prompt, y_sample = _trunk(x_prompt.reshape(bp * tp, d), x_sample.reshape(bs * ts, d), c, seq_lens,
                                t5_table, norm1_g, norm2_g, w_ada, b_ada, w_in, out_norm_a, out_norm_b,
                                w_out, na_rpb, w_gate, w_up, w_down, final_g)
    return (y_prompt.reshape(bp, tp, d), y_sample.reshape(bs, ts, d))
```
